```python
import math
import jax
import jax.numpy as jnp
from jax import lax
import numpy as np

D_MODEL = 1024
BATCH = 32
SEQ = 256
DEPTH = 2
DEC_BATCH = 8
DEC_SEQ = 2048
PAST_LEN = 256

GRID_W = 64
EPS = 1e-6
ROPE_BASE = 10000.0
Q_BLOCK = 128
N_BRANCH = 4
BRANCH_W = D_MODEL // 4
NA_HEADS = 4
NA_HEAD_DIM = 64
NA_KH = 8
NA_KW = 16
MLA_HEADS = 4
MLA_NOPE = 64
MLA_ROPE = 32
MLA_V = 64
MLA_Q_RANK = 384
MLA_KV_RANK = 256
DF_HEADS = 4
DF_HD = 32
DF_V = 64
SSM_HEADS = 4
SSM_P = 64
SSM_GROUPS = 2
SSM_N = 64
SSM_CONV = 3
SSM_CHUNK = 128
SSM_CONV_DIM = SSM_HEADS * SSM_P + 2 * SSM_GROUPS * SSM_N
N_EXPERTS = 16
EXPERT_FF = 1024
CAP_FACTOR = 2
IN_WIDTHS = (NA_HEADS * NA_HEAD_DIM, NA_HEADS * NA_HEAD_DIM, NA_HEADS * NA_HEAD_DIM,
             MLA_Q_RANK, MLA_KV_RANK, MLA_ROPE,
             DF_HEADS * 2 * DF_HD, DF_HEADS * 2 * DF_HD, DF_HEADS * DF_V,
             SSM_HEADS * SSM_P, SSM_HEADS * SSM_P, SSM_GROUPS * SSM_N, SSM_GROUPS * SSM_N, 2 * SSM_HEADS,
             N_BRANCH * D_MODEL)
IN_WIDTH = sum(IN_WIDTHS)

kernel_name = 'hybrid_diffusion_prefix_trunk_step'


def rmsnorm(x, g):
    xf = x.astype(jnp.float32)
    y = xf * lax.rsqrt(jnp.mean(xf * xf, axis=-1, keepdims=True) + EPS)
    return (y * g.astype(jnp.float32)).astype(x.dtype)


def axial_rope(x):
    L, d = x.shape[1], x.shape[-1]
    quarter = d // 4
    t = jnp.arange(L)
    inv = ROPE_BASE ** (-jnp.arange(quarter, dtype=jnp.float32) / quarter)
    rows = (t // GRID_W).astype(jnp.float32)[:, None]
    cols = (t % GRID_W).astype(jnp.float32)[:, None]
    ang = jnp.concatenate([rows * inv, cols * inv], axis=-1)
    ang = ang.reshape((1, L) + (1,) * (x.ndim - 3) + (d // 2,))
    cos = jnp.cos(ang).astype(x.dtype)
    sin = jnp.sin(ang).astype(x.dtype)
    xr, xi = x[..., 0::2], x[..., 1::2]
    return jnp.stack([xr * cos - xi * sin, xr * sin + xi * cos], axis=-1).reshape(x.shape)


def sweep_query_blocks(fn, *qs):
    b, L = qs[0].shape[:2]
    nb = L // Q_BLOCK
    blocks = tuple(q.reshape((b, nb, Q_BLOCK) + q.shape[2:]).swapaxes(0, 1) for q in qs)
    out = lax.map(lambda blk: fn(*blk), blocks).swapaxes(0, 1)
    return out.reshape((b, L) + out.shape[3:])


def softmax_attention(q, k, v):
    scale = q.shape[-1] ** -0.5

    def block(qb):
        s = jnp.einsum('bqhd,bkhd->bhqk', qb, k).astype(jnp.float32) * scale
        p = jax.nn.softmax(s, axis=-1).astype(v.dtype)
        return jnp.einsum('bhqk,bkhd->bqhd', p, v)

    return sweep_query_blocks(block, q)


def mla_attention(q_nope, q_rope, k_nope, k_rope, v):
    scale = (MLA_NOPE + MLA_ROPE) ** -0.5

    def block(qn, qr):
        s = jnp.einsum('bqhd,bkhd->bhqk', qn, k_nope) + jnp.einsum('bqhr,bkr->bhqk', qr, k_rope)
        p = jax.nn.softmax(s.astype(jnp.float32) * scale, axis=-1).astype(v.dtype)
        return jnp.einsum('bhqk,bkhd->bqhd', p, v)

    return sweep_query_blocks(block, q_nope, q_rope)


def diff_attention(q, k, v, lam):
    scale = DF_HD ** -0.5

    def block(qb):
        s = jnp.einsum('bqhcd,bkhcd->bchqk', qb, k).astype(jnp.float32) * scale
        p = jax.nn.softmax(s, axis=-1)
        w = (p[:, 0] - lam * p[:, 1]).astype(v.dtype)
        return jnp.einsum('bhqk,bkhd->bqhd', w, v)

    return sweep_query_blocks(block, q)


def neighborhood_attention(q, k, v, k_ctx, v_ctx, rel_bias):
    b, L, H, Dh = q.shape
    rows = L // GRID_W
    kh = min(NA_KH, rows)
    ncb = GRID_W // NA_KW
    band_w = 2 * NA_KW
    r = np.arange(rows)
    row_start = np.clip(r - kh // 2, 0, rows - kh)
    key_rows = row_start[:, None] + np.arange(kh)
    band_start = np.clip(np.arange(ncb) * NA_KW - NA_KW // 2, 0, GRID_W - band_w)
    key_cols = band_start[:, None] + np.arange(band_w)
    nk = kh * band_w
    key_idx = (key_rows[:, None, :, None] * GRID_W + key_cols[None, :, None, :]).reshape(rows, ncb, nk)
    q_cols = np.arange(ncb)[:, None] * NA_KW + np.arange(NA_KW)
    col_start = np.clip(q_cols - NA_KW // 2, 0, GRID_W - NA_KW)
    kc = key_cols[:, None, :]
    in_win = (kc >= col_start[..., None]) & (kc < col_start[..., None] + NA_KW)
    row_rel = (key_rows - r[:, None] + NA_KH - 1)[:, None, None, :, None]
    col_rel = np.clip(kc - q_cols[:, :, None] + NA_KW - 1, 0, 2 * NA_KW - 2)[None, :, :, None, :]
    bias = rel_bias[:, row_rel, col_rel].astype(jnp.float32)
    bias = jnp.where(in_win[None, None, :, :, None, :], bias, -1e30)
    bias = bias.reshape(H, rows, ncb, NA_KW, nk).transpose(1, 2, 0, 3, 4)
    qb = q.reshape(b, rows, ncb, NA_KW, H, Dh)
    kb = k[:, key_idx]
    vb = v[:, key_idx]
    scale = Dh ** -0.5
    s_loc = jnp.einsum('brjqhd,brjkhd->brjhqk', qb, kb).astype(jnp.float32) * scale + bias
    s_ctx = jnp.einsum('brjqhd,bkhd->brjhqk', qb, k_ctx).astype(jnp.float32) * scale
    p = jax.nn.softmax(jnp.concatenate([s_loc, s_ctx], axis=-1), axis=-1).astype(v.dtype)
    o = (jnp.einsum('brjhqk,brjkhd->brjqhd', p[..., :nk], vb)
         + jnp.einsum('brjhqk,bkhd->brjqhd', p[..., nk:], v_ctx))
    return o.reshape(b, L, H * Dh)


def depthwise_conv(x, w, bias):
    pad = SSM_CONV // 2
    y = lax.conv_general_dilated(x, w[:, None, :], window_strides=(1,), padding=[(pad, pad)],
                                 dimension_numbers=('NWC', 'WIO', 'NWC'), feature_group_count=x.shape[-1])
    return y + bias


def ssd_chunked(x, dt, A, B, C, h0):
    b, L, H, P = x.shape
    G, N = B.shape[2], B.shape[3]
    nc = L // SSM_CHUNK
    f32 = jnp.float32
    xc = x.astype(f32).reshape(b, nc, SSM_CHUNK, H, P)
    Bc = jnp.repeat(B.astype(f32), H // G, axis=2).reshape(b, nc, SSM_CHUNK, H, N)
    Cc = jnp.repeat(C.astype(f32), H // G, axis=2).reshape(b, nc, SSM_CHUNK, H, N)
    dtc = dt.reshape(b, nc, SSM_CHUNK, H)
    xdt = xc * dtc[..., None]
    a_cum = jnp.cumsum(jnp.moveaxis(dtc * A, 3, 1), axis=-1)
    causal = np.tril(np.ones((SSM_CHUNK, SSM_CHUNK), dtype=bool))
    seg = a_cum[..., :, None] - a_cum[..., None, :]
    decay = jnp.where(causal, jnp.exp(jnp.where(causal, seg, 0.0)), 0.0)
    scores = jnp.einsum('bclhn,bcshn->bhcls', Cc, Bc) * decay
    y_diag = jnp.einsum('bhcls,bcshp->bclhp', scores, xdt)
    decay_to_end = jnp.exp(a_cum[..., -1:] - a_cum)
    chunk_states = jnp.einsum('bclhn,bhcl,bclhp->bchpn', Bc, decay_to_end, xdt)
    chunk_decay = jnp.exp(a_cum[..., -1])

    def carry_state(h, inp):
        s, d = inp
        return h * d[..., None, None] + s, h

    h_final, h_start = lax.scan(carry_state, h0.astype(f32),
                                (jnp.moveaxis(chunk_states, 1, 0), jnp.moveaxis(chunk_decay, 2, 0)))
    h_start = jnp.moveaxis(h_start, 0, 1)
    y_off = jnp.einsum('bclhn,bchpn,bhcl->bclhp', Cc, h_start, jnp.exp(a_cum))
    y = (y_diag + y_off).reshape(b, L, H, P).astype(x.dtype)
    return y, h_final.astype(x.dtype)


def ssd_mixer(s_x, s_z, s_b, s_c, s_dt, h0, lp):
    b, L, _ = s_x.shape
    xbc = jax.nn.silu(depthwise_conv(jnp.concatenate([s_x, s_b, s_c], axis=-1), lp['ssm_conv_w'], lp['ssm_conv_b']))
    xs, bm, cm = jnp.split(xbc, [SSM_HEADS * SSM_P, SSM_HEADS * SSM_P + SSM_GROUPS * SSM_N], axis=-1)
    x4 = xs.reshape(b, L, SSM_HEADS, SSM_P)
    b4 = bm.reshape(b, L, SSM_GROUPS, SSM_N)
    c4 = cm.reshape(b, L, SSM_GROUPS, SSM_N)
    dt = jax.nn.softplus(s_dt.astype(jnp.float32).reshape(b, L, 2, SSM_HEADS) + lp['ssm_dt_bias'].astype(jnp.float32))
    A = -jnp.exp(lp['ssm_a_log'].astype(jnp.float32))

    def rev(a):
        return jnp.flip(a, axis=1)

    y_f, h_f = ssd_chunked(x4, dt[:, :, 0], A[0], b4, c4, h0[:, 0])
    y_b, h_b = ssd_chunked(rev(x4), rev(dt[:, :, 1]), A[1], rev(b4), rev(c4), h0[:, 1])
    y = y_f + rev(y_b) + lp['ssm_d'][:, None] * x4
    y = rmsnorm(y.reshape(b, L, SSM_HEADS * SSM_P) * jax.nn.silu(s_z), lp['ssm_norm'])
    return y, jnp.stack([h_f, h_b], axis=1)


def mixer_sublayer(h, lp, layer_idx, ctx):
    b, L, _ = h.shape
    latent = ctx is not None
    offsets = np.cumsum(IN_WIDTHS)[:-1].tolist()
    (na_q, na_k, na_v, mla_cq, mla_ckv, mla_kr, df_q, df_k, df_v,
     s_x, s_z, s_b, s_c, s_dt, gate_logits) = jnp.split(h @ lp['w_in'], offsets, axis=-1)

    na_q = na_q.reshape(b, L, NA_HEADS, NA_HEAD_DIM)
    na_k = na_k.reshape(b, L, NA_HEADS, NA_HEAD_DIM)
    na_v = na_v.reshape(b, L, NA_HEADS, NA_HEAD_DIM)
    if latent:
        na_out = neighborhood_attention(na_q, na_k, na_v, ctx['na_k'], ctx['na_v'], lp['na_rel_bias'])
    else:
        na_out = softmax_attention(na_q, na_k, na_v).reshape(b, L, NA_HEADS * NA_HEAD_DIM)

    q = (rmsnorm(mla_cq, lp['mla_q_norm']) @ lp['mla_w_uq']).reshape(b, L, MLA_HEADS, MLA_NOPE + MLA_ROPE)
    q_nope, q_rope = q[..., :MLA_NOPE], q[..., MLA_NOPE:]
    c_kv = rmsnorm(mla_ckv, lp['mla_kv_norm'])
    k_rope = mla_kr
    if latent:
        q_rope = axial_rope(q_rope)
        ckv_all = jnp.concatenate([c_kv, ctx['mla_ckv']], axis=1)
        krope_all = jnp.concatenate([axial_rope(k_rope), ctx['mla_krope']], axis=1)
    else:
        ckv_all, krope_all = c_kv, k_rope
    kv = (ckv_all @ lp['mla_w_ukv']).reshape(b, ckv_all.shape[1], MLA_HEADS, MLA_NOPE + MLA_V)
    mla_out = mla_attention(q_nope, q_rope, kv[..., :MLA_NOPE], krope_all, kv[..., MLA_NOPE:])
    mla_out = mla_out.reshape(b, L, MLA_HEADS * MLA_V)

    dq = df_q.reshape(b, L, DF_HEADS, 2, DF_HD)
    dk = df_k.reshape(b, L, DF_HEADS, 2, DF_HD)
    dv = df_v.reshape(b, L, DF_HEADS, DF_V)
    if latent:
        dq = axial_rope(dq)
        dk_all = jnp.concatenate([axial_rope(dk), ctx['df_k']], axis=1)
        dv_all = jnp.concatenate([dv, ctx['df_v']], axis=1)
    else:
        dk_all, dv_all = dk, dv
    lam_init = 0.8 - 0.6 * math.exp(-0.3 * layer_idx)
    lv = lp['df_lambda'].astype(jnp.float32)
    lam = jnp.exp(jnp.sum(lv[0] * lv[1])) - jnp.exp(jnp.sum(lv[2] * lv[3])) + lam_init
    df_o = diff_attention(dq, dk_all, dv_all, lam)
    df_out = (rmsnorm(df_o, lp['df_subln']) * (1.0 - lam_init)).reshape(b, L, DF_HEADS * DF_V)

    h0 = ctx['ssm'] if latent else jnp.zeros((b, 2, SSM_HEADS, SSM_P, SSM_N), h.dtype)
    ssm_out, ssm_state = ssd_mixer(s_x, s_z, s_b, s_c, s_dt, h0, lp)

    gates = jax.nn.sigmoid(gate_logits.astype(jnp.float32)).astype(h.dtype).reshape(b, L, N_BRANCH, D_MODEL)
    br = jnp.stack([na_out, mla_out, df_out, ssm_out], axis=2)
    proj = jnp.einsum('blnw,nwd->blnd', br, lp['w_branch'])
    out = jnp.sum(gates * proj, axis=2) @ lp['w_out']
    new_ctx = None if latent else {'na_k': na_k, 'na_v': na_v, 'mla_ckv': c_kv, 'mla_krope': k_rope,
                                   'df_k': dk, 'df_v': dv, 'ssm': ssm_state}
    return out, new_ctx


def expert_choice_ffn(h, router_w, w_gate, w_up, w_down):
    b, L, d = h.shape
    n = b * L
    cap = (CAP_FACTOR * n) // N_EXPERTS
    tok = h.reshape(n, d)
    aff = jax.nn.softmax((tok @ router_w).astype(jnp.float32), axis=-1)
    top_aff, top_idx = lax.top_k(aff.T, cap)
    xe = tok[top_idx]
    hid = jax.nn.silu(jnp.einsum('ecd,edf->ecf', xe, w_gate)) * jnp.einsum('ecd,edf->ecf', xe, w_up)
    ye = jnp.einsum('ecf,efd->ecd', hid, w_down) * top_aff[..., None].astype(h.dtype)
    out = jnp.zeros_like(tok).at[top_idx.reshape(-1)].add(ye.reshape(-1, d))
    return out.reshape(b, L, d)


def trunk_layer(x, c_vec, lp, layer_idx, ctx):
    m = jax.nn.silu(c_vec) @ lp['mod_w'] + lp['mod_b']
    sh1, sc1, g1, sh2, sc2, g2 = [t[:, None, :] for t in jnp.split(m, 6, axis=-1)]
    h = rmsnorm(x, lp['norm1']) * (1 + sc1) + sh1
    mix, new_ctx = mixer_sublayer(h, lp, layer_idx, ctx)
    x = x + g1 * mix
    h = rmsnorm(x, lp['norm2']) * (1 + sc2) + sh2
    x = x + g2 * expert_choice_ffn(h, lp['router_w'], lp['exp_w_gate'], lp['exp_w_up'], lp['exp_w_down'])
    return x, new_ctx


def setup_inputs(seed: int = 0) -> dict:
    key = jax.random.key(seed)
    ks = jax.random.split(key, 40)
    cnt = [0]
    f32 = jnp.float32

    def nxt():
        cnt[0] += 1
        return ks[cnt[0] - 1]

    def nrm(shape, scale):
        return jax.random.normal(nxt(), shape, f32) * scale

    def gain(shape):
        return 1.0 + nrm(shape, 0.02)

    D = D_MODEL
    x_prompt = nrm((BATCH, SEQ, D), 1.0)
    x_sample = nrm((DEC_BATCH, DEC_SEQ, D), 1.0)
    cache_na_k = nrm((DEC_BATCH, DEPTH, PAST_LEN, NA_HEADS, NA_HEAD_DIM), 1.0)
    cache_na_v = nrm((DEC_BATCH, DEPTH, PAST_LEN, NA_HEADS, NA_HEAD_DIM), 1.0)
    cache_mla_ckv = nrm((DEC_BATCH, DEPTH, PAST_LEN, MLA_KV_RANK), 1.0)
    cache_mla_krope = nrm((DEC_BATCH, DEPTH, PAST_LEN, MLA_ROPE), 1.0)
    cache_df_k = nrm((DEC_BATCH, DEPTH, PAST_LEN, DF_HEADS, 2, DF_HD), 1.0)
    cache_df_v = nrm((DEC_BATCH, DEPTH, PAST_LEN, DF_HEADS, DF_V), 1.0)
    state_ssm = nrm((DEC_BATCH, DEPTH, 2, SSM_HEADS, SSM_P, SSM_N), 0.5)
    c = nrm((DEC_BATCH, D), 1.0)
    c_ctx = nrm((D,), 1.0)
    mod_w = nrm((DEPTH, D, 6 * D), 0.5 * D ** -0.5)
    mod_b = nrm((DEPTH, 6 * D), 0.02)
    norm1_g = gain((DEPTH, D))
    norm2_g = gain((DEPTH, D))
    w_in = nrm((DEPTH, D, IN_WIDTH), D ** -0.5)
    na_rel_bias = nrm((DEPTH, NA_HEADS, 2 * NA_KH - 1, 2 * NA_KW - 1), 0.1)
    mla_q_norm_g = gain((DEPTH, MLA_Q_RANK))
    mla_kv_norm_g = gain((DEPTH, MLA_KV_RANK))
    mla_w_uq = nrm((DEPTH, MLA_Q_RANK, MLA_HEADS * (MLA_NOPE + MLA_ROPE)), MLA_Q_RANK ** -0.5)
    mla_w_ukv = nrm((DEPTH, MLA_KV_RANK, MLA_HEADS * (MLA_NOPE + MLA_V)), MLA_KV_RANK ** -0.5)
    df_lambda = nrm((DEPTH, 4, DF_HD), 0.1)
    df_subln_g = gain((DEPTH, DF_V))
    ssm_conv_w = nrm((DEPTH, SSM_CONV, SSM_CONV_DIM), SSM_CONV ** -0.5)
    ssm_conv_b = nrm((DEPTH, SSM_CONV_DIM), 0.02)
    dt0 = jnp.exp(jax.random.uniform(nxt(), (DEPTH, 2, SSM_HEADS), f32, math.log(1e-3), math.log(1e-1)))
    ssm_dt_bias = dt0 + jnp.log(-jnp.expm1(-dt0))
    ssm_a_log = jnp.log(jax.random.uniform(nxt(), (DEPTH, 2, SSM_HEADS), f32, 1.0, 16.0))
    ssm_d = gain((DEPTH, SSM_HEADS))
    ssm_norm_g = gain((DEPTH, SSM_HEADS * SSM_P))
    w_branch = nrm((DEPTH, N_BRANCH, BRANCH_W, D), BRANCH_W ** -0.5)
    w_out = nrm((DEPTH, D, D), D ** -0.5)
    router_w = nrm((DEPTH, D, N_EXPERTS), D ** -0.5)
    exp_w_gate = nrm((DEPTH, N_EXPERTS, D, EXPERT_FF), D ** -0.5)
    exp_w_up = nrm((DEPTH, N_EXPERTS, D, EXPERT_FF), D ** -0.5)
    exp_w_down = nrm((DEPTH, N_EXPERTS, EXPERT_FF, D), EXPERT_FF ** -0.5)
    final_norm_g = gain((D,))
    return {'x_prompt': x_prompt, 'x_sample': x_sample,
            'cache_na_k': cache_na_k, 'cache_na_v': cache_na_v,
            'cache_mla_ckv': cache_mla_ckv, 'cache_mla_krope': cache_mla_krope,
            'cache_df_k': cache_df_k, 'cache_df_v': cache_df_v, 'state_ssm': state_ssm,
            'c': c, 'c_ctx': c_ctx, 'mod_w': mod_w, 'mod_b': mod_b, 'norm1_g': norm1_g, 'norm2_g': norm2_g,
            'w_in': w_in, 'na_rel_bias': na_rel_bias, 'mla_q_norm_g': mla_q_norm_g, 'mla_kv_norm_g': mla_kv_norm_g,
            'mla_w_uq': mla_w_uq, 'mla_w_ukv': mla_w_ukv, 'df_lambda': df_lambda, 'df_subln_g': df_subln_g,
            'ssm_conv_w': ssm_conv_w, 'ssm_conv_b': ssm_conv_b, 'ssm_dt_bias': ssm_dt_bias, 'ssm_a_log': ssm_a_log,
            'ssm_d': ssm_d, 'ssm_norm_g': ssm_norm_g, 'w_branch': w_branch, 'w_out': w_out, 'router_w': router_w,
            'exp_w_gate': exp_w_gate, 'exp_w_up': exp_w_up, 'exp_w_down': exp_w_down, 'final_norm_g': final_norm_g}


def reference(x_prompt, x_sample, cache_na_k, cache_na_v, cache_mla_ckv, cache_mla_krope, cache_df_k, cache_df_v,
              state_ssm, c, c_ctx, mod_w, mod_b, norm1_g, norm2_g, w_in, na_rel_bias, mla_q_norm_g, mla_kv_norm_g,
              mla_w_uq, mla_w_ukv, df_lambda, df_subln_g, ssm_conv_w, ssm_conv_b, ssm_dt_bias, ssm_a_log, ssm_d,
              ssm_norm_g, w_branch, w_out, router_w, exp_w_gate, exp_w_up, exp_w_down, final_norm_g):
    xp, xs = x_prompt, x_sample
    new_na_k, new_na_v, new_ckv, new_krope, new_df_k, new_df_v, new_ssm = [], [], [], [], [], [], []
    for l in range(DEPTH):
        lp = {'mod_w': mod_w[l], 'mod_b': mod_b[l], 'norm1': norm1_g[l], 'norm2': norm2_g[l], 'w_in': w_in[l],
              'na_rel_bias': na_rel_bias[l], 'mla_q_norm': mla_q_norm_g[l], 'mla_kv_norm': mla_kv_norm_g[l],
              'mla_w_uq': mla_w_uq[l], 'mla_w_ukv': mla_w_ukv[l], 'df_lambda': df_lambda[l], 'df_subln': df_subln_g[l],
              'ssm_conv_w': ssm_conv_w[l], 'ssm_conv_b': ssm_conv_b[l], 'ssm_dt_bias': ssm_dt_bias[l],
              'ssm_a_log': ssm_a_log[l], 'ssm_d': ssm_d[l], 'ssm_norm': ssm_norm_g[l], 'w_branch': w_branch[l],
              'w_out': w_out[l], 'router_w': router_w[l], 'exp_w_gate': exp_w_gate[l], 'exp_w_up': exp_w_up[l],
              'exp_w_down': exp_w_down[l]}
        xp, ctx_new = trunk_layer(xp, c_ctx[None, :], lp, l, None)
        new_na_k.append(ctx_new['na_k'])
        new_na_v.append(ctx_new['na_v'])
        new_ckv.append(ctx_new['mla_ckv'])
        new_krope.append(ctx_new['mla_krope'])
        new_df_k.append(ctx_new['df_k'])
        new_df_v.append(ctx_new['df_v'])
        new_ssm.append(ctx_new['ssm'])
        ctx_cached = {'na_k': cache_na_k[:, l], 'na_v': cache_na_v[:, l], 'mla_ckv': cache_mla_ckv[:, l],
                      'mla_krope': cache_mla_krope[:, l], 'df_k': cache_df_k[:, l], 'df_v': cache_df_v[:, l],
                      'ssm': state_ssm[:, l]}
        xs, _ = trunk_layer(xs, c, lp, l, ctx_cached)
    y_prompt = rmsnorm(xp, final_norm_g)
    y_sample = rmsnorm(xs, final_norm_g)
    return (y_prompt, y_sample, jnp.stack(new_na_k, axis=1), jnp.stack(new_na_v, axis=1),
            jnp.stack(new_ckv, axis=1), jnp.stack(new_krope, axis=1), jnp.stack(new_df_k, axis=1),
            jnp.stack(new_df_v, axis=1), jnp.stack(new_ssm, axis=1))
```

```python
import functools
import math

import numpy as np
import jax
import jax.numpy as jnp
from jax import lax
from jax.experimental import pallas as pl
from jax.experimental.pallas import tpu as pltpu

F32 = jnp.float32
BF16 = jnp.bfloat16

D = 1024
B_CTX, L_CTX = 32, 256
B_LAT, L_LAT = 8, 2048
PAST = 256
DEPTH = 2
GRID_W = 64
EPS = 1e-6
ROPE_BASE = 10000.0
N_HEADS = 4
NA_KH, NA_KW = 8, 16
MLA_NOPE, MLA_ROPE, MLA_V = 64, 32, 64
MLA_Q_RANK, MLA_KV_RANK = 384, 256
DF_HD, DF_V = 32, 64
SSM_P, SSM_N, SSM_GROUPS = 64, 64, 2
N_EXPERTS = 16
CAP_FACTOR = 2

LANES = 128
SUBLANES = 8
TM = 256
R_CTX = B_CTX * L_CTX
R_LAT = B_LAT * L_LAT
R_ALL = R_CTX + R_LAT
NT_CTX = R_CTX // TM
NT_LAT = R_LAT // TM
NT_ALL = R_ALL // TM
TILES_PER_LAT_BATCH = L_LAT // TM
TQ = 256
SSD_Q = 256
MOE_W = 64
MOE_W_LOG2 = 6
SELECT_BISECT_STEPS = 40
FFN_TM = 512
NA_QROWS = 4
NA_KROWS = NA_QROWS + NA_KH - 1
NA_NK = NA_KROWS * GRID_W
VMEM_LIMIT = 56 * 1024 * 1024


def _cparams(sem):
    return pltpu.CompilerParams(dimension_semantics=sem, vmem_limit_bytes=VMEM_LIMIT)


def _nt(a, b):
    return lax.dot_general(a, b, (((1,), (1,)), ((), ())), preferred_element_type=F32)


def _dot(a, b):
    return jnp.dot(a, b, preferred_element_type=F32)


def _rms(x, g):
    return x * lax.rsqrt(jnp.mean(x * x, axis=-1, keepdims=True) + EPS) * g


def _silu(x):
    return x * jax.nn.sigmoid(x)


def _softmax_rows(s):
    m = jnp.max(s, axis=-1, keepdims=True)
    e = jnp.exp(s - m)
    return e * (1.0 / jnp.sum(e, axis=-1, keepdims=True))


LOG2E = math.log2(math.e)


def _exp_rows(s, scale):
    m = jnp.max(s, axis=-1, keepdims=True)
    e = jnp.exp2((s - m) * (scale * LOG2E))
    return e, jnp.sum(e, axis=-1, keepdims=True)


def _split3(a):
    a1 = a.astype(BF16)
    r1 = a - a1.astype(F32)
    a2 = r1.astype(BF16)
    a3 = (r1 - a2.astype(F32)).astype(BF16)
    return a1, a2, a3


def _iota(shape, dim):
    return lax.broadcasted_iota(jnp.int32, shape, dim)


def _mod_row(i):
    return jnp.where(i < NT_CTX, 0, 1 + (i - NT_CTX) // TILES_PER_LAT_BATCH)


MOD_TN = 1536


def _mod_kernel(c_ref, w_ref, b_ref, o_ref):
    s = _silu(c_ref[...]).astype(BF16)
    o_ref[0] = _dot(s, w_ref[0].astype(BF16)) + b_ref[0]


def _modulation(cvec, mod_w, mod_b):
    n = 6 * D
    return pl.pallas_call(
        _mod_kernel,
        out_shape=jax.ShapeDtypeStruct((DEPTH, 16, n), F32),
        grid=(DEPTH, n // MOD_TN),
        in_specs=[pl.BlockSpec((16, D), lambda l, j: (0, 0)),
                  pl.BlockSpec((1, D, MOD_TN), lambda l, j: (l, 0, j)),
                  pl.BlockSpec((1, 1, MOD_TN), lambda l, j: (l, 0, j))],
        out_specs=pl.BlockSpec((1, 16, MOD_TN), lambda l, j: (l, 0, j)),
        compiler_params=_cparams(("parallel", "parallel")),
        name="modulation",
    )(cvec, mod_w, mod_b.reshape(DEPTH, 1, n))


W_NA, W_MLA, W_DF, W_SSM, W_GATE = 768, 768, 768, 896, 4 * D


W_IN_COLS = np.cumsum([0, W_NA, W_MLA, W_DF, W_SSM, W_GATE])
W_IN_ALL = int(W_IN_COLS[-1])


def _ctx_tile(i):
    return jnp.minimum(i, NT_CTX - 1)


def _lat_tile(i):
    return jnp.maximum(i - NT_CTX, 0)


def _pick_group(i, ctx_ref, lat_ref):
    return jnp.where(i < NT_CTX, ctx_ref[...], lat_ref[...])


def _in_kernel(xc_ref, xl_ref, mod_ref, g_ref, w_ref, wdt_t,
               ona, omla, odf, ossm, ogate, odt_t, onak, onav, odfk, odfv, okr):
    i = pl.program_id(0)
    x = _pick_group(i, xc_ref, xl_ref)
    h = _rms(x, g_ref[...]) * (1.0 + mod_ref[0, 1:2, :]) + mod_ref[0, 0:1, :]
    hb = h.astype(BF16)
    c = W_IN_COLS
    una = _dot(hb, w_ref[:, c[0]:c[1]])
    umla = _dot(hb, w_ref[:, c[1]:c[2]])
    udf = _dot(hb, w_ref[:, c[2]:c[3]])
    ona[...] = una
    omla[...] = umla
    odf[...] = udf
    ossm[...] = _dot(hb, w_ref[:, c[3]:c[4]])
    ogate[...] = jax.nn.sigmoid(_dot(hb, w_ref[:, c[4]:c[5]])).astype(BF16)
    odt_t[...] = _nt(wdt_t[...], hb)

    @pl.when(i < NT_CTX)
    def _():
        onak[...] = una[:, 256:512]
        onav[...] = una[:, 512:768]
        odfk[...] = udf[:, 256:512]
        odfv[...] = udf[:, 512:768]
        okr[...] = umla[:, 704:736]


def _in_proj(x_ctx, x_lat, mod_l, norm_g, w_all, w_dt_t):
    widths = (W_NA, W_MLA, W_DF, W_SSM)
    const = lambda i: (0, 0)
    row = lambda i: (i, 0)
    ctx_row = lambda i: (_ctx_tile(i), 0)
    out_shape = [jax.ShapeDtypeStruct((R_ALL, w), F32) for w in widths]
    out_shape += [jax.ShapeDtypeStruct((R_ALL, W_GATE), BF16), jax.ShapeDtypeStruct((8, R_ALL), F32)]
    out_shape += [jax.ShapeDtypeStruct((R_CTX, 256), F32)] * 4 + [jax.ShapeDtypeStruct((R_CTX, MLA_ROPE), F32)]
    out_specs = [pl.BlockSpec((TM, w), row) for w in widths]
    out_specs += [pl.BlockSpec((TM, W_GATE), row), pl.BlockSpec((8, TM), lambda i: (0, i))]
    out_specs += [pl.BlockSpec((TM, 256), ctx_row)] * 4 + [pl.BlockSpec((TM, MLA_ROPE), ctx_row)]
    return pl.pallas_call(
        _in_kernel,
        out_shape=out_shape,
        grid=(NT_ALL,),
        in_specs=[pl.BlockSpec((TM, D), ctx_row),
                  pl.BlockSpec((TM, D), lambda i: (_lat_tile(i), 0)),
                  pl.BlockSpec((1, 6, D), lambda i: (_mod_row(i), 0, 0)),
                  pl.BlockSpec((1, D), const),
                  pl.BlockSpec((D, W_IN_ALL), const, pipeline_mode=pl.Buffered(1)),
                  pl.BlockSpec((8, D), const, pipeline_mode=pl.Buffered(1))],
        out_specs=out_specs,
        compiler_params=_cparams(("arbitrary",)),
        name="in_proj",
    )(x_ctx, x_lat, mod_l, norm_g, w_all, w_dt_t)


def _rope(x, cos, sin_a, sin_b):
    n = x.shape[-1]
    nxt = pltpu.roll(x, n - 1, 1)
    prv = pltpu.roll(x, 1, 1)
    return x * cos + nxt * sin_a + prv * sin_b


def _rope_tables32():
    t = np.arange(L_LAT)
    quarter = 8
    inv = ROPE_BASE ** (-np.arange(quarter, dtype=np.float64) / quarter)
    rows = (t // GRID_W).astype(np.float64)[:, None]
    cols = (t % GRID_W).astype(np.float64)[:, None]
    ang = np.concatenate([rows * inv, cols * inv], axis=-1)
    cos = np.repeat(np.cos(ang), 2, axis=-1)
    sin = np.repeat(np.sin(ang), 2, axis=-1)
    even = (np.arange(32) % 2 == 0)[None, :]
    sin_a = np.where(even, -sin, 0.0)
    sin_b = np.where(even, 0.0, sin)
    return tuple(np.asarray(a, np.float32) for a in (cos, sin_a, sin_b))


def _attn_ctx_kernel(u_ref, o_ref):
    scale = 64 ** -0.5
    for h in range(N_HEADS):
        q = u_ref[:, 64 * h:64 * h + 64].astype(BF16)
        k = u_ref[:, 256 + 64 * h:256 + 64 * h + 64].astype(BF16)
        v = u_ref[:, 512 + 64 * h:512 + 64 * h + 64].astype(BF16)
        e, den = _exp_rows(_nt(q, k), scale)
        o_ref[:, 64 * h:64 * h + 64] = _dot((e * (1.0 / den)).astype(BF16), v)


def _attn_ctx(u_na):
    return pl.pallas_call(
        _attn_ctx_kernel,
        out_shape=jax.ShapeDtypeStruct((R_CTX, 256), F32),
        grid=(B_CTX,),
        in_specs=[pl.BlockSpec((L_CTX, W_NA), lambda b: (b, 0))],
        out_specs=pl.BlockSpec((L_CTX, 256), lambda b: (b, 0)),
        compiler_params=_cparams(("parallel",)),
        name="na_ctx",
    )(u_na)


NA_ROWS = L_LAT // GRID_W
NA_NQT = NA_ROWS // NA_QROWS


def _na_pattern(qt):
    return jnp.where(qt == 0, 0, jnp.where(qt == NA_NQT - 1, 2, 1))


def _na_pattern_offsets(p, a):
    q_off = jnp.where(p == 0, 0, jnp.where(p == 1, NA_KH // 2, NA_KROWS - NA_QROWS))
    rs_rel = jnp.where(p == 0, 0, jnp.where(p == 1, a, NA_KROWS - NA_KH))
    return q_off, rs_rel


def _check_na_patterns():
    for qt in range(NA_NQT):
        ks = int(np.clip(NA_QROWS * qt - NA_KH // 2, 0, NA_ROWS - NA_KROWS))
        p = 0 if qt == 0 else (2 if qt == NA_NQT - 1 else 1)
        for a in range(NA_QROWS):
            r = NA_QROWS * qt + a
            rs = int(np.clip(r - NA_KH // 2, 0, NA_ROWS - NA_KH))
            q_off = (0, NA_KH // 2, NA_KROWS - NA_QROWS)[p]
            rs_rel = (0, a, NA_KROWS - NA_KH)[p]
            assert NA_QROWS * qt - ks == q_off and rs - ks == rs_rel and rs + NA_KH <= ks + NA_KROWS


_check_na_patterns()


def _na_bias_kernel(rb_ref, o_ref):
    p = pl.program_id(0)
    qc = _iota((GRID_W, GRID_W), 0)
    kc = _iota((GRID_W, GRID_W), 1)
    cs = jnp.clip(qc - NA_KW // 2, 0, GRID_W - NA_KW)
    col_ok = (kc >= cs) & (kc < cs + NA_KW)
    for a in range(NA_QROWS):
        q_off, rs_rel = _na_pattern_offsets(p, a)
        for b in range(NA_KROWS):
            row_ok = (b >= rs_rel) & (b < rs_rel + NA_KH)
            dr = jnp.clip(b - q_off - a + NA_KH - 1, 0, 2 * NA_KH - 2)
            v = jnp.broadcast_to(rb_ref[0, pl.ds(dr, 1), :], (GRID_W, LANES))
            t = pltpu.roll(v, LANES - (NA_KW - 1), 1, stride=1, stride_axis=0)[:, :GRID_W]
            o_ref[0, 0, GRID_W * a:GRID_W * a + GRID_W, GRID_W * b:GRID_W * b + GRID_W] = jnp.where(
                col_ok & row_ok, t * LOG2E, -1e30)


def _na_bias_table(rel_bias):
    rb = jnp.pad(rel_bias.astype(F32), ((0, 0), (0, 0), (0, LANES - (2 * NA_KW - 1))))
    return pl.pallas_call(
        _na_bias_kernel,
        out_shape=jax.ShapeDtypeStruct((3, N_HEADS, TQ, NA_NK), F32),
        grid=(3, N_HEADS),
        in_specs=[pl.BlockSpec((1, 2 * NA_KH - 1, LANES), lambda p, h: (h, 0, 0))],
        out_specs=pl.BlockSpec((1, 1, TQ, NA_NK), lambda p, h: (p, h, 0, 0)),
        compiler_params=_cparams(("parallel", "parallel")),
        name="na_bias",
    )(rb)


def _na_lat_kernel(u_ref, kc_ref, vc_ref, bias_ref, o_ref):
    qt = pl.program_id(1)
    c = 64 ** -0.5 * LOG2E
    ks = jnp.clip(NA_QROWS * qt - NA_KH // 2, 0, NA_ROWS - NA_KROWS)
    kstart = pl.multiple_of(ks * GRID_W, GRID_W)
    qstart = pl.multiple_of(qt * TQ, TQ)
    for h in range(N_HEADS):
        q = u_ref[pl.ds(qstart, TQ), 64 * h:64 * h + 64].astype(BF16)
        k = u_ref[pl.ds(kstart, NA_NK), 256 + 64 * h:256 + 64 * h + 64].astype(BF16)
        v = u_ref[pl.ds(kstart, NA_NK), 512 + 64 * h:512 + 64 * h + 64].astype(BF16)
        kc = kc_ref[0, 0, :, 64 * h:64 * h + 64].astype(BF16)
        vc = vc_ref[0, 0, :, 64 * h:64 * h + 64].astype(BF16)
        t_loc = _nt(q, k) * c + bias_ref[0, h]
        t_ctx = _nt(q, kc) * c
        m = jnp.maximum(jnp.max(t_loc, axis=-1, keepdims=True), jnp.max(t_ctx, axis=-1, keepdims=True))
        e_loc = jnp.exp2(t_loc - m)
        e_ctx = jnp.exp2(t_ctx - m)
        inv = 1.0 / (jnp.sum(e_loc, axis=-1, keepdims=True) + jnp.sum(e_ctx, axis=-1, keepdims=True))
        o = _dot((e_loc * inv).astype(BF16), v) + _dot((e_ctx * inv).astype(BF16), vc)
        o_ref[:, 64 * h:64 * h + 64] = o


def _na_lat(u_na, k_ctx, v_ctx, bias, layer):
    nqt = L_LAT // TQ
    cache = pl.BlockSpec((1, 1, PAST, 256), lambda b, t: (b, layer, 0, 0))
    return pl.pallas_call(
        _na_lat_kernel,
        out_shape=jax.ShapeDtypeStruct((R_LAT, 256), F32),
        grid=(B_LAT, nqt),
        in_specs=[pl.BlockSpec((L_LAT, W_NA), lambda b, t: (R_CTX // L_LAT + b, 0)),
                  cache, cache,
                  pl.BlockSpec((1, N_HEADS, TQ, NA_NK), lambda b, t: (_na_pattern(t), 0, 0, 0))],
        out_specs=pl.BlockSpec((TQ, 256), lambda b, t: (b * nqt + t, 0)),
        compiler_params=_cparams(("parallel", "arbitrary")),
        name="na_lat",
    )(u_na, k_ctx, v_ctx, bias)


def _mla_kernel(*refs, latent, seq):
    if latent:
        (u_ref, gq_ref, gkv_ref, wuq_ref, wuk_ref, wuv_ref, cos_ref, sa_ref, sb_ref,
         ckv_c_ref, kr_c_ref, o_ref, k_s, v_s) = refs
    else:
        (u_ref, gq_ref, gkv_ref, wuq_ref, wuk_ref, wuv_ref, o_ref, ckv_o_ref, k_s, v_s) = refs
    scale = (MLA_NOPE + MLA_ROPE) ** -0.5
    ckv = _rms(u_ref[:, 384:640], gkv_ref[...])
    kr = u_ref[:, 640:768]
    if latent:
        kr = _rope(kr, cos_ref[...], sa_ref[...], sb_ref[...])
    else:
        ckv_o_ref[...] = ckv
    ckv_b = ckv.astype(BF16)
    for h in range(N_HEADS):
        k_s[0:seq, 128 * h:128 * h + 128] = (_dot(ckv_b, wuk_ref[:, 128 * h:128 * h + 128]) + kr).astype(BF16)
    v_s[0:seq, :] = _dot(ckv_b, wuv_ref[...]).astype(BF16)
    if latent:
        cc = ckv_c_ref[0, 0].astype(BF16)
        krc = kr_c_ref[0]
        for h in range(N_HEADS):
            k_s[seq:seq + PAST, 128 * h:128 * h + 128] = (
                _dot(cc, wuk_ref[:, 128 * h:128 * h + 128]) + krc).astype(BF16)
        v_s[seq:seq + PAST, :] = _dot(cc, wuv_ref[...]).astype(BF16)

    def q_tile(t, carry):
        r0 = pl.multiple_of(t * TQ, TQ)
        cq = _rms(u_ref[pl.ds(r0, TQ), 0:384], gq_ref[...]).astype(BF16)
        for h in range(N_HEADS):
            q = _dot(cq, wuq_ref[:, 128 * h:128 * h + 128])
            if latent:
                q = _rope(q, cos_ref[pl.ds(r0, TQ), :], sa_ref[pl.ds(r0, TQ), :], sb_ref[pl.ds(r0, TQ), :])
            e, den = _exp_rows(_nt(q.astype(BF16), k_s[:, 128 * h:128 * h + 128]), scale)
            p = (e * (1.0 / den)).astype(BF16)
            o_ref[pl.ds(r0, TQ), 64 * h:64 * h + 64] = _dot(p, v_s[:, 64 * h:64 * h + 64])
        return carry

    lax.fori_loop(0, seq // TQ, q_tile, 0)


def _mla(u_mla, gq, gkv, wuq, wuk, wuv, rope128=None, ckv_ctx=None, kr_ctx=None, layer=0):
    latent = rope128 is not None
    seq = L_LAT if latent else L_CTX
    nb = B_LAT if latent else B_CTX
    off = R_CTX // L_LAT if latent else 0
    lk = seq + PAST if latent else seq
    const = lambda b: (0, 0)
    in_specs = [pl.BlockSpec((seq, W_MLA), lambda b: (off + b, 0)),
                pl.BlockSpec((1, MLA_Q_RANK), const),
                pl.BlockSpec((1, MLA_KV_RANK), const),
                pl.BlockSpec((MLA_Q_RANK, 512), const),
                pl.BlockSpec((MLA_KV_RANK, 512), const),
                pl.BlockSpec((MLA_KV_RANK, 256), const)]
    args = [u_mla, gq, gkv, wuq, wuk, wuv]
    out_shape = [jax.ShapeDtypeStruct((nb * seq, 256), F32)]
    out_specs = [pl.BlockSpec((seq, 256), lambda b: (b, 0))]
    if latent:
        in_specs += [pl.BlockSpec((seq, LANES), const)] * 3
        in_specs += [pl.BlockSpec((1, 1, PAST, MLA_KV_RANK), lambda b: (b, layer, 0, 0)),
                     pl.BlockSpec((1, PAST, LANES), lambda b: (b, 0, 0))]
        args += list(rope128) + [ckv_ctx, kr_ctx]
    else:
        out_shape.append(jax.ShapeDtypeStruct((nb * seq, MLA_KV_RANK), F32))
        out_specs.append(pl.BlockSpec((seq, MLA_KV_RANK), lambda b: (b, 0)))
    return pl.pallas_call(
        functools.partial(_mla_kernel, latent=latent, seq=seq),
        out_shape=out_shape,
        grid=(nb,),
        in_specs=in_specs,
        out_specs=out_specs,
        scratch_shapes=[pltpu.VMEM((lk, 512), BF16), pltpu.VMEM((lk, 256), BF16)],
        compiler_params=_cparams(("parallel",)),
        name="mla_lat" if latent else "mla_ctx",
    )(*args)


def _df_kernel(*refs, latent, seq, lam_init):
    if latent:
        (u_ref, lv_ref, gs_ref, cos_ref, sa_ref, sb_ref, kc_ref, vc_ref, o_ref, k_s, v_s) = refs
    else:
        (u_ref, lv_ref, gs_ref, o_ref, k_s, v_s) = refs
    scale = DF_HD ** -0.5
    lv = lv_ref[...]
    lam = (jnp.exp(jnp.sum(lv[0:1] * lv[1:2], axis=1, keepdims=True))
           - jnp.exp(jnp.sum(lv[2:3] * lv[3:4], axis=1, keepdims=True)) + lam_init)
    k = u_ref[:, 256:512]
    if latent:
        k = _rope(k, cos_ref[...], sa_ref[...], sb_ref[...])
        k_s[seq:seq + PAST, :] = kc_ref[0, 0].astype(BF16)
        v_s[seq:seq + PAST, :] = vc_ref[0, 0].astype(BF16)
    k_s[0:seq, :] = k.astype(BF16)
    v_s[0:seq, :] = u_ref[:, 512:768].astype(BF16)
    first = _iota((TQ, 64), 1) < DF_HD

    def q_tile(t, carry):
        r0 = pl.multiple_of(t * TQ, TQ)
        q = u_ref[pl.ds(r0, TQ), 0:256]
        if latent:
            q = _rope(q, cos_ref[pl.ds(r0, TQ), :], sa_ref[pl.ds(r0, TQ), :], sb_ref[pl.ds(r0, TQ), :])
        for h in range(N_HEADS):
            qh = q[:, 64 * h:64 * h + 64]
            kh = k_s[:, 64 * h:64 * h + 64]
            q0 = jnp.where(first, qh, 0.0).astype(BF16)
            q1 = jnp.where(first, 0.0, qh).astype(BF16)
            e0, d0 = _exp_rows(_nt(q0, kh), scale)
            e1, d1 = _exp_rows(_nt(q1, kh), scale)
            w = (e0 * (1.0 / d0) - e1 * (lam / d1)).astype(BF16)
            o = _dot(w, v_s[:, 64 * h:64 * h + 64])
            o_ref[pl.ds(r0, TQ), 64 * h:64 * h + 64] = _rms(o, gs_ref[...]) * (1.0 - lam_init)
        return carry

    lax.fori_loop(0, seq // TQ, q_tile, 0)


def _df(u_df, lam_vec, g_sub, lam_init, rope256=None, k_ctx=None, v_ctx=None, layer=0):
    latent = rope256 is not None
    seq = L_LAT if latent else L_CTX
    nb = B_LAT if latent else B_CTX
    off = R_CTX // L_LAT if latent else 0
    lk = seq + PAST if latent else seq
    const = lambda b: (0, 0)
    in_specs = [pl.BlockSpec((seq, W_DF), lambda b: (off + b, 0)),
                pl.BlockSpec((4, DF_HD), const),
                pl.BlockSpec((1, DF_V), const)]
    args = [u_df, lam_vec, g_sub]
    if latent:
        in_specs += [pl.BlockSpec((seq, 256), const)] * 3
        in_specs += [pl.BlockSpec((1, 1, PAST, 256), lambda b: (b, layer, 0, 0))] * 2
        args += list(rope256) + [k_ctx, v_ctx]
    return pl.pallas_call(
        functools.partial(_df_kernel, latent=latent, seq=seq, lam_init=lam_init),
        out_shape=jax.ShapeDtypeStruct((nb * seq, 256), F32),
        grid=(nb,),
        in_specs=in_specs,
        out_specs=pl.BlockSpec((seq, 256), lambda b: (b, 0)),
        scratch_shapes=[pltpu.VMEM((lk, 256), BF16), pltpu.VMEM((lk, 256), BF16)],
        compiler_params=_cparams(("parallel",)),
        name="df_lat" if latent else "df_ctx",
    )(*args)


def _softplus(x):
    return jnp.maximum(x, 0.0) + jnp.log1p(jnp.exp(-jnp.abs(x)))


def _ssd_kernel(*refs, latent, seq):
    if latent:
        (u_ref, dtt_ref, cw_ref, cb_ref, dtb_c_ref, dtb_r_ref, a_c_ref, a_r_ref, dvec_ref, gn_ref, h0_ref,
         o_ref, xs_s, bm_s, cm_s, dtc_s, dtr_s, y_s, st_s) = refs
    else:
        (u_ref, dtt_ref, cw_ref, cb_ref, dtb_c_ref, dtb_r_ref, a_c_ref, a_r_ref, dvec_ref, gn_ref,
         o_ref, st_o_ref, xs_s, bm_s, cm_s, dtc_s, dtr_s, y_s, st_s) = refs
    q = SSD_Q
    nchunk = seq // q

    def conv(a, w, b):
        row = _iota(a.shape, 0)
        prv = jnp.where(row == 0, 0.0, pltpu.roll(a, 1, 0))
        nxt = jnp.where(row == seq - 1, 0.0, pltpu.roll(a, seq - 1, 0))
        return _silu(w[0:1] * prv + w[1:2] * a + w[2:3] * nxt + b)

    cw = cw_ref[...]
    cb = cb_ref[...]
    xs_s[...] = conv(u_ref[:, 0:256], cw[:, 0:256], cb[:, 0:256])
    bm_s[...] = conv(u_ref[:, 512:640], cw[:, 256:384], cb[:, 256:384])
    cm_s[...] = conv(u_ref[:, 640:768], cw[:, 384:512], cb[:, 384:512])
    dtc_s[...] = _softplus(u_ref[:, 768:776] + dtb_c_ref[...])
    dtr_s[...] = _softplus(dtt_ref[...] + dtb_r_ref[...])
    eye_n = jnp.where(_iota((SSM_N, SSM_N), 0) == _iota((SSM_N, SSM_N), 1), 1.0, 0.0).astype(BF16)

    def transpose64(a):
        return sum(_nt(eye_n, p) for p in _split3(a))

    if latent:
        for d in range(2):
            for h in range(N_HEADS):
                st_s[d, :, 64 * h:64 * h + 64] = transpose64(h0_ref[0, 0, d, h])
    else:
        st_s[...] = jnp.zeros(st_s.shape, F32)

    ri = _iota((q, q), 0)
    ci = _iota((q, q), 1)
    lower = ri >= ci
    upper = ri <= ci
    tri_l = jnp.where(lower, 1.0, 0.0).astype(BF16)
    tri_u = jnp.where(upper, 1.0, 0.0).astype(BF16)
    lane8 = _iota((q, 8), 1)
    sub8 = _iota((8, q), 0)
    a_col = -jnp.exp(a_c_ref[...])
    a_row = -jnp.exp(a_r_ref[...])

    def chunk(c, d):
        c0 = pl.multiple_of(c * q, q)
        xs = xs_s[pl.ds(c0, q), :]
        bm = bm_s[pl.ds(c0, q), :]
        cm = cm_s[pl.ds(c0, q), :]
        dtc = dtc_s[pl.ds(c0, q), :]
        ac = dtc * a_col
        ar = dtr_s[:, pl.ds(c0, q)] * a_row
        ac_f = sum(_dot(tri_l, p) for p in _split3(ac))
        ac_b = sum(_dot(tri_u, p) for p in _split3(ac))
        acum_c = jnp.where(lane8 < N_HEADS, ac_f, ac_b)
        ar_f = sum(_dot(p, tri_u) for p in _split3(ar))
        ar_b = sum(_dot(p, tri_l) for p in _split3(ar))
        acum_r = jnp.where(sub8 < N_HEADS, ar_f, ar_b)
        bm_t = bm.T
        mask = lower if d == 0 else upper
        for g in range(SSM_GROUPS):
            cg = cm[:, 64 * g:64 * g + 64].astype(BF16)
            cb_g = _nt(cg, bm[:, 64 * g:64 * g + 64].astype(BF16))
            bt_g = bm_t[64 * g:64 * g + 64, :].astype(BF16)
            for hh in range(N_HEADS // SSM_GROUPS):
                h = g * (N_HEADS // SSM_GROUPS) + hh
                j = N_HEADS * d + h
                a_c = acum_c[:, j:j + 1]
                a_r = acum_r[j:j + 1, :]
                a_end = a_r[:, q - 1:q] if d == 0 else a_r[:, 0:1]
                seg = a_c - a_r
                decay = jnp.where(mask, jnp.exp(jnp.where(mask, seg, 0.0)), 0.0)
                xdt = xs[:, 64 * h:64 * h + 64] * dtc[:, j:j + 1]
                st = st_s[d, :, 64 * h:64 * h + 64]
                y = _dot((cb_g * decay).astype(BF16), xdt.astype(BF16))
                y = y + _dot(cg, st.astype(BF16)) * jnp.exp(a_c)
                if d == 0:
                    y_s[pl.ds(c0, q), 64 * h:64 * h + 64] = y
                else:
                    y_s[pl.ds(c0, q), 64 * h:64 * h + 64] = y_s[pl.ds(c0, q), 64 * h:64 * h + 64] + y
                xw = (xdt * jnp.exp(a_end - a_c)).astype(BF16)
                st_s[d, :, 64 * h:64 * h + 64] = st * jnp.exp(a_end) + _dot(bt_g, xw)

    def fwd(c, carry):
        chunk(c, 0)
        return carry

    def bwd(c, carry):
        chunk(nchunk - 1 - c, 1)
        return carry

    lax.fori_loop(0, nchunk, fwd, 0)
    lax.fori_loop(0, nchunk, bwd, 0)
    y = y_s[...] + dvec_ref[...] * xs_s[...]
    o_ref[...] = _rms(y * _silu(u_ref[:, 256:512]), gn_ref[...])
    if not latent:
        for d in range(2):
            for h in range(N_HEADS):
                st_o_ref[0, d, h] = transpose64(st_s[d, :, 64 * h:64 * h + 64])


def _ssd(u_ssm, dt_t, conv_w, conv_b, dt_bias, a_log, d_vec, g_norm, h0=None, layer=0):
    latent = h0 is not None
    seq = L_LAT if latent else L_CTX
    nb = B_LAT if latent else B_CTX
    off = R_CTX // L_LAT if latent else 0
    const = lambda b: (0, 0)
    dtb = dt_bias.reshape(1, 8)
    alg = a_log.reshape(1, 8)
    in_specs = [pl.BlockSpec((seq, W_SSM), lambda b: (off + b, 0)),
                pl.BlockSpec((8, seq), lambda b: (0, off + b)),
                pl.BlockSpec((3, 512), const), pl.BlockSpec((1, 512), const),
                pl.BlockSpec((1, 8), const), pl.BlockSpec((8, 1), const),
                pl.BlockSpec((1, 8), const), pl.BlockSpec((8, 1), const),
                pl.BlockSpec((1, 256), const), pl.BlockSpec((1, 256), const)]
    args = [u_ssm, dt_t, conv_w, conv_b.reshape(1, 512), dtb, dtb.reshape(8, 1), alg, alg.reshape(8, 1),
            d_vec, g_norm]
    out_shape = [jax.ShapeDtypeStruct((nb * seq, 256), F32)]
    out_specs = [pl.BlockSpec((seq, 256), lambda b: (b, 0))]
    if latent:
        in_specs.append(pl.BlockSpec((1, 1, 2, N_HEADS, SSM_P, SSM_N), lambda b: (b, layer, 0, 0, 0, 0)))
        args.append(h0)
    else:
        out_shape.append(jax.ShapeDtypeStruct((nb, 2, N_HEADS, SSM_P, SSM_N), F32))
        out_specs.append(pl.BlockSpec((1, 2, N_HEADS, SSM_P, SSM_N), lambda b: (b, 0, 0, 0, 0)))
    scratch = [pltpu.VMEM((seq, 256), F32), pltpu.VMEM((seq, 128), F32), pltpu.VMEM((seq, 128), F32),
               pltpu.VMEM((seq, 8), F32), pltpu.VMEM((8, seq), F32), pltpu.VMEM((seq, 256), F32),
               pltpu.VMEM((2, SSM_N, 256), F32)]
    return pl.pallas_call(
        functools.partial(_ssd_kernel, latent=latent, seq=seq),
        out_shape=out_shape,
        grid=(nb,),
        in_specs=in_specs,
        out_specs=out_specs,
        scratch_shapes=scratch,
        compiler_params=_cparams(("parallel",)),
        name="ssd_lat" if latent else "ssd_ctx",
    )(*args)


def _merge_kernel(xc_ref, xl_ref, mod_ref, c0, l0, c1, l1, c2, l2, c3, l3, gate_ref, wb_ref, wo_ref, g2_ref,
                  wr_ref, xo_ref, h2_ref, aff_ref, afft_ref):
    i = pl.program_id(0)
    acc = None
    for b, (bc, bl) in enumerate(((c0, l0), (c1, l1), (c2, l2), (c3, l3))):
        proj = _dot(_pick_group(i, bc, bl).astype(BF16), wb_ref[b])
        term = gate_ref[:, D * b:D * b + D].astype(F32) * proj
        acc = term if acc is None else acc + term
    x = _pick_group(i, xc_ref, xl_ref) + mod_ref[0, 2:3, :] * _dot(acc.astype(BF16), wo_ref[...])
    xo_ref[...] = x
    h2 = _rms(x, g2_ref[...]) * (1.0 + mod_ref[0, 4:5, :]) + mod_ref[0, 3:4, :]
    hb = h2.astype(BF16)
    h2_ref[...] = hb
    hl = (h2 - hb.astype(F32)).astype(BF16)
    wr = wr_ref[...]
    wh = wr.astype(BF16)
    wl = (wr - wh.astype(F32)).astype(BF16)
    logits = _dot(hb, wh) + _dot(hl, wh) + _dot(hb, wl)
    aff = _softmax_rows(logits)
    aff_ref[...] = aff
    eye = jnp.where(_iota((N_EXPERTS, N_EXPERTS), 0) == _iota((N_EXPERTS, N_EXPERTS), 1), 1.0, 0.0).astype(BF16)
    afft_ref[...] = sum(_nt(eye, p) for p in _split3(aff))


def _merge(x_ctx, x_lat, mod_l, branches, gates, wb, wo, g2, wr):
    const = lambda i: (0, 0)
    row = lambda i: (i, 0)
    ctx_row = lambda i: (_ctx_tile(i), 0)
    lat_row = lambda i: (_lat_tile(i), 0)
    return pl.pallas_call(
        _merge_kernel,
        out_shape=[jax.ShapeDtypeStruct((R_ALL, D), F32), jax.ShapeDtypeStruct((R_ALL, D), BF16),
                   jax.ShapeDtypeStruct((R_ALL, N_EXPERTS), F32), jax.ShapeDtypeStruct((N_EXPERTS, R_ALL), F32)],
        grid=(NT_ALL,),
        in_specs=[pl.BlockSpec((TM, D), ctx_row), pl.BlockSpec((TM, D), lat_row),
                  pl.BlockSpec((1, 6, D), lambda i: (_mod_row(i), 0, 0))]
                 + [pl.BlockSpec((TM, 256), ctx_row), pl.BlockSpec((TM, 256), lat_row)] * 4
                 + [pl.BlockSpec((TM, W_GATE), row),
                    pl.BlockSpec((4, 256, D), lambda i: (0, 0, 0), pipeline_mode=pl.Buffered(1)),
                    pl.BlockSpec((D, D), const, pipeline_mode=pl.Buffered(1)),
                    pl.BlockSpec((1, D), const),
                    pl.BlockSpec((D, N_EXPERTS), const)],
        out_specs=[pl.BlockSpec((TM, D), row), pl.BlockSpec((TM, D), row),
                   pl.BlockSpec((TM, N_EXPERTS), row), pl.BlockSpec((N_EXPERTS, TM), lambda i: (0, i))],
        compiler_params=_cparams(("parallel",)),
        name="merge_router",
    )(x_ctx, x_lat, mod_l, *[a for pair in branches for a in pair], gates, wb, wo, g2, wr)


def _select_kernel(afft_ref, lpos_t_ref, lpos_ref, tstart_ref, cnt_ref, gt_s, eq_s, need_s, carry_s, *, cap):
    t = pl.program_id(0)
    ne = N_EXPERTS

    @pl.when(t == 0)
    def _():
        aff = afft_ref[...]

        def count_ge(v):
            return jnp.sum(jnp.where(aff >= v, 1.0, 0.0), axis=1, keepdims=True)

        def bisect(_, lh):
            lo, hi = lh
            mid = jnp.where(lo > 0.0, jnp.sqrt(lo) * jnp.sqrt(hi), hi * 2.0 ** -32)
            mid = jnp.clip(mid, lo, hi)
            ok = count_ge(mid) >= cap
            return jnp.where(ok, mid, lo), jnp.where(ok, hi, mid)

        _, hi = lax.fori_loop(0, SELECT_BISECT_STEPS, bisect,
                              (jnp.zeros((ne, 1), F32), jnp.full((ne, 1), 2.0, F32)))

        def short(st):
            return jnp.min(st[1]) < cap

        def peel(st):
            bound, cnt = st
            nxt = jnp.max(jnp.where(aff < bound, aff, -1.0), axis=1, keepdims=True)
            upd = cnt < cap
            return jnp.where(upd, nxt, bound), jnp.where(upd, count_ge(nxt), cnt)

        thr, _ = lax.while_loop(short, peel, (hi, count_ge(hi)))
        gt = jnp.where(aff > thr, 1.0, 0.0)
        gt_s[...] = gt
        eq_s[...] = jnp.where(aff == thr, 1.0, 0.0)
        need_col = cap - jnp.sum(gt, axis=1, keepdims=True)
        eye = _iota((ne, ne), 0) == _iota((ne, ne), 1)
        need_s[...] = jnp.sum(jnp.where(eye, need_col, 0.0), axis=0, keepdims=True)
        carry_s[...] = jnp.zeros(carry_s.shape, F32)

    sl = pl.ds(pl.multiple_of(t * TM, TM), TM)
    eye_t = jnp.where(_iota((TM, TM), 0) == _iota((TM, TM), 1), 1.0, 0.0).astype(BF16)
    eye_e = jnp.where(_iota((ne, ne), 0) == _iota((ne, ne), 1), 1.0, 0.0).astype(BF16)
    before = jnp.where(_iota((TM, TM), 0) > _iota((TM, TM), 1), 1.0, 0.0).astype(BF16)
    gtm = _nt(eye_t, gt_s[:, sl].astype(BF16))
    eqm = _nt(eye_t, eq_s[:, sl].astype(BF16))
    eq_seen = carry_s[0:1, :]
    pos0 = carry_s[1:2, :]
    eq_rank = _dot(before, eqm.astype(BF16)) + eq_seen
    sel = jnp.maximum(gtm, eqm * jnp.where(eq_rank < need_s[...], 1.0, 0.0))
    lp = _dot(before, sel.astype(BF16))
    cnt = jnp.sum(sel, axis=0, keepdims=True)
    lpos_ref[...] = jnp.where(sel > 0.0, lp, -1.0)
    lp_t = _nt(eye_e, lp.astype(BF16))
    sel_t = _nt(eye_e, sel.astype(BF16))
    lpos_t_ref[...] = jnp.where(sel_t > 0.0, lp_t, -1.0)
    tstart_ref[0] = pos0.astype(jnp.int32)
    cnt_ref[0] = cnt.astype(jnp.int32)
    carry_s[0:1, :] = eq_seen + jnp.sum(eqm, axis=0, keepdims=True)
    carry_s[1:2, :] = pos0 + cnt


def _select(afft, cap):
    n = afft.shape[1]
    nt = n // TM
    ne = N_EXPERTS
    return pl.pallas_call(
        functools.partial(_select_kernel, cap=cap),
        out_shape=[jax.ShapeDtypeStruct((ne, n), F32), jax.ShapeDtypeStruct((n, ne), F32),
                   jax.ShapeDtypeStruct((nt, 1, ne), jnp.int32), jax.ShapeDtypeStruct((nt, 1, ne), jnp.int32)],
        grid=(nt,),
        in_specs=[pl.BlockSpec((ne, n), lambda t: (0, 0))],
        out_specs=[pl.BlockSpec((ne, TM), lambda t: (0, t)), pl.BlockSpec((TM, ne), lambda t: (t, 0)),
                   pl.BlockSpec((1, 1, ne), lambda t: (t, 0, 0)), pl.BlockSpec((1, 1, ne), lambda t: (t, 0, 0))],
        scratch_shapes=[pltpu.VMEM((ne, n), F32), pltpu.VMEM((ne, n), F32),
                        pltpu.VMEM((1, ne), F32), pltpu.VMEM((8, ne), F32)],
        compiler_params=_cparams(("arbitrary",)),
        name="moe_select",
    )(afft)


def _gather_kernel(tstart_sm, rounds_sm, h2_ref, lpos_t_ref, xe_ref, stage, sem, *, nt, cap):
    i = pl.program_id(0)
    slot = lax.rem(i, 2)
    ne, w = N_EXPERTS, MOE_W
    m = ne * w
    expand = jnp.where((_iota((m, ne), 0) >> MOE_W_LOG2) == _iota((m, ne), 1), 1.0, 0.0).astype(BF16)
    lpx = _dot(expand, lpos_t_ref[...].astype(BF16))
    rsub = (_iota((m, TM), 0) & (w - 1)).astype(F32)

    def build(k, sl):
        oh = jnp.where(lpx - (k * w).astype(F32) == rsub, 1.0, 0.0).astype(BF16)
        g = _dot(oh, h2_ref[...])
        for s in range(SUBLANES):
            stage[sl, pl.ds(s, m, stride=SUBLANES), :] = g[:, LANES * s:LANES * s + LANES]

    def copy(e, start, sl):
        return pltpu.make_async_copy(
            stage.at[sl, pl.ds(e * w * SUBLANES, w * SUBLANES)],
            xe_ref.at[e, pl.ds(pl.multiple_of(start * SUBLANES, SUBLANES), w * SUBLANES)],
            sem.at[sl])

    def issue(k, sl):
        for e in range(ne):
            copy(e, jnp.minimum(tstart_sm[i * ne + e] + k * w, cap), sl).start()

    def wait_all(sl):
        for e in range(ne):
            copy(e, 0, sl).wait()

    zero = jnp.int32(0)
    build(zero, slot)

    @pl.when(i > 0)
    def _():
        wait_all(1 - slot)

    issue(zero, slot)

    def extra(k, carry):
        wait_all(slot)
        build(k, slot)
        issue(k, slot)
        return carry

    lax.fori_loop(1, rounds_sm[i], extra, 0)

    @pl.when(i == nt - 1)
    def _():
        wait_all(slot)
        stage[1 - slot, pl.ds(0, w * SUBLANES), :] = jnp.zeros((w * SUBLANES, LANES), F32)
        for e in range(ne):
            pltpu.make_async_copy(
                stage.at[1 - slot, pl.ds(0, w * SUBLANES)],
                xe_ref.at[e, pl.ds(cap * SUBLANES, w * SUBLANES)], sem.at[1 - slot]).start()
        wait_all(1 - slot)


def _gather(h2b, lpos_t, tstart, rounds, cap, tile_off):
    n = lpos_t.shape[1]
    nt = n // TM
    ne, w = N_EXPERTS, MOE_W
    grid_spec = pltpu.PrefetchScalarGridSpec(
        num_scalar_prefetch=2,
        grid=(nt,),
        in_specs=[pl.BlockSpec((TM, D), lambda i, a, b: (tile_off + i, 0)),
                  pl.BlockSpec((ne, TM), lambda i, a, b: (0, i))],
        out_specs=pl.BlockSpec(memory_space=pl.ANY),
        scratch_shapes=[pltpu.VMEM((2, ne * w * SUBLANES, LANES), F32), pltpu.SemaphoreType.DMA((2,))],
    )
    return pl.pallas_call(
        functools.partial(_gather_kernel, nt=nt, cap=cap),
        out_shape=jax.ShapeDtypeStruct((ne, (cap + w) * SUBLANES, LANES), F32),
        grid_spec=grid_spec,
        compiler_params=_cparams(("arbitrary",)),
        name="moe_gather",
    )(tstart, rounds, h2b, lpos_t)


def _ffn_kernel(xe_ref, wg_ref, wu_ref, wd_ref, ye_ref, wg_s, wu_s, wd_s):
    @pl.when(pl.program_id(1) == 0)
    def _():
        wg_s[...] = wg_ref[0].astype(BF16)
        wu_s[...] = wu_ref[0].astype(BF16)
        wd_s[...] = wd_ref[0].astype(BF16)

    x = jnp.concatenate([xe_ref[0, pl.ds(s, FFN_TM, stride=SUBLANES), :] for s in range(SUBLANES)],
                        axis=1).astype(BF16)
    hid = (_silu(_dot(x, wg_s[...])) * _dot(x, wu_s[...])).astype(BF16)
    y = _dot(hid, wd_s[...])
    for s in range(SUBLANES):
        ye_ref[0, pl.ds(s, FFN_TM, stride=SUBLANES), :] = y[:, LANES * s:LANES * s + LANES]


def _ffn(xe, wg, wu, wd, cap):
    ne = N_EXPERTS
    blk = (1, FFN_TM * SUBLANES, LANES)
    wspec = pl.BlockSpec((1, D, D), lambda e, j: (e, 0, 0))
    return pl.pallas_call(
        _ffn_kernel,
        out_shape=jax.ShapeDtypeStruct((ne, cap * SUBLANES, LANES), F32),
        grid=(ne, cap // FFN_TM),
        in_specs=[pl.BlockSpec(blk, lambda e, j: (e, j, 0)), wspec, wspec, wspec],
        out_specs=pl.BlockSpec(blk, lambda e, j: (e, j, 0)),
        scratch_shapes=[pltpu.VMEM((D, D), BF16)] * 3,
        compiler_params=_cparams(("parallel", "arbitrary")),
        name="moe_ffn",
    )(xe, wg, wu, wd)


def _combine_kernel(tstart_sm, rounds_sm, lpos_ref, aff_ref, x_ref, mod_ref, fg_ref, ye_ref, o_ref,
                    ybuf, acc_s, sem, *, nt, cap, final):
    i = pl.program_id(0)
    slot = lax.rem(i, 2)
    ne, w = N_EXPERTS, MOE_W
    m = ne * w

    def win_start(tile, e, k):
        return jnp.minimum(tstart_sm[tile * ne + e] + k * w, cap - w)

    def copy(e, start, sl):
        return pltpu.make_async_copy(
            ye_ref.at[e, pl.ds(pl.multiple_of(start * SUBLANES, SUBLANES), w * SUBLANES)],
            ybuf.at[sl, pl.ds(e * w * SUBLANES, w * SUBLANES)],
            sem.at[sl])

    def fetch(tile, k, sl):
        for e in range(ne):
            copy(e, win_start(tile, e, k), sl).start()

    def wait_all(sl):
        for e in range(ne):
            copy(e, 0, sl).wait()

    zero = jnp.int32(0)

    @pl.when(i == 0)
    def _():
        fetch(i, zero, slot)

    wait_all(slot)

    @pl.when(i + 1 < nt)
    def _():
        fetch(i + 1, zero, 1 - slot)

    expand = jnp.where((_iota((ne, m), 1) >> MOE_W_LOG2) == _iota((ne, m), 0), 1.0, 0.0).astype(BF16)
    aff = aff_ref[...]
    ah = aff.astype(BF16)
    al = (aff - ah.astype(F32)).astype(BF16)
    lpx = _dot(lpos_ref[...].astype(BF16), expand)
    ahx = _dot(ah, expand)
    alx = _dot(al, expand)
    lane = _iota((1, m), 1)
    rl = (_iota((TM, m), 1) & (w - 1)).astype(F32)

    def compute(k, sl):
        shift = jnp.zeros((1, m), F32)
        for e in range(ne):
            sh = (tstart_sm[i * ne + e] - win_start(i, e, k)).astype(F32)
            shift = jnp.where((lane >> MOE_W_LOG2) == e, sh, shift)
        lo = (k * w).astype(F32)
        hit = jnp.where(lpx + shift == rl, 1.0, 0.0) * jnp.where(lpx >= lo, 1.0, 0.0) * jnp.where(lpx < lo + w, 1.0, 0.0)
        ohh = (hit * ahx).astype(BF16)
        ohl = (hit * alx).astype(BF16)
        y = jnp.concatenate(
            [jnp.concatenate([ybuf[sl, pl.ds(e * w * SUBLANES + s, w, stride=SUBLANES), :]
                              for s in range(SUBLANES)], axis=1) for e in range(ne)], axis=0)
        yh = y.astype(BF16)
        yl = (y - yh.astype(F32)).astype(BF16)
        return _dot(ohh, yh) + _dot(ohl, yh) + _dot(ohh, yl)

    acc_s[...] = compute(zero, slot)

    def extra(k, carry):
        fetch(i, k, slot)
        wait_all(slot)
        acc_s[...] = acc_s[...] + compute(k, slot)
        return carry

    lax.fori_loop(1, rounds_sm[i], extra, 0)
    x = x_ref[...] + mod_ref[0, 5:6, :] * acc_s[...]
    if final:
        x = _rms(x, fg_ref[...])
    o_ref[...] = x


def _combine(x, mod_l, lpos, aff, ye, tstart, rounds, fg, cap, tile_off, final):
    n = lpos.shape[0]
    nt = n // TM
    ne, w = N_EXPERTS, MOE_W
    grid_spec = pltpu.PrefetchScalarGridSpec(
        num_scalar_prefetch=2,
        grid=(nt,),
        in_specs=[pl.BlockSpec((TM, ne), lambda i, a, b: (i, 0)),
                  pl.BlockSpec((TM, ne), lambda i, a, b: (tile_off + i, 0)),
                  pl.BlockSpec((TM, D), lambda i, a, b: (tile_off + i, 0)),
                  pl.BlockSpec((1, 6, D), lambda i, a, b: (_mod_row(tile_off + i), 0, 0)),
                  pl.BlockSpec((1, D), lambda i, a, b: (0, 0)),
                  pl.BlockSpec(memory_space=pl.ANY)],
        out_specs=pl.BlockSpec((TM, D), lambda i, a, b: (i, 0)),
        scratch_shapes=[pltpu.VMEM((2, ne * w * SUBLANES, LANES), F32), pltpu.VMEM((TM, D), F32),
                        pltpu.SemaphoreType.DMA((2,))],
    )
    return pl.pallas_call(
        functools.partial(_combine_kernel, nt=nt, cap=cap, final=final),
        out_shape=jax.ShapeDtypeStruct((n, D), F32),
        grid_spec=grid_spec,
        compiler_params=_cparams(("arbitrary",)),
        name="moe_combine",
    )(tstart, rounds, lpos, aff, x, mod_l, fg, ye)


def _moe_group(x, mod_l, h2b, aff, afft_g, wg, wu, wd, fg, n, tile_off, final):
    cap = (CAP_FACTOR * n) // N_EXPERTS
    lpos_t, lpos, tstart, cnt = _select(afft_g, cap)
    tstart = tstart.reshape(-1)
    rounds = jnp.maximum((jnp.max(cnt.reshape(-1, N_EXPERTS), axis=1) + MOE_W - 1) // MOE_W, 1).astype(jnp.int32)
    xe = _gather(h2b, lpos_t, tstart, rounds, cap, tile_off)
    ye = _ffn(xe, wg, wu, wd, cap)
    return _combine(x, mod_l, lpos, aff, ye, tstart, rounds, fg, cap, tile_off, final)


def _layout_w_in(w):
    o = np.cumsum([0, 256, 256, 256, 384, 256, 32, 256, 256, 256, 256, 256, 128, 128, 8, 4096])
    wb = w.astype(BF16)
    zeros = lambda n: jnp.zeros((D, n), BF16)
    w_all = jnp.concatenate([wb[:, o[0]:o[5]], zeros(64), wb[:, o[5]:o[6]], zeros(32), wb[:, o[6]:o[14]],
                             zeros(120), wb[:, o[14]:o[15]]], axis=1)
    return w_all, wb[:, o[13]:o[14]].T


def _layout_mla(w_uq, w_ukv):
    uq = w_uq.reshape(MLA_Q_RANK, N_HEADS, MLA_NOPE + MLA_ROPE)
    uq = jnp.pad(uq, ((0, 0), (0, 0), (0, LANES - MLA_NOPE - MLA_ROPE))).reshape(MLA_Q_RANK, N_HEADS * LANES)
    ukv = w_ukv.reshape(MLA_KV_RANK, N_HEADS, MLA_NOPE + MLA_V)
    uk = jnp.pad(ukv[:, :, :MLA_NOPE], ((0, 0), (0, 0), (0, LANES - MLA_NOPE))).reshape(MLA_KV_RANK, N_HEADS * LANES)
    uv = ukv[:, :, MLA_NOPE:].reshape(MLA_KV_RANK, N_HEADS * MLA_V)
    return uq.astype(BF16), uk.astype(BF16), uv.astype(BF16)


def kernel(x_prompt, x_sample, cache_na_k, cache_na_v, cache_mla_ckv, cache_mla_krope, cache_df_k, cache_df_v,
           state_ssm, c, c_ctx, mod_w, mod_b, norm1_g, norm2_g, w_in, na_rel_bias, mla_q_norm_g, mla_kv_norm_g,
           mla_w_uq, mla_w_ukv, df_lambda, df_subln_g, ssm_conv_w, ssm_conv_b, ssm_dt_bias, ssm_a_log, ssm_d,
           ssm_norm_g, w_branch, w_out, router_w, exp_w_gate, exp_w_up, exp_w_down, final_norm_g):
    x_ctx = x_prompt.reshape(R_CTX, D)
    x_lat = x_sample.reshape(R_LAT, D)
    cvec = jnp.concatenate([c_ctx[None, :], c, jnp.zeros((16 - 1 - B_LAT, D), F32)], axis=0)
    mod = _modulation(cvec, mod_w, mod_b).reshape(DEPTH, 16, 6, D)

    cos32, sa32, sb32 = _rope_tables32()
    pad128 = lambda a, fill: np.pad(a, ((0, 0), (64, 32)), constant_values=fill)
    rope128 = (pad128(cos32, 1.0), pad128(sa32, 0.0), pad128(sb32, 0.0))
    rope256 = tuple(np.tile(a, (1, 8)) for a in (cos32, sa32, sb32))
    fg = final_norm_g.reshape(1, D)
    cna_k = cache_na_k.reshape(B_LAT, DEPTH, PAST, 256)
    cna_v = cache_na_v.reshape(B_LAT, DEPTH, PAST, 256)
    cdf_k = cache_df_k.reshape(B_LAT, DEPTH, PAST, 256)
    cdf_v = cache_df_v.reshape(B_LAT, DEPTH, PAST, 256)

    outs = {k: [] for k in ("na_k", "na_v", "ckv", "krope", "df_k", "df_v", "ssm")}
    for l in range(DEPTH):
        mod_l = mod[l]
        (u_na, u_mla, u_df, u_ssm, gates, dt_t, na_k, na_v, df_k, df_v, krope) = _in_proj(
            x_ctx, x_lat, mod_l, norm1_g[l].reshape(1, D), *_layout_w_in(w_in[l]))
        bias = _na_bias_table(na_rel_bias[l])
        br_na = (_attn_ctx(u_na), _na_lat(u_na, cna_k, cna_v, bias, l))
        wuq, wuk, wuv = _layout_mla(mla_w_uq[l], mla_w_ukv[l])
        gq = mla_q_norm_g[l].reshape(1, MLA_Q_RANK)
        gkv = mla_kv_norm_g[l].reshape(1, MLA_KV_RANK)
        mla_c, ckv_new = _mla(u_mla, gq, gkv, wuq, wuk, wuv)
        kr_ctx = jnp.pad(cache_mla_krope[:, l], ((0, 0), (0, 0), (64, 32)))
        (mla_l,) = _mla(u_mla, gq, gkv, wuq, wuk, wuv, rope128, cache_mla_ckv, kr_ctx, l)
        lam_init = 0.8 - 0.6 * math.exp(-0.3 * l)
        gs = df_subln_g[l].reshape(1, DF_V)
        br_df = (_df(u_df, df_lambda[l], gs, lam_init),
                 _df(u_df, df_lambda[l], gs, lam_init, rope256, cdf_k, cdf_v, l))
        d_vec = jnp.repeat(ssm_d[l], SSM_P).reshape(1, 256)
        gn = ssm_norm_g[l].reshape(1, 256)
        ssm_c, st_new = _ssd(u_ssm, dt_t, ssm_conv_w[l], ssm_conv_b[l], ssm_dt_bias[l], ssm_a_log[l], d_vec, gn)
        (ssm_l,) = _ssd(u_ssm, dt_t, ssm_conv_w[l], ssm_conv_b[l], ssm_dt_bias[l], ssm_a_log[l], d_vec, gn,
                        state_ssm, l)
        x_mid, h2b, aff, afft = _merge(x_ctx, x_lat, mod_l, (br_na, (mla_c, mla_l), br_df, (ssm_c, ssm_l)), gates,
                                       w_branch[l].astype(BF16), w_out[l].astype(BF16),
                                       norm2_g[l].reshape(1, D), router_w[l])
        final = l == DEPTH - 1
        x_ctx = _moe_group(x_mid, mod_l, h2b, aff, afft[:, :R_CTX], exp_w_gate[l], exp_w_up[l], exp_w_down[l],
                           fg, R_CTX, 0, final)
        x_lat = _moe_group(x_mid, mod_l, h2b, aff, afft[:, R_CTX:], exp_w_gate[l], exp_w_up[l], exp_w_down[l],
                           fg, R_LAT, NT_CTX, final)
        outs["na_k"].append(na_k.reshape(B_CTX, L_CTX, N_HEADS, 64))
        outs["na_v"].append(na_v.reshape(B_CTX, L_CTX, N_HEADS, 64))
        outs["ckv"].append(ckv_new.reshape(B_CTX, L_CTX, MLA_KV_RANK))
        outs["krope"].append(krope.reshape(B_CTX, L_CTX, MLA_ROPE))
        outs["df_k"].append(df_k.reshape(B_CTX, L_CTX, N_HEADS, 2, DF_HD))
        outs["df_v"].append(df_v.reshape(B_CTX, L_CTX, N_HEADS, DF_V))
        outs["ssm"].append(st_new)
    stack = lambda k: jnp.stack(outs[k], axis=1)
    return (x_ctx.reshape(B_CTX, L_CTX, D), x_lat.reshape(B_LAT, L_LAT, D), stack("na_k"), stack("na_v"),
            stack("ckv"), stack("krope"), stack("df_k"), stack("df_v"), stack("ssm"))
```

```python
import functools
import math

import numpy as np
import jax
import jax.numpy as jnp
from jax import lax
from jax.experimental import pallas as pl
from jax.experimental.pallas import tpu as pltpu

F32 = jnp.float32
BF16 = jnp.bfloat16

D = 1024
B_CTX, L_CTX = 32, 256
B_LAT, L_LAT = 8, 2048
PAST = 256
DEPTH = 2
GRID_W = 64
EPS = 1e-6
ROPE_BASE = 10000.0
N_HEADS = 4
NA_KH, NA_KW = 8, 16
MLA_NOPE, MLA_ROPE, MLA_V = 64, 32, 64
MLA_Q_RANK, MLA_KV_RANK = 384, 256
DF_HD, DF_V = 32, 64
SSM_P, SSM_N, SSM_GROUPS = 64, 64, 2
N_EXPERTS = 16
CAP_FACTOR = 2

LANES = 128
SUBLANES = 8
TM = 256
R_CTX = B_CTX * L_CTX
R_LAT = B_LAT * L_LAT
R_ALL = R_CTX + R_LAT
NT_CTX = R_CTX // TM
NT_LAT = R_LAT // TM
NT_ALL = R_ALL // TM
TILES_PER_LAT_BATCH = L_LAT // TM
TQ = 256
SSD_Q = 256
MOE_W = 64
MOE_W_LOG2 = 6
SELECT_BISECT_STEPS = 40
FFN_TM = 512
NA_QROWS = 4
NA_KROWS = NA_QROWS + NA_KH - 1
NA_NK = NA_KROWS * GRID_W
VMEM_LIMIT = 56 * 1024 * 1024


def _cparams(sem):
    return pltpu.CompilerParams(dimension_semantics=sem, vmem_limit_bytes=VMEM_LIMIT)


def _nt(a, b):
    return lax.dot_general(a, b, (((1,), (1,)), ((), ())), preferred_element_type=F32)


def _dot(a, b):
    return jnp.dot(a, b, preferred_element_type=F32)


def _rms(x, g):
    return x * lax.rsqrt(jnp.mean(x * x, axis=-1, keepdims=True) + EPS) * g


def _silu(x):
    return x * jax.nn.sigmoid(x)


def _softmax_rows(s):
    m = jnp.max(s, axis=-1, keepdims=True)
    e = jnp.exp(s - m)
    return e * (1.0 / jnp.sum(e, axis=-1, keepdims=True))


LOG2E = math.log2(math.e)


def _exp_rows(s, scale):
    m = jnp.max(s, axis=-1, keepdims=True)
    e = jnp.exp2((s - m) * (scale * LOG2E))
    return e, jnp.sum(e, axis=-1, keepdims=True)


def _exp_only(s, scale):
    return jnp.exp2((s - jnp.max(s, axis=-1, keepdims=True)) * (scale * LOG2E))


def _pv_normalised(e, v_ext):
    o = _dot(e.astype(BF16), v_ext)
    return o[:, 0:64] * (1.0 / o[:, 64:65])


def _store_v_ext(v_s, row0, v):
    rows = v.shape[0]
    lane = _iota((rows, LANES), 1)
    ones_col = jnp.where(lane == 64, 1.0, 0.0)
    for pair in range(2):
        blk = v[:, LANES * pair:LANES * pair + LANES]
        even = jnp.where(lane < 64, blk, ones_col)
        odd = jnp.where(lane < 64, pltpu.roll(blk, 64, 1), ones_col)
        v_s[row0:row0 + rows, 2 * LANES * pair:2 * LANES * pair + LANES] = even.astype(BF16)
        v_s[row0:row0 + rows, 2 * LANES * pair + LANES:2 * LANES * pair + 2 * LANES] = odd.astype(BF16)


def _split3(a):
    a1 = a.astype(BF16)
    r1 = a - a1.astype(F32)
    a2 = r1.astype(BF16)
    a3 = (r1 - a2.astype(F32)).astype(BF16)
    return a1, a2, a3


def _iota(shape, dim):
    return lax.broadcasted_iota(jnp.int32, shape, dim)


def _mod_row(i):
    return jnp.where(i < NT_CTX, 0, 1 + (i - NT_CTX) // TILES_PER_LAT_BATCH)


MOD_TN = 1536


def _mod_kernel(c_ref, w_ref, b_ref, o_ref):
    s = _silu(c_ref[...]).astype(BF16)
    o_ref[0] = _dot(s, w_ref[0].astype(BF16)) + b_ref[0]


def _modulation(cvec, mod_w, mod_b):
    n = 6 * D
    return pl.pallas_call(
        _mod_kernel,
        out_shape=jax.ShapeDtypeStruct((DEPTH, 16, n), F32),
        grid=(DEPTH, n // MOD_TN),
        in_specs=[pl.BlockSpec((16, D), lambda l, j: (0, 0)),
                  pl.BlockSpec((1, D, MOD_TN), lambda l, j: (l, 0, j)),
                  pl.BlockSpec((1, 1, MOD_TN), lambda l, j: (l, 0, j))],
        out_specs=pl.BlockSpec((1, 16, MOD_TN), lambda l, j: (l, 0, j)),
        compiler_params=_cparams(("parallel", "parallel")),
        name="modulation",
    )(cvec, mod_w, mod_b.reshape(DEPTH, 1, n))


W_NA, W_MLA, W_DF, W_SSM, W_GATE = 768, 768, 768, 896, 4 * D


W_IN_COLS = np.cumsum([0, W_NA, W_MLA, W_DF, W_SSM, W_GATE])
W_IN_ALL = int(W_IN_COLS[-1])


def _ctx_tile(i):
    return jnp.minimum(i, NT_CTX - 1)


def _lat_tile(i):
    return jnp.maximum(i - NT_CTX, 0)


def _pick_group(i, ctx_ref, lat_ref):
    return jnp.where(i < NT_CTX, ctx_ref[...], lat_ref[...])


def _in_kernel(xc_ref, xl_ref, mod_ref, g_ref, w_ref, wdt_t,
               ona, omla, odf, ossm, ogate, odt_t, onak, onav, odfk, odfv, okr):
    i = pl.program_id(0)
    x = _pick_group(i, xc_ref, xl_ref)
    h = _rms(x, g_ref[...]) * (1.0 + mod_ref[0, 1:2, :]) + mod_ref[0, 0:1, :]
    hb = h.astype(BF16)
    c = W_IN_COLS
    una = _dot(hb, w_ref[:, c[0]:c[1]])
    umla = _dot(hb, w_ref[:, c[1]:c[2]])
    udf = _dot(hb, w_ref[:, c[2]:c[3]])
    ona[...] = una
    omla[...] = umla
    odf[...] = udf
    ossm[...] = _dot(hb, w_ref[:, c[3]:c[4]])
    ogate[...] = jax.nn.sigmoid(_dot(hb, w_ref[:, c[4]:c[5]])).astype(BF16)
    odt_t[...] = _nt(wdt_t[...], hb)

    @pl.when(i < NT_CTX)
    def _():
        onak[...] = una[:, 256:512]
        onav[...] = una[:, 512:768]
        odfk[...] = udf[:, 256:512]
        odfv[...] = udf[:, 512:768]
        okr[...] = umla[:, 704:736]


def _in_proj(x_ctx, x_lat, mod_l, norm_g, w_all, w_dt_t):
    widths = (W_NA, W_MLA, W_DF, W_SSM)
    const = lambda i: (0, 0)
    row = lambda i: (i, 0)
    ctx_row = lambda i: (_ctx_tile(i), 0)
    out_shape = [jax.ShapeDtypeStruct((R_ALL, w), F32) for w in widths]
    out_shape += [jax.ShapeDtypeStruct((R_ALL, W_GATE), BF16), jax.ShapeDtypeStruct((8, R_ALL), F32)]
    out_shape += [jax.ShapeDtypeStruct((R_CTX, 256), F32)] * 4 + [jax.ShapeDtypeStruct((R_CTX, MLA_ROPE), F32)]
    out_specs = [pl.BlockSpec((TM, w), row) for w in widths]
    out_specs += [pl.BlockSpec((TM, W_GATE), row), pl.BlockSpec((8, TM), lambda i: (0, i))]
    out_specs += [pl.BlockSpec((TM, 256), ctx_row)] * 4 + [pl.BlockSpec((TM, MLA_ROPE), ctx_row)]
    return pl.pallas_call(
        _in_kernel,
        out_shape=out_shape,
        grid=(NT_ALL,),
        in_specs=[pl.BlockSpec((TM, D), ctx_row),
                  pl.BlockSpec((TM, D), lambda i: (_lat_tile(i), 0)),
                  pl.BlockSpec((1, 6, D), lambda i: (_mod_row(i), 0, 0)),
                  pl.BlockSpec((1, D), const),
                  pl.BlockSpec((D, W_IN_ALL), const, pipeline_mode=pl.Buffered(1)),
                  pl.BlockSpec((8, D), const, pipeline_mode=pl.Buffered(1))],
        out_specs=out_specs,
        compiler_params=_cparams(("arbitrary",)),
        name="in_proj",
    )(x_ctx, x_lat, mod_l, norm_g, w_all, w_dt_t)


def _rope(x, cos, sin_a, sin_b):
    n = x.shape[-1]
    nxt = pltpu.roll(x, n - 1, 1)
    prv = pltpu.roll(x, 1, 1)
    return x * cos + nxt * sin_a + prv * sin_b


def _rope_tables32():
    t = np.arange(L_LAT)
    quarter = 8
    inv = ROPE_BASE ** (-np.arange(quarter, dtype=np.float64) / quarter)
    rows = (t // GRID_W).astype(np.float64)[:, None]
    cols = (t % GRID_W).astype(np.float64)[:, None]
    ang = np.concatenate([rows * inv, cols * inv], axis=-1)
    cos = np.repeat(np.cos(ang), 2, axis=-1)
    sin = np.repeat(np.sin(ang), 2, axis=-1)
    even = (np.arange(32) % 2 == 0)[None, :]
    sin_a = np.where(even, -sin, 0.0)
    sin_b = np.where(even, 0.0, sin)
    return tuple(np.asarray(a, np.float32) for a in (cos, sin_a, sin_b))


def _attn_ctx_kernel(u_ref, o_ref, v_s):
    scale = 64 ** -0.5
    _store_v_ext(v_s, 0, u_ref[:, 512:768])
    for h in range(N_HEADS):
        q = u_ref[:, 64 * h:64 * h + 64].astype(BF16)
        k = u_ref[:, 256 + 64 * h:256 + 64 * h + 64].astype(BF16)
        e = _exp_only(_nt(q, k), scale)
        o_ref[:, 64 * h:64 * h + 64] = _pv_normalised(e, v_s[:, 128 * h:128 * h + 128])


def _attn_ctx(u_na):
    return pl.pallas_call(
        _attn_ctx_kernel,
        out_shape=jax.ShapeDtypeStruct((R_CTX, 256), F32),
        grid=(B_CTX,),
        in_specs=[pl.BlockSpec((L_CTX, W_NA), lambda b: (b, 0))],
        out_specs=pl.BlockSpec((L_CTX, 256), lambda b: (b, 0)),
        scratch_shapes=[pltpu.VMEM((L_CTX, 4 * LANES), BF16)],
        compiler_params=_cparams(("parallel",)),
        name="na_ctx",
    )(u_na)


NA_ROWS = L_LAT // GRID_W
NA_NQT = NA_ROWS // NA_QROWS


def _na_pattern(qt):
    return jnp.where(qt == 0, 0, jnp.where(qt == NA_NQT - 1, 2, 1))


def _na_pattern_offsets(p, a):
    q_off = jnp.where(p == 0, 0, jnp.where(p == 1, NA_KH // 2, NA_KROWS - NA_QROWS))
    rs_rel = jnp.where(p == 0, 0, jnp.where(p == 1, a, NA_KROWS - NA_KH))
    return q_off, rs_rel


def _check_na_patterns():
    for qt in range(NA_NQT):
        ks = int(np.clip(NA_QROWS * qt - NA_KH // 2, 0, NA_ROWS - NA_KROWS))
        p = 0 if qt == 0 else (2 if qt == NA_NQT - 1 else 1)
        for a in range(NA_QROWS):
            r = NA_QROWS * qt + a
            rs = int(np.clip(r - NA_KH // 2, 0, NA_ROWS - NA_KH))
            q_off = (0, NA_KH // 2, NA_KROWS - NA_QROWS)[p]
            rs_rel = (0, a, NA_KROWS - NA_KH)[p]
            assert NA_QROWS * qt - ks == q_off and rs - ks == rs_rel and rs + NA_KH <= ks + NA_KROWS


_check_na_patterns()


def _na_bias_kernel(rb_ref, o_ref):
    p = pl.program_id(0)
    qc = _iota((GRID_W, GRID_W), 0)
    kc = _iota((GRID_W, GRID_W), 1)
    cs = jnp.clip(qc - NA_KW // 2, 0, GRID_W - NA_KW)
    col_ok = (kc >= cs) & (kc < cs + NA_KW)
    for a in range(NA_QROWS):
        q_off, rs_rel = _na_pattern_offsets(p, a)
        for b in range(NA_KROWS):
            row_ok = (b >= rs_rel) & (b < rs_rel + NA_KH)
            dr = jnp.clip(b - q_off - a + NA_KH - 1, 0, 2 * NA_KH - 2)
            v = jnp.broadcast_to(rb_ref[0, pl.ds(dr, 1), :], (GRID_W, LANES))
            t = pltpu.roll(v, LANES - (NA_KW - 1), 1, stride=1, stride_axis=0)[:, :GRID_W]
            o_ref[0, 0, GRID_W * a:GRID_W * a + GRID_W, GRID_W * b:GRID_W * b + GRID_W] = jnp.where(
                col_ok & row_ok, t * LOG2E, -1e30)


def _na_bias_table(rel_bias):
    rb = jnp.pad(rel_bias.astype(F32), ((0, 0), (0, 0), (0, LANES - (2 * NA_KW - 1))))
    return pl.pallas_call(
        _na_bias_kernel,
        out_shape=jax.ShapeDtypeStruct((3, N_HEADS, TQ, NA_NK), F32),
        grid=(3, N_HEADS),
        in_specs=[pl.BlockSpec((1, 2 * NA_KH - 1, LANES), lambda p, h: (h, 0, 0))],
        out_specs=pl.BlockSpec((1, 1, TQ, NA_NK), lambda p, h: (p, h, 0, 0)),
        compiler_params=_cparams(("parallel", "parallel")),
        name="na_bias",
    )(rb)


def _na_lat_kernel(u_ref, kc_ref, vc_ref, bias_ref, o_ref, v_s, vc_s):
    qt = pl.program_id(1)
    c = 64 ** -0.5 * LOG2E
    ks = jnp.clip(NA_QROWS * qt - NA_KH // 2, 0, NA_ROWS - NA_KROWS)
    kstart = pl.multiple_of(ks * GRID_W, GRID_W)
    qstart = pl.multiple_of(qt * TQ, TQ)
    _store_v_ext(v_s, 0, u_ref[pl.ds(kstart, NA_NK), 512:768])
    _store_v_ext(vc_s, 0, vc_ref[0, 0])
    for h in range(N_HEADS):
        q = u_ref[pl.ds(qstart, TQ), 64 * h:64 * h + 64].astype(BF16)
        k = u_ref[pl.ds(kstart, NA_NK), 256 + 64 * h:256 + 64 * h + 64].astype(BF16)
        kc = kc_ref[0, 0, :, 64 * h:64 * h + 64].astype(BF16)
        t_loc = _nt(q, k) * c + bias_ref[0, h]
        t_ctx = _nt(q, kc) * c
        m = jnp.maximum(jnp.max(t_loc, axis=-1, keepdims=True), jnp.max(t_ctx, axis=-1, keepdims=True))
        e_loc = jnp.exp2(t_loc - m).astype(BF16)
        e_ctx = jnp.exp2(t_ctx - m).astype(BF16)
        o = _dot(e_loc, v_s[:, 128 * h:128 * h + 128]) + _dot(e_ctx, vc_s[:, 128 * h:128 * h + 128])
        o_ref[:, 64 * h:64 * h + 64] = o[:, 0:64] * (1.0 / o[:, 64:65])


def _na_lat(u_na, k_ctx, v_ctx, bias, layer):
    nqt = L_LAT // TQ
    cache = pl.BlockSpec((1, 1, PAST, 256), lambda b, t: (b, layer, 0, 0))
    return pl.pallas_call(
        _na_lat_kernel,
        out_shape=jax.ShapeDtypeStruct((R_LAT, 256), F32),
        grid=(B_LAT, nqt),
        in_specs=[pl.BlockSpec((L_LAT, W_NA), lambda b, t: (R_CTX // L_LAT + b, 0)),
                  cache, cache,
                  pl.BlockSpec((1, N_HEADS, TQ, NA_NK), lambda b, t: (_na_pattern(t), 0, 0, 0))],
        out_specs=pl.BlockSpec((TQ, 256), lambda b, t: (b * nqt + t, 0)),
        scratch_shapes=[pltpu.VMEM((NA_NK, 4 * LANES), BF16), pltpu.VMEM((PAST, 4 * LANES), BF16)],
        compiler_params=_cparams(("parallel", "arbitrary")),
        name="na_lat",
    )(u_na, k_ctx, v_ctx, bias)


def _mla_kernel(*refs, latent, seq):
    if latent:
        (u_ref, gq_ref, gkv_ref, wuq_ref, wuk_ref, wuv_ref, cos_ref, sa_ref, sb_ref,
         ckv_c_ref, kr_c_ref, o_ref, k_s, v_s) = refs
    else:
        (u_ref, gq_ref, gkv_ref, wuq_ref, wuk_ref, wuv_ref, o_ref, ckv_o_ref, k_s, v_s) = refs
    scale = (MLA_NOPE + MLA_ROPE) ** -0.5
    ckv = _rms(u_ref[:, 384:640], gkv_ref[...])
    kr = u_ref[:, 640:768]
    if latent:
        kr = _rope(kr, cos_ref[...], sa_ref[...], sb_ref[...])
    else:
        ckv_o_ref[...] = ckv
    ckv_b = ckv.astype(BF16)
    for h in range(N_HEADS):
        k_s[0:seq, 128 * h:128 * h + 128] = (_dot(ckv_b, wuk_ref[:, 128 * h:128 * h + 128]) + kr).astype(BF16)
    ones_col = jnp.where((_iota((1, 4 * LANES), 1) & (LANES - 1)) == MLA_V, 1.0, 0.0)
    v_s[0:seq, :] = (_dot(ckv_b, wuv_ref[...]) + ones_col).astype(BF16)
    if latent:
        cc = ckv_c_ref[0, 0].astype(BF16)
        krc = kr_c_ref[0]
        for h in range(N_HEADS):
            k_s[seq:seq + PAST, 128 * h:128 * h + 128] = (
                _dot(cc, wuk_ref[:, 128 * h:128 * h + 128]) + krc).astype(BF16)
        v_s[seq:seq + PAST, :] = (_dot(cc, wuv_ref[...]) + ones_col).astype(BF16)

    def q_tile(t, carry):
        r0 = pl.multiple_of(t * TQ, TQ)
        cq = _rms(u_ref[pl.ds(r0, TQ), 0:384], gq_ref[...]).astype(BF16)
        for h in range(N_HEADS):
            q = _dot(cq, wuq_ref[:, 128 * h:128 * h + 128])
            if latent:
                q = _rope(q, cos_ref[pl.ds(r0, TQ), :], sa_ref[pl.ds(r0, TQ), :], sb_ref[pl.ds(r0, TQ), :])
            e = _exp_only(_nt(q.astype(BF16), k_s[:, 128 * h:128 * h + 128]), scale)
            o_ref[pl.ds(r0, TQ), 64 * h:64 * h + 64] = _pv_normalised(e, v_s[:, 128 * h:128 * h + 128])
        return carry

    lax.fori_loop(0, seq // TQ, q_tile, 0)


def _mla(u_mla, gq, gkv, wuq, wuk, wuv, rope128=None, ckv_ctx=None, kr_ctx=None, layer=0):
    latent = rope128 is not None
    seq = L_LAT if latent else L_CTX
    nb = B_LAT if latent else B_CTX
    off = R_CTX // L_LAT if latent else 0
    lk = seq + PAST if latent else seq
    const = lambda b: (0, 0)
    in_specs = [pl.BlockSpec((seq, W_MLA), lambda b: (off + b, 0)),
                pl.BlockSpec((1, MLA_Q_RANK), const),
                pl.BlockSpec((1, MLA_KV_RANK), const),
                pl.BlockSpec((MLA_Q_RANK, 512), const),
                pl.BlockSpec((MLA_KV_RANK, 512), const),
                pl.BlockSpec((MLA_KV_RANK, 512), const)]
    args = [u_mla, gq, gkv, wuq, wuk, wuv]
    out_shape = [jax.ShapeDtypeStruct((nb * seq, 256), F32)]
    out_specs = [pl.BlockSpec((seq, 256), lambda b: (b, 0))]
    if latent:
        in_specs += [pl.BlockSpec((seq, LANES), const)] * 3
        in_specs += [pl.BlockSpec((1, 1, PAST, MLA_KV_RANK), lambda b: (b, layer, 0, 0)),
                     pl.BlockSpec((1, PAST, LANES), lambda b: (b, 0, 0))]
        args += list(rope128) + [ckv_ctx, kr_ctx]
    else:
        out_shape.append(jax.ShapeDtypeStruct((nb * seq, MLA_KV_RANK), F32))
        out_specs.append(pl.BlockSpec((seq, MLA_KV_RANK), lambda b: (b, 0)))
    return pl.pallas_call(
        functools.partial(_mla_kernel, latent=latent, seq=seq),
        out_shape=out_shape,
        grid=(nb,),
        in_specs=in_specs,
        out_specs=out_specs,
        scratch_shapes=[pltpu.VMEM((lk, 512), BF16), pltpu.VMEM((lk, 512), BF16)],
        compiler_params=_cparams(("parallel",)),
        name="mla_lat" if latent else "mla_ctx",
    )(*args)


def _df_kernel(*refs, latent, seq, lam_init):
    if latent:
        (u_ref, lv_ref, gs_ref, cos_ref, sa_ref, sb_ref, kc_ref, vc_ref, o_ref, k_s, v_s) = refs
    else:
        (u_ref, lv_ref, gs_ref, o_ref, k_s, v_s) = refs
    scale = DF_HD ** -0.5
    lv = lv_ref[...]
    lam = (jnp.exp(jnp.sum(lv[0:1] * lv[1:2], axis=1, keepdims=True))
           - jnp.exp(jnp.sum(lv[2:3] * lv[3:4], axis=1, keepdims=True)) + lam_init)
    k = u_ref[:, 256:512]
    if latent:
        k = _rope(k, cos_ref[...], sa_ref[...], sb_ref[...])
        k_s[seq:seq + PAST, :] = kc_ref[0, 0].astype(BF16)
        _store_v_ext(v_s, seq, vc_ref[0, 0])
    k_s[0:seq, :] = k.astype(BF16)
    _store_v_ext(v_s, 0, u_ref[:, 512:768])
    first = _iota((TQ, 64), 1) < DF_HD

    def q_tile(t, carry):
        r0 = pl.multiple_of(t * TQ, TQ)
        q = u_ref[pl.ds(r0, TQ), 0:256]
        if latent:
            q = _rope(q, cos_ref[pl.ds(r0, TQ), :], sa_ref[pl.ds(r0, TQ), :], sb_ref[pl.ds(r0, TQ), :])
        for h in range(N_HEADS):
            qh = q[:, 64 * h:64 * h + 64]
            kh = k_s[:, 64 * h:64 * h + 64]
            q0 = jnp.where(first, qh, 0.0).astype(BF16)
            q1 = jnp.where(first, 0.0, qh).astype(BF16)
            vh = v_s[:, 128 * h:128 * h + 128]
            o = (_pv_normalised(_exp_only(_nt(q0, kh), scale), vh)
                 - lam * _pv_normalised(_exp_only(_nt(q1, kh), scale), vh))
            o_ref[pl.ds(r0, TQ), 64 * h:64 * h + 64] = _rms(o, gs_ref[...]) * (1.0 - lam_init)
        return carry

    lax.fori_loop(0, seq // TQ, q_tile, 0)


def _df(u_df, lam_vec, g_sub, lam_init, rope256=None, k_ctx=None, v_ctx=None, layer=0):
    latent = rope256 is not None
    seq = L_LAT if latent else L_CTX
    nb = B_LAT if latent else B_CTX
    off = R_CTX // L_LAT if latent else 0
    lk = seq + PAST if latent else seq
    const = lambda b: (0, 0)
    in_specs = [pl.BlockSpec((seq, W_DF), lambda b: (off + b, 0)),
                pl.BlockSpec((4, DF_HD), const),
                pl.BlockSpec((1, DF_V), const)]
    args = [u_df, lam_vec, g_sub]
    if latent:
        in_specs += [pl.BlockSpec((seq, 256), const)] * 3
        in_specs += [pl.BlockSpec((1, 1, PAST, 256), lambda b: (b, layer, 0, 0))] * 2
        args += list(rope256) + [k_ctx, v_ctx]
    return pl.pallas_call(
        functools.partial(_df_kernel, latent=latent, seq=seq, lam_init=lam_init),
        out_shape=jax.ShapeDtypeStruct((nb * seq, 256), F32),
        grid=(nb,),
        in_specs=in_specs,
        out_specs=pl.BlockSpec((seq, 256), lambda b: (b, 0)),
        scratch_shapes=[pltpu.VMEM((lk, 256), BF16), pltpu.VMEM((lk, 512), BF16)],
        compiler_params=_cparams(("parallel",)),
        name="df_lat" if latent else "df_ctx",
    )(*args)


def _softplus(x):
    return jnp.maximum(x, 0.0) + jnp.log1p(jnp.exp(-jnp.abs(x)))


def _ssd_kernel(*refs, latent, seq):
    if latent:
        (u_ref, dtt_ref, cw_ref, cb_ref, dtb_c_ref, dtb_r_ref, a_c_ref, a_r_ref, dvec_ref, gn_ref, h0_ref,
         o_ref, xs_s, bm_s, cm_s, dtc_s, dtr_s, y_s, st_s) = refs
    else:
        (u_ref, dtt_ref, cw_ref, cb_ref, dtb_c_ref, dtb_r_ref, a_c_ref, a_r_ref, dvec_ref, gn_ref,
         o_ref, st_o_ref, xs_s, bm_s, cm_s, dtc_s, dtr_s, y_s, st_s) = refs
    q = SSD_Q
    nchunk = seq // q

    def conv(a, w, b):
        row = _iota(a.shape, 0)
        prv = jnp.where(row == 0, 0.0, pltpu.roll(a, 1, 0))
        nxt = jnp.where(row == seq - 1, 0.0, pltpu.roll(a, seq - 1, 0))
        return _silu(w[0:1] * prv + w[1:2] * a + w[2:3] * nxt + b)

    cw = cw_ref[...]
    cb = cb_ref[...]
    xs_s[...] = conv(u_ref[:, 0:256], cw[:, 0:256], cb[:, 0:256])
    bm_s[...] = conv(u_ref[:, 512:640], cw[:, 256:384], cb[:, 256:384])
    cm_s[...] = conv(u_ref[:, 640:768], cw[:, 384:512], cb[:, 384:512])
    dtc_s[...] = _softplus(u_ref[:, 768:776] + dtb_c_ref[...])
    dtr_s[...] = _softplus(dtt_ref[...] + dtb_r_ref[...])
    eye_n = jnp.where(_iota((SSM_N, SSM_N), 0) == _iota((SSM_N, SSM_N), 1), 1.0, 0.0).astype(BF16)

    def transpose64(a):
        return sum(_nt(eye_n, p) for p in _split3(a))

    if latent:
        for d in range(2):
            for h in range(N_HEADS):
                st_s[d, :, 64 * h:64 * h + 64] = transpose64(h0_ref[0, 0, d, h])
    else:
        st_s[...] = jnp.zeros(st_s.shape, F32)

    ri = _iota((q, q), 0)
    ci = _iota((q, q), 1)
    lower = ri >= ci
    upper = ri <= ci
    tri_l = jnp.where(lower, 1.0, 0.0).astype(BF16)
    tri_u = jnp.where(upper, 1.0, 0.0).astype(BF16)
    lane8 = _iota((q, 8), 1)
    sub8 = _iota((8, q), 0)
    a_col = -jnp.exp(a_c_ref[...])
    a_row = -jnp.exp(a_r_ref[...])

    def chunk(c, d):
        c0 = pl.multiple_of(c * q, q)
        xs = xs_s[pl.ds(c0, q), :]
        bm = bm_s[pl.ds(c0, q), :]
        cm = cm_s[pl.ds(c0, q), :]
        dtc = dtc_s[pl.ds(c0, q), :]
        ac = dtc * a_col
        ar = dtr_s[:, pl.ds(c0, q)] * a_row
        ac_f = sum(_dot(tri_l, p) for p in _split3(ac))
        ac_b = sum(_dot(tri_u, p) for p in _split3(ac))
        acum_c = jnp.where(lane8 < N_HEADS, ac_f, ac_b)
        ar_f = sum(_dot(p, tri_u) for p in _split3(ar))
        ar_b = sum(_dot(p, tri_l) for p in _split3(ar))
        acum_r = jnp.where(sub8 < N_HEADS, ar_f, ar_b)
        bm_t = bm.T
        mask = lower if d == 0 else upper
        for g in range(SSM_GROUPS):
            cg = cm[:, 64 * g:64 * g + 64].astype(BF16)
            cb_g = _nt(cg, bm[:, 64 * g:64 * g + 64].astype(BF16))
            bt_g = bm_t[64 * g:64 * g + 64, :].astype(BF16)
            for hh in range(N_HEADS // SSM_GROUPS):
                h = g * (N_HEADS // SSM_GROUPS) + hh
                j = N_HEADS * d + h
                a_c = acum_c[:, j:j + 1]
                a_r = acum_r[j:j + 1, :]
                a_end = a_r[:, q - 1:q] if d == 0 else a_r[:, 0:1]
                seg = a_c - a_r
                decay = jnp.where(mask, jnp.exp(jnp.where(mask, seg, 0.0)), 0.0)
                xdt = xs[:, 64 * h:64 * h + 64] * dtc[:, j:j + 1]
                st = st_s[d, :, 64 * h:64 * h + 64]
                y = _dot((cb_g * decay).astype(BF16), xdt.astype(BF16))
                y = y + _dot(cg, st.astype(BF16)) * jnp.exp(a_c)
                if d == 0:
                    y_s[pl.ds(c0, q), 64 * h:64 * h + 64] = y
                else:
                    y_s[pl.ds(c0, q), 64 * h:64 * h + 64] = y_s[pl.ds(c0, q), 64 * h:64 * h + 64] + y
                xw = (xdt * jnp.exp(a_end - a_c)).astype(BF16)
                st_s[d, :, 64 * h:64 * h + 64] = st * jnp.exp(a_end) + _dot(bt_g, xw)

    def fwd(c, carry):
        chunk(c, 0)
        return carry

    def bwd(c, carry):
        chunk(nchunk - 1 - c, 1)
        return carry

    lax.fori_loop(0, nchunk, fwd, 0)
    lax.fori_loop(0, nchunk, bwd, 0)
    y = y_s[...] + dvec_ref[...] * xs_s[...]
    o_ref[...] = _rms(y * _silu(u_ref[:, 256:512]), gn_ref[...])
    if not latent:
        for d in range(2):
            for h in range(N_HEADS):
                st_o_ref[0, d, h] = transpose64(st_s[d, :, 64 * h:64 * h + 64])


def _ssd(u_ssm, dt_t, conv_w, conv_b, dt_bias, a_log, d_vec, g_norm, h0=None, layer=0):
    latent = h0 is not None
    seq = L_LAT if latent else L_CTX
    nb = B_LAT if latent else B_CTX
    off = R_CTX // L_LAT if latent else 0
    const = lambda b: (0, 0)
    dtb = dt_bias.reshape(1, 8)
    alg = a_log.reshape(1, 8)
    in_specs = [pl.BlockSpec((seq, W_SSM), lambda b: (off + b, 0)),
                pl.BlockSpec((8, seq), lambda b: (0, off + b)),
                pl.BlockSpec((3, 512), const), pl.BlockSpec((1, 512), const),
                pl.BlockSpec((1, 8), const), pl.BlockSpec((8, 1), const),
                pl.BlockSpec((1, 8), const), pl.BlockSpec((8, 1), const),
                pl.BlockSpec((1, 256), const), pl.BlockSpec((1, 256), const)]
    args = [u_ssm, dt_t, conv_w, conv_b.reshape(1, 512), dtb, dtb.reshape(8, 1), alg, alg.reshape(8, 1),
            d_vec, g_norm]
    out_shape = [jax.ShapeDtypeStruct((nb * seq, 256), F32)]
    out_specs = [pl.BlockSpec((seq, 256), lambda b: (b, 0))]
    if latent:
        in_specs.append(pl.BlockSpec((1, 1, 2, N_HEADS, SSM_P, SSM_N), lambda b: (b, layer, 0, 0, 0, 0)))
        args.append(h0)
    else:
        out_shape.append(jax.ShapeDtypeStruct((nb, 2, N_HEADS, SSM_P, SSM_N), F32))
        out_specs.append(pl.BlockSpec((1, 2, N_HEADS, SSM_P, SSM_N), lambda b: (b, 0, 0, 0, 0)))
    scratch = [pltpu.VMEM((seq, 256), F32), pltpu.VMEM((seq, 128), F32), pltpu.VMEM((seq, 128), F32),
               pltpu.VMEM((seq, 8), F32), pltpu.VMEM((8, seq), F32), pltpu.VMEM((seq, 256), F32),
               pltpu.VMEM((2, SSM_N, 256), F32)]
    return pl.pallas_call(
        functools.partial(_ssd_kernel, latent=latent, seq=seq),
        out_shape=out_shape,
        grid=(nb,),
        in_specs=in_specs,
        out_specs=out_specs,
        scratch_shapes=scratch,
        compiler_params=_cparams(("parallel",)),
        name="ssd_lat" if latent else "ssd_ctx",
    )(*args)


def _merge_kernel(xc_ref, xl_ref, mod_ref, c0, l0, c1, l1, c2, l2, c3, l3, gate_ref, wb_ref, wo_ref, g2_ref,
                  wr_ref, xo_ref, h2_ref, aff_ref, afft_ref):
    i = pl.program_id(0)
    acc = None
    for b, (bc, bl) in enumerate(((c0, l0), (c1, l1), (c2, l2), (c3, l3))):
        proj = _dot(_pick_group(i, bc, bl).astype(BF16), wb_ref[b])
        term = gate_ref[:, D * b:D * b + D].astype(F32) * proj
        acc = term if acc is None else acc + term
    x = _pick_group(i, xc_ref, xl_ref) + mod_ref[0, 2:3, :] * _dot(acc.astype(BF16), wo_ref[...])
    xo_ref[...] = x
    h2 = _rms(x, g2_ref[...]) * (1.0 + mod_ref[0, 4:5, :]) + mod_ref[0, 3:4, :]
    hb = h2.astype(BF16)
    h2_ref[...] = hb
    hl = (h2 - hb.astype(F32)).astype(BF16)
    wr = wr_ref[...]
    wh = wr.astype(BF16)
    wl = (wr - wh.astype(F32)).astype(BF16)
    logits = _dot(hb, wh) + _dot(hl, wh) + _dot(hb, wl)
    aff = _softmax_rows(logits)
    aff_ref[...] = aff
    eye = jnp.where(_iota((N_EXPERTS, N_EXPERTS), 0) == _iota((N_EXPERTS, N_EXPERTS), 1), 1.0, 0.0).astype(BF16)
    afft_ref[...] = sum(_nt(eye, p) for p in _split3(aff))


def _merge(x_ctx, x_lat, mod_l, branches, gates, wb, wo, g2, wr):
    const = lambda i: (0, 0)
    row = lambda i: (i, 0)
    ctx_row = lambda i: (_ctx_tile(i), 0)
    lat_row = lambda i: (_lat_tile(i), 0)
    return pl.pallas_call(
        _merge_kernel,
        out_shape=[jax.ShapeDtypeStruct((R_ALL, D), F32), jax.ShapeDtypeStruct((R_ALL, D), BF16),
                   jax.ShapeDtypeStruct((R_ALL, N_EXPERTS), F32), jax.ShapeDtypeStruct((N_EXPERTS, R_ALL), F32)],
        grid=(NT_ALL,),
        in_specs=[pl.BlockSpec((TM, D), ctx_row), pl.BlockSpec((TM, D), lat_row),
                  pl.BlockSpec((1, 6, D), lambda i: (_mod_row(i), 0, 0))]
                 + [pl.BlockSpec((TM, 256), ctx_row), pl.BlockSpec((TM, 256), lat_row)] * 4
                 + [pl.BlockSpec((TM, W_GATE), row),
                    pl.BlockSpec((4, 256, D), lambda i: (0, 0, 0), pipeline_mode=pl.Buffered(1)),
                    pl.BlockSpec((D, D), const, pipeline_mode=pl.Buffered(1)),
                    pl.BlockSpec((1, D), const),
                    pl.BlockSpec((D, N_EXPERTS), const)],
        out_specs=[pl.BlockSpec((TM, D), row), pl.BlockSpec((TM, D), row),
                   pl.BlockSpec((TM, N_EXPERTS), row), pl.BlockSpec((N_EXPERTS, TM), lambda i: (0, i))],
        compiler_params=_cparams(("parallel",)),
        name="merge_router",
    )(x_ctx, x_lat, mod_l, *[a for pair in branches for a in pair], gates, wb, wo, g2, wr)


def _select_kernel(afft_ref, lpos_t_ref, lpos_ref, tstart_ref, cnt_ref, gt_s, eq_s, need_s, carry_s, *, cap):
    t = pl.program_id(0)
    ne = N_EXPERTS

    @pl.when(t == 0)
    def _():
        aff = afft_ref[...]

        def count_ge(v):
            return jnp.sum(jnp.where(aff >= v, 1.0, 0.0), axis=1, keepdims=True)

        def bisect(_, lh):
            lo, hi = lh
            mid = jnp.where(lo > 0.0, jnp.sqrt(lo) * jnp.sqrt(hi), hi * 2.0 ** -32)
            mid = jnp.clip(mid, lo, hi)
            ok = count_ge(mid) >= cap
            return jnp.where(ok, mid, lo), jnp.where(ok, hi, mid)

        _, hi = lax.fori_loop(0, SELECT_BISECT_STEPS, bisect,
                              (jnp.zeros((ne, 1), F32), jnp.full((ne, 1), 2.0, F32)))

        def short(st):
            return jnp.min(st[1]) < cap

        def peel(st):
            bound, cnt = st
            nxt = jnp.max(jnp.where(aff < bound, aff, -1.0), axis=1, keepdims=True)
            upd = cnt < cap
            return jnp.where(upd, nxt, bound), jnp.where(upd, count_ge(nxt), cnt)

        thr, _ = lax.while_loop(short, peel, (hi, count_ge(hi)))
        gt = jnp.where(aff > thr, 1.0, 0.0)
        gt_s[...] = gt
        eq_s[...] = jnp.where(aff == thr, 1.0, 0.0)
        need_col = cap - jnp.sum(gt, axis=1, keepdims=True)
        eye = _iota((ne, ne), 0) == _iota((ne, ne), 1)
        need_s[...] = jnp.sum(jnp.where(eye, need_col, 0.0), axis=0, keepdims=True)
        carry_s[...] = jnp.zeros(carry_s.shape, F32)

    sl = pl.ds(pl.multiple_of(t * TM, TM), TM)
    eye_t = jnp.where(_iota((TM, TM), 0) == _iota((TM, TM), 1), 1.0, 0.0).astype(BF16)
    eye_e = jnp.where(_iota((ne, ne), 0) == _iota((ne, ne), 1), 1.0, 0.0).astype(BF16)
    before = jnp.where(_iota((TM, TM), 0) > _iota((TM, TM), 1), 1.0, 0.0).astype(BF16)
    gtm = _nt(eye_t, gt_s[:, sl].astype(BF16))
    eqm = _nt(eye_t, eq_s[:, sl].astype(BF16))
    eq_seen = carry_s[0:1, :]
    pos0 = carry_s[1:2, :]
    eq_rank = _dot(before, eqm.astype(BF16)) + eq_seen
    sel = jnp.maximum(gtm, eqm * jnp.where(eq_rank < need_s[...], 1.0, 0.0))
    lp = _dot(before, sel.astype(BF16))
    cnt = jnp.sum(sel, axis=0, keepdims=True)
    lpos_ref[...] = jnp.where(sel > 0.0, lp, -1.0)
    lp_t = _nt(eye_e, lp.astype(BF16))
    sel_t = _nt(eye_e, sel.astype(BF16))
    lpos_t_ref[...] = jnp.where(sel_t > 0.0, lp_t, -1.0)
    tstart_ref[0] = pos0.astype(jnp.int32)
    cnt_ref[0] = cnt.astype(jnp.int32)
    carry_s[0:1, :] = eq_seen + jnp.sum(eqm, axis=0, keepdims=True)
    carry_s[1:2, :] = pos0 + cnt


def _select(afft, cap):
    n = afft.shape[1]
    nt = n // TM
    ne = N_EXPERTS
    return pl.pallas_call(
        functools.partial(_select_kernel, cap=cap),
        out_shape=[jax.ShapeDtypeStruct((ne, n), F32), jax.ShapeDtypeStruct((n, ne), F32),
                   jax.ShapeDtypeStruct((nt, 1, ne), jnp.int32), jax.ShapeDtypeStruct((nt, 1, ne), jnp.int32)],
        grid=(nt,),
        in_specs=[pl.BlockSpec((ne, n), lambda t: (0, 0))],
        out_specs=[pl.BlockSpec((ne, TM), lambda t: (0, t)), pl.BlockSpec((TM, ne), lambda t: (t, 0)),
                   pl.BlockSpec((1, 1, ne), lambda t: (t, 0, 0)), pl.BlockSpec((1, 1, ne), lambda t: (t, 0, 0))],
        scratch_shapes=[pltpu.VMEM((ne, n), F32), pltpu.VMEM((ne, n), F32),
                        pltpu.VMEM((1, ne), F32), pltpu.VMEM((8, ne), F32)],
        compiler_params=_cparams(("arbitrary",)),
        name="moe_select",
    )(afft)


def _gather_kernel(tstart_sm, rounds_sm, h2_ref, lpos_t_ref, xe_ref, stage, sem, *, nt, cap):
    i = pl.program_id(0)
    slot = lax.rem(i, 2)
    ne, w = N_EXPERTS, MOE_W
    m = ne * w
    expand = jnp.where((_iota((m, ne), 0) >> MOE_W_LOG2) == _iota((m, ne), 1), 1.0, 0.0).astype(BF16)
    lpx = _dot(expand, lpos_t_ref[...].astype(BF16))
    rsub = (_iota((m, TM), 0) & (w - 1)).astype(F32)

    def build(k, sl):
        oh = jnp.where(lpx - (k * w).astype(F32) == rsub, 1.0, 0.0).astype(BF16)
        g = _dot(oh, h2_ref[...])
        for s in range(SUBLANES):
            stage[sl, pl.ds(s, m, stride=SUBLANES), :] = g[:, LANES * s:LANES * s + LANES]

    def copy(e, start, sl):
        return pltpu.make_async_copy(
            stage.at[sl, pl.ds(e * w * SUBLANES, w * SUBLANES)],
            xe_ref.at[e, pl.ds(pl.multiple_of(start * SUBLANES, SUBLANES), w * SUBLANES)],
            sem.at[sl])

    def issue(k, sl):
        for e in range(ne):
            copy(e, jnp.minimum(tstart_sm[i * ne + e] + k * w, cap), sl).start()

    def wait_all(sl):
        for e in range(ne):
            copy(e, 0, sl).wait()

    zero = jnp.int32(0)
    build(zero, slot)

    @pl.when(i > 0)
    def _():
        wait_all(1 - slot)

    issue(zero, slot)

    def extra(k, carry):
        wait_all(slot)
        build(k, slot)
        issue(k, slot)
        return carry

    lax.fori_loop(1, rounds_sm[i], extra, 0)

    @pl.when(i == nt - 1)
    def _():
        wait_all(slot)
        stage[1 - slot, pl.ds(0, w * SUBLANES), :] = jnp.zeros((w * SUBLANES, LANES), F32)
        for e in range(ne):
            pltpu.make_async_copy(
                stage.at[1 - slot, pl.ds(0, w * SUBLANES)],
                xe_ref.at[e, pl.ds(cap * SUBLANES, w * SUBLANES)], sem.at[1 - slot]).start()
        wait_all(1 - slot)


def _gather(h2b, lpos_t, tstart, rounds, cap, tile_off):
    n = lpos_t.shape[1]
    nt = n // TM
    ne, w = N_EXPERTS, MOE_W
    grid_spec = pltpu.PrefetchScalarGridSpec(
        num_scalar_prefetch=2,
        grid=(nt,),
        in_specs=[pl.BlockSpec((TM, D), lambda i, a, b: (tile_off + i, 0)),
                  pl.BlockSpec((ne, TM), lambda i, a, b: (0, i))],
        out_specs=pl.BlockSpec(memory_space=pl.ANY),
        scratch_shapes=[pltpu.VMEM((2, ne * w * SUBLANES, LANES), F32), pltpu.SemaphoreType.DMA((2,))],
    )
    return pl.pallas_call(
        functools.partial(_gather_kernel, nt=nt, cap=cap),
        out_shape=jax.ShapeDtypeStruct((ne, (cap + w) * SUBLANES, LANES), F32),
        grid_spec=grid_spec,
        compiler_params=_cparams(("arbitrary",)),
        name="moe_gather",
    )(tstart, rounds, h2b, lpos_t)


def _ffn_kernel(xe_ref, wg_ref, wu_ref, wd_ref, ye_ref, wg_s, wu_s, wd_s):
    @pl.when(pl.program_id(1) == 0)
    def _():
        wg_s[...] = wg_ref[0, 0].astype(BF16)
        wu_s[...] = wu_ref[0, 0].astype(BF16)
        wd_s[...] = wd_ref[0, 0].astype(BF16)

    x = jnp.concatenate([xe_ref[0, pl.ds(s, FFN_TM, stride=SUBLANES), :] for s in range(SUBLANES)],
                        axis=1).astype(BF16)
    hid = (_silu(_dot(x, wg_s[...])) * _dot(x, wu_s[...])).astype(BF16)
    y = _dot(hid, wd_s[...])
    for s in range(SUBLANES):
        ye_ref[0, pl.ds(s, FFN_TM, stride=SUBLANES), :] = y[:, LANES * s:LANES * s + LANES]


def _ffn(xe, wg, wu, wd, cap, layer):
    ne = N_EXPERTS
    blk = (1, FFN_TM * SUBLANES, LANES)
    wspec = pl.BlockSpec((1, 1, D, D), lambda e, j: (layer, e, 0, 0))
    return pl.pallas_call(
        _ffn_kernel,
        out_shape=jax.ShapeDtypeStruct((ne, cap * SUBLANES, LANES), F32),
        grid=(ne, cap // FFN_TM),
        in_specs=[pl.BlockSpec(blk, lambda e, j: (e, j, 0)), wspec, wspec, wspec],
        out_specs=pl.BlockSpec(blk, lambda e, j: (e, j, 0)),
        scratch_shapes=[pltpu.VMEM((D, D), BF16)] * 3,
        compiler_params=_cparams(("parallel", "arbitrary")),
        name="moe_ffn",
    )(xe, wg, wu, wd)


def _combine_kernel(tstart_sm, rounds_sm, lpos_ref, aff_ref, x_ref, mod_ref, fg_ref, ye_ref, o_ref,
                    ybuf, acc_s, sem, *, nt, cap, final):
    i = pl.program_id(0)
    slot = lax.rem(i, 2)
    ne, w = N_EXPERTS, MOE_W
    m = ne * w

    def win_start(tile, e, k):
        return jnp.minimum(tstart_sm[tile * ne + e] + k * w, cap - w)

    def copy(e, start, sl):
        return pltpu.make_async_copy(
            ye_ref.at[e, pl.ds(pl.multiple_of(start * SUBLANES, SUBLANES), w * SUBLANES)],
            ybuf.at[sl, pl.ds(e * w * SUBLANES, w * SUBLANES)],
            sem.at[sl])

    def fetch(tile, k, sl):
        for e in range(ne):
            copy(e, win_start(tile, e, k), sl).start()

    def wait_all(sl):
        for e in range(ne):
            copy(e, 0, sl).wait()

    zero = jnp.int32(0)

    @pl.when(i == 0)
    def _():
        fetch(i, zero, slot)

    wait_all(slot)

    @pl.when(i + 1 < nt)
    def _():
        fetch(i + 1, zero, 1 - slot)

    expand = jnp.where((_iota((ne, m), 1) >> MOE_W_LOG2) == _iota((ne, m), 0), 1.0, 0.0).astype(BF16)
    aff = aff_ref[...]
    ah = aff.astype(BF16)
    al = (aff - ah.astype(F32)).astype(BF16)
    lpx = _dot(lpos_ref[...].astype(BF16), expand)
    ahx = _dot(ah, expand)
    alx = _dot(al, expand)
    lane = _iota((1, m), 1)
    rl = (_iota((TM, m), 1) & (w - 1)).astype(F32)

    def compute(k, sl):
        shift = jnp.zeros((1, m), F32)
        for e in range(ne):
            sh = (tstart_sm[i * ne + e] - win_start(i, e, k)).astype(F32)
            shift = jnp.where((lane >> MOE_W_LOG2) == e, sh, shift)
        lo = (k * w).astype(F32)
        hit = jnp.where(lpx + shift == rl, 1.0, 0.0) * jnp.where(lpx >= lo, 1.0, 0.0) * jnp.where(lpx < lo + w, 1.0, 0.0)
        ohh = (hit * ahx).astype(BF16)
        ohl = (hit * alx).astype(BF16)
        y = jnp.concatenate(
            [jnp.concatenate([ybuf[sl, pl.ds(e * w * SUBLANES + s, w, stride=SUBLANES), :]
                              for s in range(SUBLANES)], axis=1) for e in range(ne)], axis=0)
        yb = y.astype(BF16)
        return _dot(ohh, yb) + _dot(ohl, yb)

    acc_s[...] = compute(zero, slot)

    def extra(k, carry):
        fetch(i, k, slot)
        wait_all(slot)
        acc_s[...] = acc_s[...] + compute(k, slot)
        return carry

    lax.fori_loop(1, rounds_sm[i], extra, 0)
    x = x_ref[...] + mod_ref[0, 5:6, :] * acc_s[...]
    if final:
        x = _rms(x, fg_ref[...])
    o_ref[...] = x


def _combine(x, mod_l, lpos, aff, ye, tstart, rounds, fg, cap, tile_off, final):
    n = lpos.shape[0]
    nt = n // TM
    ne, w = N_EXPERTS, MOE_W
    grid_spec = pltpu.PrefetchScalarGridSpec(
        num_scalar_prefetch=2,
        grid=(nt,),
        in_specs=[pl.BlockSpec((TM, ne), lambda i, a, b: (i, 0)),
                  pl.BlockSpec((TM, ne), lambda i, a, b: (tile_off + i, 0)),
                  pl.BlockSpec((TM, D), lambda i, a, b: (tile_off + i, 0)),
                  pl.BlockSpec((1, 6, D), lambda i, a, b: (_mod_row(tile_off + i), 0, 0)),
                  pl.BlockSpec((1, D), lambda i, a, b: (0, 0)),
                  pl.BlockSpec(memory_space=pl.ANY)],
        out_specs=pl.BlockSpec((TM, D), lambda i, a, b: (i, 0)),
        scratch_shapes=[pltpu.VMEM((2, ne * w * SUBLANES, LANES), F32), pltpu.VMEM((TM, D), F32),
                        pltpu.SemaphoreType.DMA((2,))],
    )
    return pl.pallas_call(
        functools.partial(_combine_kernel, nt=nt, cap=cap, final=final),
        out_shape=jax.ShapeDtypeStruct((n, D), F32),
        grid_spec=grid_spec,
        compiler_params=_cparams(("arbitrary",)),
        name="moe_combine",
    )(tstart, rounds, lpos, aff, x, mod_l, fg, ye)


def _moe_group(x, mod_l, h2b, aff, afft_g, wg, wu, wd, fg, n, tile_off, layer):
    final = layer == DEPTH - 1
    cap = (CAP_FACTOR * n) // N_EXPERTS
    lpos_t, lpos, tstart, cnt = _select(afft_g, cap)
    tstart = tstart.reshape(-1)
    rounds = jnp.maximum((jnp.max(cnt.reshape(-1, N_EXPERTS), axis=1) + MOE_W - 1) // MOE_W, 1).astype(jnp.int32)
    xe = _gather(h2b, lpos_t, tstart, rounds, cap, tile_off)
    ye = _ffn(xe, wg, wu, wd, cap, layer)
    return _combine(x, mod_l, lpos, aff, ye, tstart, rounds, fg, cap, tile_off, final)


def _layout_w_in(w):
    o = np.cumsum([0, 256, 256, 256, 384, 256, 32, 256, 256, 256, 256, 256, 128, 128, 8, 4096])
    wb = w.astype(BF16)
    zeros = lambda n: jnp.zeros((D, n), BF16)
    w_all = jnp.concatenate([wb[:, o[0]:o[5]], zeros(64), wb[:, o[5]:o[6]], zeros(32), wb[:, o[6]:o[14]],
                             zeros(120), wb[:, o[14]:o[15]]], axis=1)
    return w_all, wb[:, o[13]:o[14]].T


def _layout_mla(w_uq, w_ukv):
    uq = w_uq.reshape(MLA_Q_RANK, N_HEADS, MLA_NOPE + MLA_ROPE)
    uq = jnp.pad(uq, ((0, 0), (0, 0), (0, LANES - MLA_NOPE - MLA_ROPE))).reshape(MLA_Q_RANK, N_HEADS * LANES)
    ukv = w_ukv.reshape(MLA_KV_RANK, N_HEADS, MLA_NOPE + MLA_V)
    uk = jnp.pad(ukv[:, :, :MLA_NOPE], ((0, 0), (0, 0), (0, LANES - MLA_NOPE))).reshape(MLA_KV_RANK, N_HEADS * LANES)
    uv = jnp.pad(ukv[:, :, MLA_NOPE:], ((0, 0), (0, 0), (0, LANES - MLA_V))).reshape(MLA_KV_RANK, N_HEADS * LANES)
    return uq.astype(BF16), uk.astype(BF16), uv.astype(BF16)


def kernel(x_prompt, x_sample, cache_na_k, cache_na_v, cache_mla_ckv, cache_mla_krope, cache_df_k, cache_df_v,
           state_ssm, c, c_ctx, mod_w, mod_b, norm1_g, norm2_g, w_in, na_rel_bias, mla_q_norm_g, mla_kv_norm_g,
           mla_w_uq, mla_w_ukv, df_lambda, df_subln_g, ssm_conv_w, ssm_conv_b, ssm_dt_bias, ssm_a_log, ssm_d,
           ssm_norm_g, w_branch, w_out, router_w, exp_w_gate, exp_w_up, exp_w_down, final_norm_g):
    x_ctx = x_prompt.reshape(R_CTX, D)
    x_lat = x_sample.reshape(R_LAT, D)
    cvec = jnp.concatenate([c_ctx[None, :], c, jnp.zeros((16 - 1 - B_LAT, D), F32)], axis=0)
    mod = _modulation(cvec, mod_w, mod_b).reshape(DEPTH, 16, 6, D)

    cos32, sa32, sb32 = _rope_tables32()
    pad128 = lambda a, fill: np.pad(a, ((0, 0), (64, 32)), constant_values=fill)
    rope128 = (pad128(cos32, 1.0), pad128(sa32, 0.0), pad128(sb32, 0.0))
    rope256 = tuple(np.tile(a, (1, 8)) for a in (cos32, sa32, sb32))
    fg = final_norm_g.reshape(1, D)
    cna_k = cache_na_k.reshape(B_LAT, DEPTH, PAST, 256)
    cna_v = cache_na_v.reshape(B_LAT, DEPTH, PAST, 256)
    cdf_k = cache_df_k.reshape(B_LAT, DEPTH, PAST, 256)
    cdf_v = cache_df_v.reshape(B_LAT, DEPTH, PAST, 256)

    outs = {k: [] for k in ("na_k", "na_v", "ckv", "krope", "df_k", "df_v", "ssm")}
    for l in range(DEPTH):
        mod_l = mod[l]
        (u_na, u_mla, u_df, u_ssm, gates, dt_t, na_k, na_v, df_k, df_v, krope) = _in_proj(
            x_ctx, x_lat, mod_l, norm1_g[l].reshape(1, D), *_layout_w_in(w_in[l]))
        bias = _na_bias_table(na_rel_bias[l])
        br_na = (_attn_ctx(u_na), _na_lat(u_na, cna_k, cna_v, bias, l))
        wuq, wuk, wuv = _layout_mla(mla_w_uq[l], mla_w_ukv[l])
        gq = mla_q_norm_g[l].reshape(1, MLA_Q_RANK)
        gkv = mla_kv_norm_g[l].reshape(1, MLA_KV_RANK)
        mla_c, ckv_new = _mla(u_mla, gq, gkv, wuq, wuk, wuv)
        kr_ctx = jnp.pad(cache_mla_krope[:, l], ((0, 0), (0, 0), (64, 32)))
        (mla_l,) = _mla(u_mla, gq, gkv, wuq, wuk, wuv, rope128, cache_mla_ckv, kr_ctx, l)
        lam_init = 0.8 - 0.6 * math.exp(-0.3 * l)
        gs = df_subln_g[l].reshape(1, DF_V)
        br_df = (_df(u_df, df_lambda[l], gs, lam_init),
                 _df(u_df, df_lambda[l], gs, lam_init, rope256, cdf_k, cdf_v, l))
        d_vec = jnp.repeat(ssm_d[l], SSM_P).reshape(1, 256)
        gn = ssm_norm_g[l].reshape(1, 256)
        ssm_c, st_new = _ssd(u_ssm, dt_t, ssm_conv_w[l], ssm_conv_b[l], ssm_dt_bias[l], ssm_a_log[l], d_vec, gn)
        (ssm_l,) = _ssd(u_ssm, dt_t, ssm_conv_w[l], ssm_conv_b[l], ssm_dt_bias[l], ssm_a_log[l], d_vec, gn,
                        state_ssm, l)
        x_mid, h2b, aff, afft = _merge(x_ctx, x_lat, mod_l, (br_na, (mla_c, mla_l), br_df, (ssm_c, ssm_l)), gates,
                                       w_branch[l].astype(BF16), w_out[l].astype(BF16),
                                       norm2_g[l].reshape(1, D), router_w[l])
        x_ctx = _moe_group(x_mid, mod_l, h2b, aff, afft[:, :R_CTX], exp_w_gate, exp_w_up, exp_w_down,
                           fg, R_CTX, 0, l)
        x_lat = _moe_group(x_mid, mod_l, h2b, aff, afft[:, R_CTX:], exp_w_gate, exp_w_up, exp_w_down,
                           fg, R_LAT, NT_CTX, l)
        outs["na_k"].append(na_k.reshape(B_CTX, L_CTX, N_HEADS, 64))
        outs["na_v"].append(na_v.reshape(B_CTX, L_CTX, N_HEADS, 64))
        outs["ckv"].append(ckv_new.reshape(B_CTX, L_CTX, MLA_KV_RANK))
        outs["krope"].append(krope.reshape(B_CTX, L_CTX, MLA_ROPE))
        outs["df_k"].append(df_k.reshape(B_CTX, L_CTX, N_HEADS, 2, DF_HD))
        outs["df_v"].append(df_v.reshape(B_CTX, L_CTX, N_HEADS, DF_V))
        outs["ssm"].append(st_new)
    stack = lambda k: jnp.stack(outs[k], axis=1)
    return (x_ctx.reshape(B_CTX, L_CTX, D), x_lat.reshape(B_LAT, L_LAT, D), stack("na_k"), stack("na_v"),
            stack("ckv"), stack("krope"), stack("df_k"), stack("df_v"), stack("ssm"))
```

```python
import functools
import math

import numpy as np
import jax
import jax.numpy as jnp
from jax import lax
from jax.experimental import pallas as pl
from jax.experimental.pallas import tpu as pltpu

F32 = jnp.float32
BF16 = jnp.bfloat16

D = 1024
B_CTX, L_CTX = 32, 256
B_LAT, L_LAT = 8, 2048
PAST = 256
DEPTH = 2
GRID_W = 64
EPS = 1e-6
ROPE_BASE = 10000.0
N_HEADS = 4
NA_KH, NA_KW = 8, 16
MLA_NOPE, MLA_ROPE, MLA_V = 64, 32, 64
MLA_Q_RANK, MLA_KV_RANK = 384, 256
DF_HD, DF_V = 32, 64
SSM_P, SSM_N, SSM_GROUPS = 64, 64, 2
N_EXPERTS = 16
CAP_FACTOR = 2

LANES = 128
SUBLANES = 8
TM = 256
R_CTX = B_CTX * L_CTX
R_LAT = B_LAT * L_LAT
R_ALL = R_CTX + R_LAT
NT_CTX = R_CTX // TM
NT_LAT = R_LAT // TM
NT_ALL = R_ALL // TM
TILES_PER_LAT_BATCH = L_LAT // TM
TQ = 256
ATT_TQ = 512
SSD_Q = 256
MOE_W = 64
MOE_ALIGN = 16
MOE_WIN = MOE_W + MOE_ALIGN
SELECT_BISECT_STEPS = 40
FFN_TM = 512
NA_QROWS = 4
NA_KROWS = NA_QROWS + NA_KH - 1
NA_NK = NA_KROWS * GRID_W
VMEM_LIMIT = 56 * 1024 * 1024


def _cparams(sem):
    return pltpu.CompilerParams(dimension_semantics=sem, vmem_limit_bytes=VMEM_LIMIT)


def _nt(a, b):
    return lax.dot_general(a, b, (((1,), (1,)), ((), ())), preferred_element_type=F32)


def _dot(a, b):
    return jnp.dot(a, b, preferred_element_type=F32)


def _rms(x, g):
    return x * lax.rsqrt(jnp.mean(x * x, axis=-1, keepdims=True) + EPS) * g


def _silu(x):
    return x * jax.nn.sigmoid(x)


def _softmax_rows(s):
    m = jnp.max(s, axis=-1, keepdims=True)
    e = jnp.exp(s - m)
    return e * (1.0 / jnp.sum(e, axis=-1, keepdims=True))


LOG2E = math.log2(math.e)


def _exp_rows(s, scale):
    m = jnp.max(s, axis=-1, keepdims=True)
    e = jnp.exp2((s - m) * (scale * LOG2E))
    return e, jnp.sum(e, axis=-1, keepdims=True)


def _exp_only(s, scale):
    return jnp.exp2((s - jnp.max(s, axis=-1, keepdims=True)) * (scale * LOG2E))


def _pv_normalised(e, v_ext):
    o = _dot(e.astype(BF16), v_ext)
    return o[:, 0:64] * (1.0 / o[:, 64:65])


def _store_v_ext(v_s, row0, v):
    rows = v.shape[0]
    lane = _iota((rows, LANES), 1)
    ones_col = jnp.where(lane == 64, 1.0, 0.0)
    for pair in range(2):
        blk = v[:, LANES * pair:LANES * pair + LANES]
        even = jnp.where(lane < 64, blk, ones_col)
        odd = jnp.where(lane < 64, pltpu.roll(blk, 64, 1), ones_col)
        v_s[row0:row0 + rows, 2 * LANES * pair:2 * LANES * pair + LANES] = even.astype(BF16)
        v_s[row0:row0 + rows, 2 * LANES * pair + LANES:2 * LANES * pair + 2 * LANES] = odd.astype(BF16)


def _split3(a):
    a1 = a.astype(BF16)
    r1 = a - a1.astype(F32)
    a2 = r1.astype(BF16)
    a3 = (r1 - a2.astype(F32)).astype(BF16)
    return a1, a2, a3


def _iota(shape, dim):
    return lax.broadcasted_iota(jnp.int32, shape, dim)


def _mod_row(i):
    return jnp.where(i < NT_CTX, 0, 1 + (i - NT_CTX) // TILES_PER_LAT_BATCH)


MOD_TN = 1536


def _mod_kernel(c_ref, w_ref, b_ref, o_ref):
    s = _silu(c_ref[...]).astype(BF16)
    o_ref[0] = _dot(s, w_ref[0].astype(BF16)) + b_ref[0]


def _modulation(cvec, mod_w, mod_b):
    n = 6 * D
    return pl.pallas_call(
        _mod_kernel,
        out_shape=jax.ShapeDtypeStruct((DEPTH, 16, n), F32),
        grid=(DEPTH, n // MOD_TN),
        in_specs=[pl.BlockSpec((16, D), lambda l, j: (0, 0)),
                  pl.BlockSpec((1, D, MOD_TN), lambda l, j: (l, 0, j)),
                  pl.BlockSpec((1, 1, MOD_TN), lambda l, j: (l, 0, j))],
        out_specs=pl.BlockSpec((1, 16, MOD_TN), lambda l, j: (l, 0, j)),
        compiler_params=_cparams(("parallel", "parallel")),
        name="modulation",
    )(cvec, mod_w, mod_b.reshape(DEPTH, 1, n))


W_NA, W_MLA, W_DF, W_SSM, W_GATE = 768, 768, 768, 896, 4 * D


W_IN_COLS = np.cumsum([0, W_NA, W_MLA, W_DF, W_SSM, W_GATE])
W_IN_ALL = int(W_IN_COLS[-1])


def _ctx_tile(i):
    return jnp.minimum(i, NT_CTX - 1)


def _lat_tile(i):
    return jnp.maximum(i - NT_CTX, 0)


def _pick_group(i, ctx_ref, lat_ref):
    return jnp.where(i < NT_CTX, ctx_ref[...], lat_ref[...])


def _in_kernel(xc_ref, xl_ref, mod_ref, g_ref, w_ref, wdt_t,
               ona, omla, odf, ossm, ogate, odt_t, onak, onav, odfk, odfv, okr):
    i = pl.program_id(0)
    x = _pick_group(i, xc_ref, xl_ref)
    h = _rms(x, g_ref[...]) * (1.0 + mod_ref[0, 1:2, :]) + mod_ref[0, 0:1, :]
    hb = h.astype(BF16)
    c = W_IN_COLS
    una = _dot(hb, w_ref[:, c[0]:c[1]])
    umla = _dot(hb, w_ref[:, c[1]:c[2]])
    udf = _dot(hb, w_ref[:, c[2]:c[3]])
    ona[...] = una
    omla[...] = umla
    odf[...] = udf
    ossm[...] = _dot(hb, w_ref[:, c[3]:c[4]])
    ogate[...] = jax.nn.sigmoid(_dot(hb, w_ref[:, c[4]:c[5]])).astype(BF16)
    odt_t[...] = _nt(wdt_t[...], hb)

    @pl.when(i < NT_CTX)
    def _():
        onak[...] = una[:, 256:512]
        onav[...] = una[:, 512:768]
        odfk[...] = udf[:, 256:512]
        odfv[...] = udf[:, 512:768]
        okr[...] = umla[:, 704:736]


def _in_proj(x_ctx, x_lat, mod_l, norm_g, w_all, w_dt_t):
    widths = (W_NA, W_MLA, W_DF, W_SSM)
    const = lambda i: (0, 0)
    row = lambda i: (i, 0)
    ctx_row = lambda i: (_ctx_tile(i), 0)
    out_shape = [jax.ShapeDtypeStruct((R_ALL, w), F32) for w in widths]
    out_shape += [jax.ShapeDtypeStruct((R_ALL, W_GATE), BF16), jax.ShapeDtypeStruct((8, R_ALL), F32)]
    out_shape += [jax.ShapeDtypeStruct((R_CTX, 256), F32)] * 4 + [jax.ShapeDtypeStruct((R_CTX, MLA_ROPE), F32)]
    out_specs = [pl.BlockSpec((TM, w), row) for w in widths]
    out_specs += [pl.BlockSpec((TM, W_GATE), row), pl.BlockSpec((8, TM), lambda i: (0, i))]
    out_specs += [pl.BlockSpec((TM, 256), ctx_row)] * 4 + [pl.BlockSpec((TM, MLA_ROPE), ctx_row)]
    return pl.pallas_call(
        _in_kernel,
        out_shape=out_shape,
        grid=(NT_ALL,),
        in_specs=[pl.BlockSpec((TM, D), ctx_row),
                  pl.BlockSpec((TM, D), lambda i: (_lat_tile(i), 0)),
                  pl.BlockSpec((1, 6, D), lambda i: (_mod_row(i), 0, 0)),
                  pl.BlockSpec((1, D), const),
                  pl.BlockSpec((D, W_IN_ALL), const, pipeline_mode=pl.Buffered(1)),
                  pl.BlockSpec((8, D), const, pipeline_mode=pl.Buffered(1))],
        out_specs=out_specs,
        compiler_params=_cparams(("arbitrary",)),
        name="in_proj",
    )(x_ctx, x_lat, mod_l, norm_g, w_all, w_dt_t)


def _rope(x, cos, sin_a, sin_b):
    n = x.shape[-1]
    nxt = pltpu.roll(x, n - 1, 1)
    prv = pltpu.roll(x, 1, 1)
    return x * cos + nxt * sin_a + prv * sin_b


def _rope_tables32():
    t = np.arange(L_LAT)
    quarter = 8
    inv = ROPE_BASE ** (-np.arange(quarter, dtype=np.float64) / quarter)
    rows = (t // GRID_W).astype(np.float64)[:, None]
    cols = (t % GRID_W).astype(np.float64)[:, None]
    ang = np.concatenate([rows * inv, cols * inv], axis=-1)
    cos = np.repeat(np.cos(ang), 2, axis=-1)
    sin = np.repeat(np.sin(ang), 2, axis=-1)
    even = (np.arange(32) % 2 == 0)[None, :]
    sin_a = np.where(even, -sin, 0.0)
    sin_b = np.where(even, 0.0, sin)
    return tuple(np.asarray(a, np.float32) for a in (cos, sin_a, sin_b))


def _attn_ctx_kernel(u_ref, o_ref, v_s):
    scale = 64 ** -0.5
    _store_v_ext(v_s, 0, u_ref[:, 512:768])
    s = jnp.concatenate([_nt(u_ref[:, 64 * h:64 * h + 64].astype(BF16),
                             u_ref[:, 256 + 64 * h:256 + 64 * h + 64].astype(BF16)) for h in range(N_HEADS)], axis=0)
    e = _exp_only(s, scale)
    for h in range(N_HEADS):
        o_ref[:, 64 * h:64 * h + 64] = _pv_normalised(e[L_CTX * h:L_CTX * h + L_CTX], v_s[:, 128 * h:128 * h + 128])


def _attn_ctx(u_na):
    return pl.pallas_call(
        _attn_ctx_kernel,
        out_shape=jax.ShapeDtypeStruct((R_CTX, 256), F32),
        grid=(B_CTX,),
        in_specs=[pl.BlockSpec((L_CTX, W_NA), lambda b: (b, 0))],
        out_specs=pl.BlockSpec((L_CTX, 256), lambda b: (b, 0)),
        scratch_shapes=[pltpu.VMEM((L_CTX, 4 * LANES), BF16)],
        compiler_params=_cparams(("parallel",)),
        name="na_ctx",
    )(u_na)


NA_ROWS = L_LAT // GRID_W
NA_NQT = NA_ROWS // NA_QROWS


def _na_pattern(qt):
    return jnp.where(qt == 0, 0, jnp.where(qt == NA_NQT - 1, 2, 1))


def _na_pattern_offsets(p, a):
    q_off = jnp.where(p == 0, 0, jnp.where(p == 1, NA_KH // 2, NA_KROWS - NA_QROWS))
    rs_rel = jnp.where(p == 0, 0, jnp.where(p == 1, a, NA_KROWS - NA_KH))
    return q_off, rs_rel


def _check_na_patterns():
    for qt in range(NA_NQT):
        ks = int(np.clip(NA_QROWS * qt - NA_KH // 2, 0, NA_ROWS - NA_KROWS))
        p = 0 if qt == 0 else (2 if qt == NA_NQT - 1 else 1)
        for a in range(NA_QROWS):
            r = NA_QROWS * qt + a
            rs = int(np.clip(r - NA_KH // 2, 0, NA_ROWS - NA_KH))
            q_off = (0, NA_KH // 2, NA_KROWS - NA_QROWS)[p]
            rs_rel = (0, a, NA_KROWS - NA_KH)[p]
            assert NA_QROWS * qt - ks == q_off and rs - ks == rs_rel and rs + NA_KH <= ks + NA_KROWS


_check_na_patterns()


def _na_bias_kernel(rb_ref, o_ref):
    p = pl.program_id(0)
    qc = _iota((GRID_W, GRID_W), 0)
    kc = _iota((GRID_W, GRID_W), 1)
    cs = jnp.clip(qc - NA_KW // 2, 0, GRID_W - NA_KW)
    col_ok = (kc >= cs) & (kc < cs + NA_KW)
    for a in range(NA_QROWS):
        q_off, rs_rel = _na_pattern_offsets(p, a)
        for b in range(NA_KROWS):
            row_ok = (b >= rs_rel) & (b < rs_rel + NA_KH)
            dr = jnp.clip(b - q_off - a + NA_KH - 1, 0, 2 * NA_KH - 2)
            v = jnp.broadcast_to(rb_ref[0, pl.ds(dr, 1), :], (GRID_W, LANES))
            t = pltpu.roll(v, LANES - (NA_KW - 1), 1, stride=1, stride_axis=0)[:, :GRID_W]
            o_ref[0, 0, GRID_W * a:GRID_W * a + GRID_W, GRID_W * b:GRID_W * b + GRID_W] = jnp.where(
                col_ok & row_ok, t * LOG2E, -1e30)


def _na_bias_table(rel_bias):
    rb = jnp.pad(rel_bias.astype(F32), ((0, 0), (0, 0), (0, LANES - (2 * NA_KW - 1))))
    return pl.pallas_call(
        _na_bias_kernel,
        out_shape=jax.ShapeDtypeStruct((3, N_HEADS, TQ, NA_NK), F32),
        grid=(3, N_HEADS),
        in_specs=[pl.BlockSpec((1, 2 * NA_KH - 1, LANES), lambda p, h: (h, 0, 0))],
        out_specs=pl.BlockSpec((1, 1, TQ, NA_NK), lambda p, h: (p, h, 0, 0)),
        compiler_params=_cparams(("parallel", "parallel")),
        name="na_bias",
    )(rb)


def _na_lat_kernel(u_ref, kc_ref, vc_ref, bias_ref, o_ref, v_s, vc_s):
    qt = pl.program_id(1)
    c = 64 ** -0.5 * LOG2E
    ks = jnp.clip(NA_QROWS * qt - NA_KH // 2, 0, NA_ROWS - NA_KROWS)
    kstart = pl.multiple_of(ks * GRID_W, GRID_W)
    qstart = pl.multiple_of(qt * TQ, TQ)
    _store_v_ext(v_s, 0, u_ref[pl.ds(kstart, NA_NK), 512:768])
    _store_v_ext(vc_s, 0, vc_ref[0, 0])
    for h in range(N_HEADS):
        q = u_ref[pl.ds(qstart, TQ), 64 * h:64 * h + 64].astype(BF16)
        k = u_ref[pl.ds(kstart, NA_NK), 256 + 64 * h:256 + 64 * h + 64].astype(BF16)
        kc = kc_ref[0, 0, :, 64 * h:64 * h + 64].astype(BF16)
        t_loc = _nt(q, k) * c + bias_ref[0, h]
        t_ctx = _nt(q, kc) * c
        m = jnp.maximum(jnp.max(t_loc, axis=-1, keepdims=True), jnp.max(t_ctx, axis=-1, keepdims=True))
        e_loc = jnp.exp2(t_loc - m).astype(BF16)
        e_ctx = jnp.exp2(t_ctx - m).astype(BF16)
        o = _dot(e_loc, v_s[:, 128 * h:128 * h + 128]) + _dot(e_ctx, vc_s[:, 128 * h:128 * h + 128])
        o_ref[:, 64 * h:64 * h + 64] = o[:, 0:64] * (1.0 / o[:, 64:65])


def _na_lat(u_na, k_ctx, v_ctx, bias, layer):
    nqt = L_LAT // TQ
    cache = pl.BlockSpec((1, 1, PAST, 256), lambda b, t: (b, layer, 0, 0))
    return pl.pallas_call(
        _na_lat_kernel,
        out_shape=jax.ShapeDtypeStruct((R_LAT, 256), F32),
        grid=(B_LAT, nqt),
        in_specs=[pl.BlockSpec((L_LAT, W_NA), lambda b, t: (R_CTX // L_LAT + b, 0)),
                  cache, cache,
                  pl.BlockSpec((1, N_HEADS, TQ, NA_NK), lambda b, t: (_na_pattern(t), 0, 0, 0))],
        out_specs=pl.BlockSpec((TQ, 256), lambda b, t: (b * nqt + t, 0)),
        scratch_shapes=[pltpu.VMEM((NA_NK, 4 * LANES), BF16), pltpu.VMEM((PAST, 4 * LANES), BF16)],
        compiler_params=_cparams(("parallel", "arbitrary")),
        name="na_lat",
    )(u_na, k_ctx, v_ctx, bias)


def _mla_kernel(*refs, latent, seq):
    if latent:
        (u_ref, gq_ref, gkv_ref, wuq_ref, wuk_ref, wuv_ref, cos_ref, sa_ref, sb_ref,
         ckv_c_ref, kr_c_ref, o_ref, k_s, v_s) = refs
    else:
        (u_ref, gq_ref, gkv_ref, wuq_ref, wuk_ref, wuv_ref, o_ref, ckv_o_ref, k_s, v_s) = refs
    scale = (MLA_NOPE + MLA_ROPE) ** -0.5
    ckv = _rms(u_ref[:, 384:640], gkv_ref[...])
    kr = u_ref[:, 640:768]
    if latent:
        kr = _rope(kr, cos_ref[...], sa_ref[...], sb_ref[...])
    else:
        ckv_o_ref[...] = ckv
    ckv_b = ckv.astype(BF16)
    for h in range(N_HEADS):
        k_s[0:seq, 128 * h:128 * h + 128] = (_dot(ckv_b, wuk_ref[:, 128 * h:128 * h + 128]) + kr).astype(BF16)
    ones_col = jnp.where((_iota((1, 4 * LANES), 1) & (LANES - 1)) == MLA_V, 1.0, 0.0)
    v_s[0:seq, :] = (_dot(ckv_b, wuv_ref[...]) + ones_col).astype(BF16)
    if latent:
        cc = ckv_c_ref[0, 0].astype(BF16)
        krc = kr_c_ref[0]
        for h in range(N_HEADS):
            k_s[seq:seq + PAST, 128 * h:128 * h + 128] = (
                _dot(cc, wuk_ref[:, 128 * h:128 * h + 128]) + krc).astype(BF16)
        v_s[seq:seq + PAST, :] = (_dot(cc, wuv_ref[...]) + ones_col).astype(BF16)

    tq = min(ATT_TQ, seq)

    def q_tile(t, carry):
        r0 = pl.multiple_of(t * tq, tq)
        cq = _rms(u_ref[pl.ds(r0, tq), 0:384], gq_ref[...]).astype(BF16)
        for h in range(N_HEADS):
            q = _dot(cq, wuq_ref[:, 128 * h:128 * h + 128])
            q = _rope(q, cos_ref[pl.ds(r0, tq), :], sa_ref[pl.ds(r0, tq), :], sb_ref[pl.ds(r0, tq), :])
            e = _exp_only(_nt(q.astype(BF16), k_s[:, 128 * h:128 * h + 128]), scale)
            o_ref[pl.ds(r0, tq), 64 * h:64 * h + 64] = _pv_normalised(e, v_s[:, 128 * h:128 * h + 128])
        return carry

    if latent:
        lax.fori_loop(0, seq // tq, q_tile, 0)
    else:
        cq = _rms(u_ref[:, 0:384], gq_ref[...]).astype(BF16)
        s = jnp.concatenate([_nt(_dot(cq, wuq_ref[:, 128 * h:128 * h + 128]).astype(BF16),
                                 k_s[:, 128 * h:128 * h + 128]) for h in range(N_HEADS)], axis=0)
        e = _exp_only(s, scale)
        for h in range(N_HEADS):
            o_ref[:, 64 * h:64 * h + 64] = _pv_normalised(e[seq * h:seq * h + seq], v_s[:, 128 * h:128 * h + 128])


def _mla(u_mla, gq, gkv, wuq, wuk, wuv, rope128=None, ckv_ctx=None, kr_ctx=None, layer=0):
    latent = rope128 is not None
    seq = L_LAT if latent else L_CTX
    nb = B_LAT if latent else B_CTX
    off = R_CTX // L_LAT if latent else 0
    lk = seq + PAST if latent else seq
    const = lambda b: (0, 0)
    in_specs = [pl.BlockSpec((seq, W_MLA), lambda b: (off + b, 0)),
                pl.BlockSpec((1, MLA_Q_RANK), const),
                pl.BlockSpec((1, MLA_KV_RANK), const),
                pl.BlockSpec((MLA_Q_RANK, 512), const),
                pl.BlockSpec((MLA_KV_RANK, 512), const),
                pl.BlockSpec((MLA_KV_RANK, 512), const)]
    args = [u_mla, gq, gkv, wuq, wuk, wuv]
    out_shape = [jax.ShapeDtypeStruct((nb * seq, 256), F32)]
    out_specs = [pl.BlockSpec((seq, 256), lambda b: (b, 0))]
    if latent:
        in_specs += [pl.BlockSpec((seq, LANES), const)] * 3
        in_specs += [pl.BlockSpec((1, 1, PAST, MLA_KV_RANK), lambda b: (b, layer, 0, 0)),
                     pl.BlockSpec((1, PAST, LANES), lambda b: (b, 0, 0))]
        args += list(rope128) + [ckv_ctx, kr_ctx]
    else:
        out_shape.append(jax.ShapeDtypeStruct((nb * seq, MLA_KV_RANK), F32))
        out_specs.append(pl.BlockSpec((seq, MLA_KV_RANK), lambda b: (b, 0)))
    return pl.pallas_call(
        functools.partial(_mla_kernel, latent=latent, seq=seq),
        out_shape=out_shape,
        grid=(nb,),
        in_specs=in_specs,
        out_specs=out_specs,
        scratch_shapes=[pltpu.VMEM((lk, 512), BF16), pltpu.VMEM((lk, 512), BF16)],
        compiler_params=_cparams(("parallel",)),
        name="mla_lat" if latent else "mla_ctx",
    )(*args)


def _df_kernel(*refs, latent, seq, lam_init):
    if latent:
        (u_ref, lv_ref, gs_ref, cos_ref, sa_ref, sb_ref, kc_ref, vc_ref, o_ref, k_s, v_s) = refs
    else:
        (u_ref, lv_ref, gs_ref, o_ref, k_s, v_s) = refs
    scale = DF_HD ** -0.5
    lv = lv_ref[...]
    lam = (jnp.exp(jnp.sum(lv[0:1] * lv[1:2], axis=1, keepdims=True))
           - jnp.exp(jnp.sum(lv[2:3] * lv[3:4], axis=1, keepdims=True)) + lam_init)
    k = u_ref[:, 256:512]
    if latent:
        k = _rope(k, cos_ref[...], sa_ref[...], sb_ref[...])
        k_s[seq:seq + PAST, :] = kc_ref[0, 0].astype(BF16)
        _store_v_ext(v_s, seq, vc_ref[0, 0])
    k_s[0:seq, :] = k.astype(BF16)
    _store_v_ext(v_s, 0, u_ref[:, 512:768])
    tq = min(ATT_TQ, seq)
    first = _iota((tq, 64), 1) < DF_HD

    def q_tile(t, carry):
        r0 = pl.multiple_of(t * tq, tq)
        q = u_ref[pl.ds(r0, tq), 0:256]
        if latent:
            q = _rope(q, cos_ref[pl.ds(r0, tq), :], sa_ref[pl.ds(r0, tq), :], sb_ref[pl.ds(r0, tq), :])
        for h in range(N_HEADS):
            qh = q[:, 64 * h:64 * h + 64]
            kh = k_s[:, 64 * h:64 * h + 64]
            q0 = jnp.where(first, qh, 0.0).astype(BF16)
            q1 = jnp.where(first, 0.0, qh).astype(BF16)
            vh = v_s[:, 128 * h:128 * h + 128]
            o = (_pv_normalised(_exp_only(_nt(q0, kh), scale), vh)
                 - lam * _pv_normalised(_exp_only(_nt(q1, kh), scale), vh))
            o_ref[pl.ds(r0, tq), 64 * h:64 * h + 64] = _rms(o, gs_ref[...]) * (1.0 - lam_init)
        return carry

    if latent:
        lax.fori_loop(0, seq // tq, q_tile, 0)
    else:
        q = u_ref[:, 0:256]
        blocks = []
        for h in range(N_HEADS):
            qh = q[:, 64 * h:64 * h + 64]
            kh = k_s[:, 64 * h:64 * h + 64]
            blocks.append(_nt(jnp.where(first, qh, 0.0).astype(BF16), kh))
            blocks.append(_nt(jnp.where(first, 0.0, qh).astype(BF16), kh))
        e = _exp_only(jnp.concatenate(blocks, axis=0), scale)
        for h in range(N_HEADS):
            vh = v_s[:, 128 * h:128 * h + 128]
            o = (_pv_normalised(e[2 * h * seq:(2 * h + 1) * seq], vh)
                 - lam * _pv_normalised(e[(2 * h + 1) * seq:(2 * h + 2) * seq], vh))
            o_ref[:, 64 * h:64 * h + 64] = _rms(o, gs_ref[...]) * (1.0 - lam_init)


def _df(u_df, lam_vec, g_sub, lam_init, rope256=None, k_ctx=None, v_ctx=None, layer=0):
    latent = rope256 is not None
    seq = L_LAT if latent else L_CTX
    nb = B_LAT if latent else B_CTX
    off = R_CTX // L_LAT if latent else 0
    lk = seq + PAST if latent else seq
    const = lambda b: (0, 0)
    in_specs = [pl.BlockSpec((seq, W_DF), lambda b: (off + b, 0)),
                pl.BlockSpec((4, DF_HD), const),
                pl.BlockSpec((1, DF_V), const)]
    args = [u_df, lam_vec, g_sub]
    if latent:
        in_specs += [pl.BlockSpec((seq, 256), const)] * 3
        in_specs += [pl.BlockSpec((1, 1, PAST, 256), lambda b: (b, layer, 0, 0))] * 2
        args += list(rope256) + [k_ctx, v_ctx]
    return pl.pallas_call(
        functools.partial(_df_kernel, latent=latent, seq=seq, lam_init=lam_init),
        out_shape=jax.ShapeDtypeStruct((nb * seq, 256), F32),
        grid=(nb,),
        in_specs=in_specs,
        out_specs=pl.BlockSpec((seq, 256), lambda b: (b, 0)),
        scratch_shapes=[pltpu.VMEM((lk, 256), BF16), pltpu.VMEM((lk, 512), BF16)],
        compiler_params=_cparams(("parallel",)),
        name="df_lat" if latent else "df_ctx",
    )(*args)


def _softplus(x):
    return jnp.maximum(x, 0.0) + jnp.log1p(jnp.exp(-jnp.abs(x)))


def _ssd_kernel(*refs, latent, seq):
    if latent:
        (u_ref, dtt_ref, cw_ref, cb_ref, dtb_c_ref, dtb_r_ref, a_c_ref, a_r_ref, dvec_ref, gn_ref, h0_ref,
         o_ref, xs_s, bm_s, cm_s, dtc_s, dtr_s, y_s, st_s) = refs
    else:
        (u_ref, dtt_ref, cw_ref, cb_ref, dtb_c_ref, dtb_r_ref, a_c_ref, a_r_ref, dvec_ref, gn_ref,
         o_ref, st_o_ref, xs_s, bm_s, cm_s, dtc_s, dtr_s, y_s, st_s) = refs
    q = SSD_Q
    nchunk = seq // q

    def conv(a, w, b):
        row = _iota(a.shape, 0)
        prv = jnp.where(row == 0, 0.0, pltpu.roll(a, 1, 0))
        nxt = jnp.where(row == seq - 1, 0.0, pltpu.roll(a, seq - 1, 0))
        return _silu(w[0:1] * prv + w[1:2] * a + w[2:3] * nxt + b)

    cw = cw_ref[...]
    cb = cb_ref[...]
    xs_s[...] = conv(u_ref[:, 0:256], cw[:, 0:256], cb[:, 0:256])
    bm_s[...] = conv(u_ref[:, 512:640], cw[:, 256:384], cb[:, 256:384])
    cm_s[...] = conv(u_ref[:, 640:768], cw[:, 384:512], cb[:, 384:512])
    dtc_s[...] = _softplus(u_ref[:, 768:776] + dtb_c_ref[...])
    dtr_s[...] = _softplus(dtt_ref[...] + dtb_r_ref[...])
    eye_n = jnp.where(_iota((SSM_N, SSM_N), 0) == _iota((SSM_N, SSM_N), 1), 1.0, 0.0).astype(BF16)

    def transpose64(a):
        return sum(_nt(eye_n, p) for p in _split3(a))

    hpg = N_HEADS // SSM_GROUPS
    st_s[...] = jnp.zeros(st_s.shape, F32)
    if latent:
        for d in range(2):
            for h in range(N_HEADS):
                g = h // hpg
                st_s[d, 64 * g:64 * g + 64, 64 * h:64 * h + 64] = transpose64(h0_ref[0, 0, d, h])

    ri = _iota((q, q), 0)
    ci = _iota((q, q), 1)
    lower = ri >= ci
    upper = ri <= ci
    a_col = -jnp.exp(a_c_ref[...])
    a_row = -jnp.exp(a_r_ref[...])
    head_of_lane = _iota((1, 256), 1) >> 6
    own_group = (_iota((2 * SSM_N, 256), 0) >> 6) == (_iota((2 * SSM_N, 256), 1) >> 7)

    def chunk(c, d):
        c0 = pl.multiple_of(c * q, q)
        xs = xs_s[pl.ds(c0, q), :]
        bm_b = bm_s[pl.ds(c0, q), :].astype(BF16)
        cm_b = cm_s[pl.ds(c0, q), :].astype(BF16)
        dtc = dtc_s[pl.ds(c0, q), :]
        mask = lower if d == 0 else upper
        tri = jnp.where(mask, 1.0, 0.0).astype(BF16)
        tri_t = jnp.where(upper if d == 0 else lower, 1.0, 0.0).astype(BF16)
        acum_c = sum(_dot(tri, p) for p in _split3(dtc * a_col))
        acum_r = sum(_dot(p, tri_t) for p in _split3(dtr_s[:, pl.ds(c0, q)] * a_row))
        spread = jnp.where(_iota((8, 256), 0) == N_HEADS * d + (_iota((8, 256), 1) >> 6), 1.0, 0.0).astype(BF16)
        a_exp = sum(_dot(p, spread) for p in _split3(acum_c))
        dt_exp = sum(_dot(p, spread) for p in _split3(dtc))
        a_end = a_exp[q - 1:q, :] if d == 0 else a_exp[0:1, :]
        xdt = xs * dt_exp
        xdt_b = xdt.astype(BF16)
        st = st_s[d]
        y = _dot(cm_b, st.astype(BF16)) * jnp.exp(a_exp)
        for g in range(SSM_GROUPS):
            cb_g = _nt(cm_b[:, 64 * g:64 * g + 64], bm_b[:, 64 * g:64 * g + 64])
            for hh in range(hpg):
                h = g * hpg + hh
                j = N_HEADS * d + h
                seg = acum_c[:, j:j + 1] - acum_r[j:j + 1, :]
                decay = jnp.where(mask, jnp.exp(jnp.where(mask, seg, 0.0)), 0.0)
                x_h = jnp.where(head_of_lane == h, xdt_b, jnp.zeros_like(xdt_b))
                y = y + _dot((cb_g * decay).astype(BF16), x_h)
        if d == 0:
            y_s[pl.ds(c0, q), :] = y
        else:
            y_s[pl.ds(c0, q), :] = y_s[pl.ds(c0, q), :] + y
        xw = (xdt * jnp.exp(a_end - a_exp)).astype(BF16)
        upd = _dot(bm_s[pl.ds(c0, q), :].T.astype(BF16), xw)
        st_s[d] = st * jnp.exp(a_end) + jnp.where(own_group, upd, 0.0)

    def fwd(c, carry):
        chunk(c, 0)
        return carry

    def bwd(c, carry):
        chunk(nchunk - 1 - c, 1)
        return carry

    lax.fori_loop(0, nchunk, fwd, 0)
    lax.fori_loop(0, nchunk, bwd, 0)
    y = y_s[...] + dvec_ref[...] * xs_s[...]
    o_ref[...] = _rms(y * _silu(u_ref[:, 256:512]), gn_ref[...])
    if not latent:
        for d in range(2):
            for h in range(N_HEADS):
                g = h // hpg
                st_o_ref[0, d, h] = transpose64(st_s[d, 64 * g:64 * g + 64, 64 * h:64 * h + 64])


def _ssd(u_ssm, dt_t, conv_w, conv_b, dt_bias, a_log, d_vec, g_norm, h0=None, layer=0):
    latent = h0 is not None
    seq = L_LAT if latent else L_CTX
    nb = B_LAT if latent else B_CTX
    off = R_CTX // L_LAT if latent else 0
    const = lambda b: (0, 0)
    dtb = dt_bias.reshape(1, 8)
    alg = a_log.reshape(1, 8)
    in_specs = [pl.BlockSpec((seq, W_SSM), lambda b: (off + b, 0)),
                pl.BlockSpec((8, seq), lambda b: (0, off + b)),
                pl.BlockSpec((3, 512), const), pl.BlockSpec((1, 512), const),
                pl.BlockSpec((1, 8), const), pl.BlockSpec((8, 1), const),
                pl.BlockSpec((1, 8), const), pl.BlockSpec((8, 1), const),
                pl.BlockSpec((1, 256), const), pl.BlockSpec((1, 256), const)]
    args = [u_ssm, dt_t, conv_w, conv_b.reshape(1, 512), dtb, dtb.reshape(8, 1), alg, alg.reshape(8, 1),
            d_vec, g_norm]
    out_shape = [jax.ShapeDtypeStruct((nb * seq, 256), F32)]
    out_specs = [pl.BlockSpec((seq, 256), lambda b: (b, 0))]
    if latent:
        in_specs.append(pl.BlockSpec((1, 1, 2, N_HEADS, SSM_P, SSM_N), lambda b: (b, layer, 0, 0, 0, 0)))
        args.append(h0)
    else:
        out_shape.append(jax.ShapeDtypeStruct((nb, 2, N_HEADS, SSM_P, SSM_N), F32))
        out_specs.append(pl.BlockSpec((1, 2, N_HEADS, SSM_P, SSM_N), lambda b: (b, 0, 0, 0, 0)))
    scratch = [pltpu.VMEM((seq, 256), F32), pltpu.VMEM((seq, 128), F32), pltpu.VMEM((seq, 128), F32),
               pltpu.VMEM((seq, 8), F32), pltpu.VMEM((8, seq), F32), pltpu.VMEM((seq, 256), F32),
               pltpu.VMEM((2, SSM_GROUPS * SSM_N, 256), F32)]
    return pl.pallas_call(
        functools.partial(_ssd_kernel, latent=latent, seq=seq),
        out_shape=out_shape,
        grid=(nb,),
        in_specs=in_specs,
        out_specs=out_specs,
        scratch_shapes=scratch,
        compiler_params=_cparams(("parallel",)),
        name="ssd_lat" if latent else "ssd_ctx",
    )(*args)


def _merge_kernel(xc_ref, xl_ref, mod_ref, c0, l0, c1, l1, c2, l2, c3, l3, gate_ref, wb_ref, wo_ref, g2_ref,
                  wr_ref, xo_ref, h2_ref, aff_ref, afft_ref):
    i = pl.program_id(0)
    acc = None
    for b, (bc, bl) in enumerate(((c0, l0), (c1, l1), (c2, l2), (c3, l3))):
        proj = _dot(_pick_group(i, bc, bl).astype(BF16), wb_ref[b])
        term = gate_ref[:, D * b:D * b + D].astype(F32) * proj
        acc = term if acc is None else acc + term
    x = _pick_group(i, xc_ref, xl_ref) + mod_ref[0, 2:3, :] * _dot(acc.astype(BF16), wo_ref[...])
    xo_ref[...] = x
    h2 = _rms(x, g2_ref[...]) * (1.0 + mod_ref[0, 4:5, :]) + mod_ref[0, 3:4, :]
    hb = h2.astype(BF16)
    h2_ref[...] = hb
    hl = (h2 - hb.astype(F32)).astype(BF16)
    wr = wr_ref[...]
    wh = wr.astype(BF16)
    wl = (wr - wh.astype(F32)).astype(BF16)
    logits = _dot(hb, wh) + _dot(hl, wh) + _dot(hb, wl)
    aff = _softmax_rows(logits)
    aff_ref[...] = aff
    eye = jnp.where(_iota((N_EXPERTS, N_EXPERTS), 0) == _iota((N_EXPERTS, N_EXPERTS), 1), 1.0, 0.0).astype(BF16)
    afft_ref[...] = sum(_nt(eye, p) for p in _split3(aff))


def _merge(x_ctx, x_lat, mod_l, branches, gates, wb, wo, g2, wr):
    const = lambda i: (0, 0)
    row = lambda i: (i, 0)
    ctx_row = lambda i: (_ctx_tile(i), 0)
    lat_row = lambda i: (_lat_tile(i), 0)
    return pl.pallas_call(
        _merge_kernel,
        out_shape=[jax.ShapeDtypeStruct((R_ALL, D), F32), jax.ShapeDtypeStruct((R_ALL, D), BF16),
                   jax.ShapeDtypeStruct((R_ALL, N_EXPERTS), F32), jax.ShapeDtypeStruct((N_EXPERTS, R_ALL), F32)],
        grid=(NT_ALL,),
        in_specs=[pl.BlockSpec((TM, D), ctx_row), pl.BlockSpec((TM, D), lat_row),
                  pl.BlockSpec((1, 6, D), lambda i: (_mod_row(i), 0, 0))]
                 + [pl.BlockSpec((TM, 256), ctx_row), pl.BlockSpec((TM, 256), lat_row)] * 4
                 + [pl.BlockSpec((TM, W_GATE), row),
                    pl.BlockSpec((4, 256, D), lambda i: (0, 0, 0), pipeline_mode=pl.Buffered(1)),
                    pl.BlockSpec((D, D), const, pipeline_mode=pl.Buffered(1)),
                    pl.BlockSpec((1, D), const),
                    pl.BlockSpec((D, N_EXPERTS), const)],
        out_specs=[pl.BlockSpec((TM, D), row), pl.BlockSpec((TM, D), row),
                   pl.BlockSpec((TM, N_EXPERTS), row), pl.BlockSpec((N_EXPERTS, TM), lambda i: (0, i))],
        compiler_params=_cparams(("parallel",)),
        name="merge_router",
    )(x_ctx, x_lat, mod_l, *[a for pair in branches for a in pair], gates, wb, wo, g2, wr)


def _select_kernel(afft_ref, lpos_t_ref, lpos_ref, tstart_ref, cnt_ref, gt_s, eq_s, need_s, carry_s, *, cap):
    t = pl.program_id(0)
    ne = N_EXPERTS

    @pl.when(t == 0)
    def _():
        aff = afft_ref[...]

        def count_ge(v):
            return jnp.sum(jnp.where(aff >= v, 1.0, 0.0), axis=1, keepdims=True)

        def bisect(_, lh):
            lo, hi = lh
            mid = jnp.where(lo > 0.0, jnp.sqrt(lo) * jnp.sqrt(hi), hi * 2.0 ** -32)
            mid = jnp.clip(mid, lo, hi)
            ok = count_ge(mid) >= cap
            return jnp.where(ok, mid, lo), jnp.where(ok, hi, mid)

        _, hi = lax.fori_loop(0, SELECT_BISECT_STEPS, bisect,
                              (jnp.zeros((ne, 1), F32), jnp.full((ne, 1), 2.0, F32)))

        def short(st):
            return jnp.min(st[1]) < cap

        def peel(st):
            bound, cnt = st
            nxt = jnp.max(jnp.where(aff < bound, aff, -1.0), axis=1, keepdims=True)
            upd = cnt < cap
            return jnp.where(upd, nxt, bound), jnp.where(upd, count_ge(nxt), cnt)

        thr, _ = lax.while_loop(short, peel, (hi, count_ge(hi)))
        gt = jnp.where(aff > thr, 1.0, 0.0)
        gt_s[...] = gt
        eq_s[...] = jnp.where(aff == thr, 1.0, 0.0)
        need_col = cap - jnp.sum(gt, axis=1, keepdims=True)
        eye = _iota((ne, ne), 0) == _iota((ne, ne), 1)
        need_s[...] = jnp.sum(jnp.where(eye, need_col, 0.0), axis=0, keepdims=True)
        carry_s[...] = jnp.zeros(carry_s.shape, F32)

    sl = pl.ds(pl.multiple_of(t * TM, TM), TM)
    eye_t = jnp.where(_iota((TM, TM), 0) == _iota((TM, TM), 1), 1.0, 0.0).astype(BF16)
    eye_e = jnp.where(_iota((ne, ne), 0) == _iota((ne, ne), 1), 1.0, 0.0).astype(BF16)
    before = jnp.where(_iota((TM, TM), 0) > _iota((TM, TM), 1), 1.0, 0.0).astype(BF16)
    gtm = _nt(eye_t, gt_s[:, sl].astype(BF16))
    eqm = _nt(eye_t, eq_s[:, sl].astype(BF16))
    eq_seen = carry_s[0:1, :]
    pos0 = carry_s[1:2, :]
    eq_rank = _dot(before, eqm.astype(BF16)) + eq_seen
    sel = jnp.maximum(gtm, eqm * jnp.where(eq_rank < need_s[...], 1.0, 0.0))
    lp = _dot(before, sel.astype(BF16))
    cnt = jnp.sum(sel, axis=0, keepdims=True)
    lpos_ref[...] = jnp.where(sel > 0.0, lp, -1.0)
    lp_t = _nt(eye_e, lp.astype(BF16))
    sel_t = _nt(eye_e, sel.astype(BF16))
    lpos_t_ref[...] = jnp.where(sel_t > 0.0, lp_t, -1.0)
    tstart_ref[0] = pos0.astype(jnp.int32)
    cnt_ref[0] = cnt.astype(jnp.int32)
    carry_s[0:1, :] = eq_seen + jnp.sum(eqm, axis=0, keepdims=True)
    carry_s[1:2, :] = pos0 + cnt


def _select(afft, cap):
    n = afft.shape[1]
    nt = n // TM
    ne = N_EXPERTS
    return pl.pallas_call(
        functools.partial(_select_kernel, cap=cap),
        out_shape=[jax.ShapeDtypeStruct((ne, n), F32), jax.ShapeDtypeStruct((n, ne), F32),
                   jax.ShapeDtypeStruct((nt, 1, ne), jnp.int32), jax.ShapeDtypeStruct((nt, 1, ne), jnp.int32)],
        grid=(nt,),
        in_specs=[pl.BlockSpec((ne, n), lambda t: (0, 0))],
        out_specs=[pl.BlockSpec((ne, TM), lambda t: (0, t)), pl.BlockSpec((TM, ne), lambda t: (t, 0)),
                   pl.BlockSpec((1, 1, ne), lambda t: (t, 0, 0)), pl.BlockSpec((1, 1, ne), lambda t: (t, 0, 0))],
        scratch_shapes=[pltpu.VMEM((ne, n), F32), pltpu.VMEM((ne, n), F32),
                        pltpu.VMEM((1, ne), F32), pltpu.VMEM((8, ne), F32)],
        compiler_params=_cparams(("arbitrary",)),
        name="moe_select",
    )(afft)


def _win_index(idx):
    e = jnp.zeros_like(idx)
    for k in range(1, N_EXPERTS):
        e = e + jnp.where(idx >= k * MOE_WIN, 1, 0)
    return e, idx - e * MOE_WIN


def _gather_consts():
    ne, win, al = N_EXPERTS, MOE_WIN, MOE_ALIGN
    rows = np.arange(ne * win)
    ex = (rows[:, None] // win == np.arange(ne)[None, :]).astype(np.float32)
    rows16 = np.arange(ne * al)
    ex16 = (rows16[:, None] // al == np.arange(ne)[None, :]).astype(np.float32)
    return (jnp.asarray(np.concatenate([ex, ex], axis=1), BF16),
            jnp.asarray(np.broadcast_to((rows % win)[:, None], (ne * win, TM)), F32),
            jnp.asarray(np.concatenate([ex16, ex16], axis=1), BF16),
            jnp.asarray(np.broadcast_to((rows16 % al)[:, None], (ne * al, TM)), F32))


def _gather_kernel(tstart_sm, cnt_sm, rounds_sm, h2_ref, lpos_t_ref, ex_ref, row_ref, ex16_ref, row16_ref,
                   xe_ref, stage, pend, sem, *, nt, cap):
    i = pl.program_id(0)
    slot = lax.rem(i, 2)
    ne, w, win, al = N_EXPERTS, MOE_W, MOE_WIN, MOE_ALIGN

    @pl.when(i == 0)
    def _():
        pend[...] = jnp.zeros(pend.shape, F32)

    def first_slot(e):
        return tstart_sm[i * ne + e]

    def below(e):
        return first_slot(e) & (al - 1)

    lp_t = lpos_t_ref[...]
    lp_t = jnp.where(lp_t < 0.0, -1e6, lp_t)
    sub = _iota((ne, TM), 0)
    shift = jnp.zeros((ne, TM), F32)
    shift16 = jnp.zeros((ne, TM), F32)
    for e in range(ne):
        shift = jnp.where(sub == e, below(e).astype(F32), shift)
        filled = below(e) + cnt_sm[i * ne + e]
        shift16 = jnp.where(sub == e, (below(e) - ((filled >> 4) << 4)).astype(F32), shift16)
    tgt = _dot(ex_ref[...], jnp.concatenate([lp_t, shift], axis=0).astype(BF16))

    def build(k, sl):
        oh = jnp.where(tgt == row_ref[...] + (k * w).astype(F32), 1.0, 0.0).astype(BF16)
        return _dot(oh, h2_ref[...])

    def copy(e, start, sl):
        return pltpu.make_async_copy(
            stage.at[sl, pl.ds(e * win, win)],
            xe_ref.at[e, pl.ds(pl.multiple_of(start, al), win)],
            sem.at[sl])

    def issue(k, sl):
        for e in range(ne):
            copy(e, jnp.minimum(first_slot(e) - below(e) + k * w, cap), sl).start()

    def wait_all(sl):
        for e in range(ne):
            copy(e, 0, sl).wait()

    zero = jnp.int32(0)
    g = build(zero, slot)
    stage[slot] = g.astype(BF16)
    for e in range(ne):
        stage[slot, e * win:e * win + al, :] = (g[e * win:e * win + al] + pend[e]).astype(BF16)

    tgt16 = _dot(ex16_ref[...], jnp.concatenate([lp_t, shift16], axis=0).astype(BF16))
    oh16 = jnp.where(tgt16 == row16_ref[...], 1.0, 0.0).astype(BF16)
    new_pend = _dot(oh16, h2_ref[...])
    for e in range(ne):
        keep = jnp.where(below(e) + cnt_sm[i * ne + e] < al, 1.0, 0.0)
        pend[e] = new_pend[e * al:(e + 1) * al] + keep * pend[e]

    @pl.when(i > 0)
    def _():
        wait_all(1 - slot)

    issue(zero, slot)

    def extra(k, carry):
        wait_all(slot)
        stage[slot] = build(k, slot).astype(BF16)
        issue(k, slot)
        return carry

    lax.fori_loop(1, rounds_sm[i], extra, 0)

    @pl.when(i == nt - 1)
    def _():
        wait_all(slot)
        stage[1 - slot, 0:win, :] = jnp.zeros((win, D), BF16)
        for e in range(ne):
            pltpu.make_async_copy(
                stage.at[1 - slot, pl.ds(0, win)], xe_ref.at[e, pl.ds(cap, win)], sem.at[1 - slot]).start()
        wait_all(1 - slot)


def _gather(h2b, lpos_t, tstart, cnt, rounds, cap, tile_off):
    n = lpos_t.shape[1]
    nt = n // TM
    ne, win = N_EXPERTS, MOE_WIN
    consts = _gather_consts()
    grid_spec = pltpu.PrefetchScalarGridSpec(
        num_scalar_prefetch=3,
        grid=(nt,),
        in_specs=[pl.BlockSpec((TM, D), lambda i, a, b, c: (tile_off + i, 0)),
                  pl.BlockSpec((ne, TM), lambda i, a, b, c: (0, i))]
                 + [pl.BlockSpec(t.shape, lambda i, a, b, c: (0, 0)) for t in consts],
        out_specs=pl.BlockSpec(memory_space=pl.ANY),
        scratch_shapes=[pltpu.VMEM((2, ne * win, D), BF16), pltpu.VMEM((ne, MOE_ALIGN, D), F32),
                        pltpu.SemaphoreType.DMA((2,))],
    )
    return pl.pallas_call(
        functools.partial(_gather_kernel, nt=nt, cap=cap),
        out_shape=jax.ShapeDtypeStruct((ne, cap + win, D), BF16),
        grid_spec=grid_spec,
        compiler_params=_cparams(("arbitrary",)),
        name="moe_gather",
    )(tstart, cnt, rounds, h2b, lpos_t, *consts)


def _ffn_kernel(xe_ref, wg_ref, wu_ref, wd_ref, ye_ref, wg_s, wu_s, wd_s):
    @pl.when(pl.program_id(1) == 0)
    def _():
        wg_s[...] = wg_ref[0, 0].astype(BF16)
        wu_s[...] = wu_ref[0, 0].astype(BF16)
        wd_s[...] = wd_ref[0, 0].astype(BF16)

    x = xe_ref[0]
    hid = (_silu(_dot(x, wg_s[...])) * _dot(x, wu_s[...])).astype(BF16)
    ye_ref[0] = _dot(hid, wd_s[...]).astype(BF16)


def _ffn(xe, wg, wu, wd, cap, layer):
    ne = N_EXPERTS
    blk = (1, FFN_TM, D)
    wspec = pl.BlockSpec((1, 1, D, D), lambda e, j: (layer, e, 0, 0))
    return pl.pallas_call(
        _ffn_kernel,
        out_shape=jax.ShapeDtypeStruct((ne, cap, D), BF16),
        grid=(ne, cap // FFN_TM),
        in_specs=[pl.BlockSpec(blk, lambda e, j: (e, j, 0)), wspec, wspec, wspec],
        out_specs=pl.BlockSpec(blk, lambda e, j: (e, j, 0)),
        scratch_shapes=[pltpu.VMEM((D, D), BF16)] * 3,
        compiler_params=_cparams(("parallel", "arbitrary")),
        name="moe_ffn",
    )(xe, wg, wu, wd)


def _combine_kernel(tstart_sm, rounds_sm, lpos_ref, aff_ref, x_ref, mod_ref, fg_ref, ye_ref, o_ref,
                    ybuf, acc_s, sem, *, nt, cap, final):
    i = pl.program_id(0)
    slot = lax.rem(i, 2)
    ne, w, win, al = N_EXPERTS, MOE_W, MOE_WIN, MOE_ALIGN
    m = ne * win

    def win_start(tile, e, k):
        first = tstart_sm[tile * ne + e] + k * w
        return jnp.minimum((first >> 4) << 4, cap - win)

    def copy(e, start, sl):
        return pltpu.make_async_copy(
            ye_ref.at[e, pl.ds(pl.multiple_of(start, al), win)],
            ybuf.at[sl, pl.ds(e * win, win)],
            sem.at[sl])

    def fetch(tile, k, sl):
        for e in range(ne):
            copy(e, win_start(tile, e, k), sl).start()

    def wait_all(sl):
        for e in range(ne):
            copy(e, 0, sl).wait()

    zero = jnp.int32(0)

    @pl.when(i == 0)
    def _():
        fetch(i, zero, slot)

    wait_all(slot)

    @pl.when(i + 1 < nt)
    def _():
        fetch(i + 1, zero, 1 - slot)

    e_lane, r_lane = _win_index(_iota((1, m), 1))
    expand = jnp.where(_win_index(_iota((ne, m), 1))[0] == _iota((ne, m), 0), 1.0, 0.0).astype(BF16)
    aff = aff_ref[...]
    lpx = _dot(lpos_ref[...].astype(BF16), expand)
    affx = _dot(aff.astype(BF16), expand)
    rl = r_lane.astype(F32)

    def compute(k, sl):
        shift = jnp.zeros((1, m), F32)
        for e in range(ne):
            sh = (tstart_sm[i * ne + e] - win_start(i, e, k)).astype(F32)
            shift = jnp.where(e_lane == e, sh, shift)
        lo = (k * w).astype(F32)
        hit = jnp.where(lpx + shift == rl, 1.0, 0.0) * jnp.where(lpx >= lo, 1.0, 0.0) * jnp.where(lpx < lo + w, 1.0, 0.0)
        return _dot((hit * affx).astype(BF16), ybuf[sl])

    acc_s[...] = compute(zero, slot)

    def extra(k, carry):
        fetch(i, k, slot)
        wait_all(slot)
        acc_s[...] = acc_s[...] + compute(k, slot)
        return carry

    lax.fori_loop(1, rounds_sm[i], extra, 0)
    x = x_ref[...] + mod_ref[0, 5:6, :] * acc_s[...]
    if final:
        x = _rms(x, fg_ref[...])
    o_ref[...] = x


def _combine(x, mod_l, lpos, aff, ye, tstart, rounds, fg, cap, tile_off, final):
    n = lpos.shape[0]
    nt = n // TM
    ne = N_EXPERTS
    grid_spec = pltpu.PrefetchScalarGridSpec(
        num_scalar_prefetch=2,
        grid=(nt,),
        in_specs=[pl.BlockSpec((TM, ne), lambda i, a, b: (i, 0)),
                  pl.BlockSpec((TM, ne), lambda i, a, b: (tile_off + i, 0)),
                  pl.BlockSpec((TM, D), lambda i, a, b: (tile_off + i, 0)),
                  pl.BlockSpec((1, 6, D), lambda i, a, b: (_mod_row(tile_off + i), 0, 0)),
                  pl.BlockSpec((1, D), lambda i, a, b: (0, 0)),
                  pl.BlockSpec(memory_space=pl.ANY)],
        out_specs=pl.BlockSpec((TM, D), lambda i, a, b: (i, 0)),
        scratch_shapes=[pltpu.VMEM((2, ne * MOE_WIN, D), BF16), pltpu.VMEM((TM, D), F32),
                        pltpu.SemaphoreType.DMA((2,))],
    )
    return pl.pallas_call(
        functools.partial(_combine_kernel, nt=nt, cap=cap, final=final),
        out_shape=jax.ShapeDtypeStruct((n, D), F32),
        grid_spec=grid_spec,
        compiler_params=_cparams(("arbitrary",)),
        name="moe_combine",
    )(tstart, rounds, lpos, aff, x, mod_l, fg, ye)


def _moe_group(x, mod_l, h2b, aff, afft_g, wg, wu, wd, fg, n, tile_off, layer):
    final = layer == DEPTH - 1
    cap = (CAP_FACTOR * n) // N_EXPERTS
    lpos_t, lpos, tstart, cnt = _select(afft_g, cap)
    tstart = tstart.reshape(-1)
    rounds = jnp.maximum((jnp.max(cnt.reshape(-1, N_EXPERTS), axis=1) + MOE_W - 1) // MOE_W, 1).astype(jnp.int32)
    xe = _gather(h2b, lpos_t, tstart, cnt.reshape(-1), rounds, cap, tile_off)
    ye = _ffn(xe, wg, wu, wd, cap, layer)
    return _combine(x, mod_l, lpos, aff, ye, tstart, rounds, fg, cap, tile_off, final)


def _layout_w_in(w):
    o = np.cumsum([0, 256, 256, 256, 384, 256, 32, 256, 256, 256, 256, 256, 128, 128, 8, 4096])
    wb = w.astype(BF16)
    zeros = lambda n: jnp.zeros((D, n), BF16)
    w_all = jnp.concatenate([wb[:, o[0]:o[5]], zeros(64), wb[:, o[5]:o[6]], zeros(32), wb[:, o[6]:o[14]],
                             zeros(120), wb[:, o[14]:o[15]]], axis=1)
    return w_all, wb[:, o[13]:o[14]].T


def _layout_mla(w_uq, w_ukv):
    uq = w_uq.reshape(MLA_Q_RANK, N_HEADS, MLA_NOPE + MLA_ROPE)
    uq = jnp.pad(uq, ((0, 0), (0, 0), (0, LANES - MLA_NOPE - MLA_ROPE))).reshape(MLA_Q_RANK, N_HEADS * LANES)
    ukv = w_ukv.reshape(MLA_KV_RANK, N_HEADS, MLA_NOPE + MLA_V)
    uk = jnp.pad(ukv[:, :, :MLA_NOPE], ((0, 0), (0, 0), (0, LANES - MLA_NOPE))).reshape(MLA_KV_RANK, N_HEADS * LANES)
    uv = jnp.pad(ukv[:, :, MLA_NOPE:], ((0, 0), (0, 0), (0, LANES - MLA_V))).reshape(MLA_KV_RANK, N_HEADS * LANES)
    return uq.astype(BF16), uk.astype(BF16), uv.astype(BF16)


def kernel(x_prompt, x_sample, cache_na_k, cache_na_v, cache_mla_ckv, cache_mla_krope, cache_df_k, cache_df_v,
           state_ssm, c, c_ctx, mod_w, mod_b, norm1_g, norm2_g, w_in, na_rel_bias, mla_q_norm_g, mla_kv_norm_g,
           mla_w_uq, mla_w_ukv, df_lambda, df_subln_g, ssm_conv_w, ssm_conv_b, ssm_dt_bias, ssm_a_log, ssm_d,
           ssm_norm_g, w_branch, w_out, router_w, exp_w_gate, exp_w_up, exp_w_down, final_norm_g):
    x_ctx = x_prompt.reshape(R_CTX, D)
    x_lat = x_sample.reshape(R_LAT, D)
    cvec = jnp.concatenate([c_ctx[None, :], c, jnp.zeros((16 - 1 - B_LAT, D), F32)], axis=0)
    mod = _modulation(cvec, mod_w, mod_b).reshape(DEPTH, 16, 6, D)

    cos32, sa32, sb32 = _rope_tables32()
    pad128 = lambda a, fill: np.pad(a, ((0, 0), (64, 32)), constant_values=fill)
    rope128 = (pad128(cos32, 1.0), pad128(sa32, 0.0), pad128(sb32, 0.0))
    rope256 = tuple(np.tile(a, (1, 8)) for a in (cos32, sa32, sb32))
    fg = final_norm_g.reshape(1, D)
    cna_k = cache_na_k.reshape(B_LAT, DEPTH, PAST, 256)
    cna_v = cache_na_v.reshape(B_LAT, DEPTH, PAST, 256)
    cdf_k = cache_df_k.reshape(B_LAT, DEPTH, PAST, 256)
    cdf_v = cache_df_v.reshape(B_LAT, DEPTH, PAST, 256)

    outs = {k: [] for k in ("na_k", "na_v", "ckv", "krope", "df_k", "df_v", "ssm")}
    for l in range(DEPTH):
        mod_l = mod[l]
        (u_na, u_mla, u_df, u_ssm, gates, dt_t, na_k, na_v, df_k, df_v, krope) = _in_proj(
            x_ctx, x_lat, mod_l, norm1_g[l].reshape(1, D), *_layout_w_in(w_in[l]))
        bias = _na_bias_table(na_rel_bias[l])
        br_na = (_attn_ctx(u_na), _na_lat(u_na, cna_k, cna_v, bias, l))
        wuq, wuk, wuv = _layout_mla(mla_w_uq[l], mla_w_ukv[l])
        gq = mla_q_norm_g[l].reshape(1, MLA_Q_RANK)
        gkv = mla_kv_norm_g[l].reshape(1, MLA_KV_RANK)
        mla_c, ckv_new = _mla(u_mla, gq, gkv, wuq, wuk, wuv)
        kr_ctx = jnp.pad(cache_mla_krope[:, l], ((0, 0), (0, 0), (64, 32)))
        (mla_l,) = _mla(u_mla, gq, gkv, wuq, wuk, wuv, rope128, cache_mla_ckv, kr_ctx, l)
        lam_init = 0.8 - 0.6 * math.exp(-0.3 * l)
        gs = df_subln_g[l].reshape(1, DF_V)
        br_df = (_df(u_df, df_lambda[l], gs, lam_init),
                 _df(u_df, df_lambda[l], gs, lam_init, rope256, cdf_k, cdf_v, l))
        d_vec = jnp.repeat(ssm_d[l], SSM_P).reshape(1, 256)
        gn = ssm_norm_g[l].reshape(1, 256)
        ssm_c, st_new = _ssd(u_ssm, dt_t, ssm_conv_w[l], ssm_conv_b[l], ssm_dt_bias[l], ssm_a_log[l], d_vec, gn)
        (ssm_l,) = _ssd(u_ssm, dt_t, ssm_conv_w[l], ssm_conv_b[l], ssm_dt_bias[l], ssm_a_log[l], d_vec, gn,
                        state_ssm, l)
        x_mid, h2b, aff, afft = _merge(x_ctx, x_lat, mod_l, (br_na, (mla_c, mla_l), br_df, (ssm_c, ssm_l)), gates,
                                       w_branch[l].astype(BF16), w_out[l].astype(BF16),
                                       norm2_g[l].reshape(1, D), router_w[l])
        x_ctx = _moe_group(x_mid, mod_l, h2b, aff, afft[:, :R_CTX], exp_w_gate, exp_w_up, exp_w_down,
                           fg, R_CTX, 0, l)
        x_lat = _moe_group(x_mid, mod_l, h2b, aff, afft[:, R_CTX:], exp_w_gate, exp_w_up, exp_w_down,
                           fg, R_LAT, NT_CTX, l)
        outs["na_k"].append(na_k.reshape(B_CTX, L_CTX, N_HEADS, 64))
        outs["na_v"].append(na_v.reshape(B_CTX, L_CTX, N_HEADS, 64))
        outs["ckv"].append(ckv_new.reshape(B_CTX, L_CTX, MLA_KV_RANK))
        outs["krope"].append(krope.reshape(B_CTX, L_CTX, MLA_ROPE))
        outs["df_k"].append(df_k.reshape(B_CTX, L_CTX, N_HEADS, 2, DF_HD))
        outs["df_v"].append(df_v.reshape(B_CTX, L_CTX, N_HEADS, DF_V))
        outs["ssm"].append(st_new)
    stack = lambda k: jnp.stack(outs[k], axis=1)
    return (x_ctx.reshape(B_CTX, L_CTX, D), x_lat.reshape(B_LAT, L_LAT, D), stack("na_k"), stack("na_v"),
            stack("ckv"), stack("krope"), stack("df_k"), stack("df_v"), stack("ssm"))
```

```python
import functools
import math

import numpy as np
import jax
import jax.numpy as jnp
from jax import lax
from jax.experimental import pallas as pl
from jax.experimental.pallas import tpu as pltpu

F32 = jnp.float32
BF16 = jnp.bfloat16

D = 1024
B_CTX, L_CTX = 32, 256
B_LAT, L_LAT = 8, 2048
PAST = 256
DEPTH = 2
GRID_W = 64
EPS = 1e-6
ROPE_BASE = 10000.0
N_HEADS = 4
NA_KH, NA_KW = 8, 16
MLA_NOPE, MLA_ROPE, MLA_V = 64, 32, 64
MLA_Q_RANK, MLA_KV_RANK = 384, 256
DF_HD, DF_V = 32, 64
SSM_P, SSM_N, SSM_GROUPS = 64, 64, 2
N_EXPERTS = 16
CAP_FACTOR = 2

LANES = 128
SUBLANES = 8
TM = 256
R_CTX = B_CTX * L_CTX
R_LAT = B_LAT * L_LAT
R_ALL = R_CTX + R_LAT
NT_CTX = R_CTX // TM
NT_LAT = R_LAT // TM
NT_ALL = R_ALL // TM
TILES_PER_LAT_BATCH = L_LAT // TM
TQ = 256
ATT_TQ = 512
SSD_Q = 256
MOE_W = 64
MOE_ALIGN = 16
MOE_WIN = MOE_W + MOE_ALIGN
COMBINE_BUFS = 3
SELECT_BISECT_STEPS = 40
FFN_TM = 512
NA_QROWS = 4
NA_KROWS = NA_QROWS + NA_KH - 1
NA_NK = NA_KROWS * GRID_W
VMEM_LIMIT = 56 * 1024 * 1024


def _cparams(sem):
    return pltpu.CompilerParams(dimension_semantics=sem, vmem_limit_bytes=VMEM_LIMIT)


def _nt(a, b):
    return lax.dot_general(a, b, (((1,), (1,)), ((), ())), preferred_element_type=F32)


def _dot(a, b):
    return jnp.dot(a, b, preferred_element_type=F32)


def _rms(x, g):
    return x * lax.rsqrt(jnp.mean(x * x, axis=-1, keepdims=True) + EPS) * g


def _silu(x):
    return x * jax.nn.sigmoid(x)


def _softmax_rows(s):
    m = jnp.max(s, axis=-1, keepdims=True)
    e = jnp.exp(s - m)
    return e * (1.0 / jnp.sum(e, axis=-1, keepdims=True))


LOG2E = math.log2(math.e)


def _exp_rows(s, scale):
    m = jnp.max(s, axis=-1, keepdims=True)
    e = jnp.exp2((s - m) * (scale * LOG2E))
    return e, jnp.sum(e, axis=-1, keepdims=True)


def _exp_only(s, scale):
    return jnp.exp2((s - jnp.max(s, axis=-1, keepdims=True)) * (scale * LOG2E))


def _pv_normalised(e, v_ext):
    o = _dot(e.astype(BF16), v_ext)
    return o[:, 0:64] * (1.0 / o[:, 64:65])


def _store_v_ext(v_s, row0, v):
    rows = v.shape[0]
    lane = _iota((rows, LANES), 1)
    ones_col = jnp.where(lane == 64, 1.0, 0.0)
    for pair in range(2):
        blk = v[:, LANES * pair:LANES * pair + LANES]
        even = jnp.where(lane < 64, blk, ones_col)
        odd = jnp.where(lane < 64, pltpu.roll(blk, 64, 1), ones_col)
        v_s[row0:row0 + rows, 2 * LANES * pair:2 * LANES * pair + LANES] = even.astype(BF16)
        v_s[row0:row0 + rows, 2 * LANES * pair + LANES:2 * LANES * pair + 2 * LANES] = odd.astype(BF16)


def _split3(a):
    a1 = a.astype(BF16)
    r1 = a - a1.astype(F32)
    a2 = r1.astype(BF16)
    a3 = (r1 - a2.astype(F32)).astype(BF16)
    return a1, a2, a3


def _iota(shape, dim):
    return lax.broadcasted_iota(jnp.int32, shape, dim)


def _mod_row(i):
    return jnp.where(i < NT_CTX, 0, 1 + (i - NT_CTX) // TILES_PER_LAT_BATCH)


MOD_TN = 1536


def _mod_kernel(c_ref, w_ref, b_ref, o_ref):
    s = _silu(c_ref[...]).astype(BF16)
    o_ref[0] = _dot(s, w_ref[0].astype(BF16)) + b_ref[0]


def _modulation(cvec, mod_w, mod_b):
    n = 6 * D
    return pl.pallas_call(
        _mod_kernel,
        out_shape=jax.ShapeDtypeStruct((DEPTH, 16, n), F32),
        grid=(DEPTH, n // MOD_TN),
        in_specs=[pl.BlockSpec((16, D), lambda l, j: (0, 0)),
                  pl.BlockSpec((1, D, MOD_TN), lambda l, j: (l, 0, j)),
                  pl.BlockSpec((1, 1, MOD_TN), lambda l, j: (l, 0, j))],
        out_specs=pl.BlockSpec((1, 16, MOD_TN), lambda l, j: (l, 0, j)),
        compiler_params=_cparams(("parallel", "parallel")),
        name="modulation",
    )(cvec, mod_w, mod_b.reshape(DEPTH, 1, n))


W_NA, W_MLA, W_DF, W_SSM, W_GATE = 768, 768, 768, 896, 4 * D


W_IN_COLS = np.cumsum([0, W_NA, W_MLA, W_DF, W_SSM, W_GATE])
W_IN_ALL = int(W_IN_COLS[-1])


def _ctx_tile(i):
    return jnp.minimum(i, NT_CTX - 1)


def _lat_tile(i):
    return jnp.maximum(i - NT_CTX, 0)


def _pick_group(i, ctx_ref, lat_ref):
    return jnp.where(i < NT_CTX, ctx_ref[...], lat_ref[...])


def _in_kernel(xc_ref, xl_ref, mod_ref, g_ref, w_ref, wdt_t,
               ona, omla, odf, ossm, ogate, odt_t, onak, onav, odfk, odfv, okr):
    i = pl.program_id(0)
    x = _pick_group(i, xc_ref, xl_ref)
    h = _rms(x, g_ref[...]) * (1.0 + mod_ref[0, 1:2, :]) + mod_ref[0, 0:1, :]
    hb = h.astype(BF16)
    c = W_IN_COLS
    una = _dot(hb, w_ref[:, c[0]:c[1]])
    umla = _dot(hb, w_ref[:, c[1]:c[2]])
    udf = _dot(hb, w_ref[:, c[2]:c[3]])
    ona[...] = una
    omla[...] = umla
    odf[...] = udf
    ossm[...] = _dot(hb, w_ref[:, c[3]:c[4]])
    ogate[...] = jax.nn.sigmoid(_dot(hb, w_ref[:, c[4]:c[5]])).astype(BF16)
    odt_t[...] = _nt(wdt_t[...], hb)

    @pl.when(i < NT_CTX)
    def _():
        onak[...] = una[:, 256:512]
        onav[...] = una[:, 512:768]
        odfk[...] = udf[:, 256:512]
        odfv[...] = udf[:, 512:768]
        okr[...] = umla[:, 704:736]


def _in_proj(x_ctx, x_lat, mod_l, norm_g, w_all, w_dt_t):
    widths = (W_NA, W_MLA, W_DF, W_SSM)
    const = lambda i: (0, 0)
    row = lambda i: (i, 0)
    ctx_row = lambda i: (_ctx_tile(i), 0)
    out_shape = [jax.ShapeDtypeStruct((R_ALL, w), F32) for w in widths]
    out_shape += [jax.ShapeDtypeStruct((R_ALL, W_GATE), BF16), jax.ShapeDtypeStruct((8, R_ALL), F32)]
    out_shape += [jax.ShapeDtypeStruct((R_CTX, 256), F32)] * 4 + [jax.ShapeDtypeStruct((R_CTX, MLA_ROPE), F32)]
    out_specs = [pl.BlockSpec((TM, w), row) for w in widths]
    out_specs += [pl.BlockSpec((TM, W_GATE), row), pl.BlockSpec((8, TM), lambda i: (0, i))]
    out_specs += [pl.BlockSpec((TM, 256), ctx_row)] * 4 + [pl.BlockSpec((TM, MLA_ROPE), ctx_row)]
    return pl.pallas_call(
        _in_kernel,
        out_shape=out_shape,
        grid=(NT_ALL,),
        in_specs=[pl.BlockSpec((TM, D), ctx_row),
                  pl.BlockSpec((TM, D), lambda i: (_lat_tile(i), 0)),
                  pl.BlockSpec((1, 6, D), lambda i: (_mod_row(i), 0, 0)),
                  pl.BlockSpec((1, D), const),
                  pl.BlockSpec((D, W_IN_ALL), const, pipeline_mode=pl.Buffered(1)),
                  pl.BlockSpec((8, D), const, pipeline_mode=pl.Buffered(1))],
        out_specs=out_specs,
        compiler_params=_cparams(("arbitrary",)),
        name="in_proj",
    )(x_ctx, x_lat, mod_l, norm_g, w_all, w_dt_t)


def _rope(x, cos, sin_a, sin_b):
    n = x.shape[-1]
    nxt = pltpu.roll(x, n - 1, 1)
    prv = pltpu.roll(x, 1, 1)
    return x * cos + nxt * sin_a + prv * sin_b


def _rope_tables32():
    t = np.arange(L_LAT)
    quarter = 8
    inv = ROPE_BASE ** (-np.arange(quarter, dtype=np.float64) / quarter)
    rows = (t // GRID_W).astype(np.float64)[:, None]
    cols = (t % GRID_W).astype(np.float64)[:, None]
    ang = np.concatenate([rows * inv, cols * inv], axis=-1)
    cos = np.repeat(np.cos(ang), 2, axis=-1)
    sin = np.repeat(np.sin(ang), 2, axis=-1)
    even = (np.arange(32) % 2 == 0)[None, :]
    sin_a = np.where(even, -sin, 0.0)
    sin_b = np.where(even, 0.0, sin)
    return tuple(np.asarray(a, np.float32) for a in (cos, sin_a, sin_b))


def _attn_ctx_kernel(u_ref, o_ref, v_s):
    scale = 64 ** -0.5
    _store_v_ext(v_s, 0, u_ref[:, 512:768])
    s = jnp.concatenate([_nt(u_ref[:, 64 * h:64 * h + 64].astype(BF16),
                             u_ref[:, 256 + 64 * h:256 + 64 * h + 64].astype(BF16)) for h in range(N_HEADS)], axis=0)
    e = _exp_only(s, scale)
    for h in range(N_HEADS):
        o_ref[:, 64 * h:64 * h + 64] = _pv_normalised(e[L_CTX * h:L_CTX * h + L_CTX],
                                                      v_s[:, 128 * h:128 * h + 128]).astype(BF16)


def _attn_ctx(u_na):
    return pl.pallas_call(
        _attn_ctx_kernel,
        out_shape=jax.ShapeDtypeStruct((R_CTX, 256), BF16),
        grid=(B_CTX,),
        in_specs=[pl.BlockSpec((L_CTX, W_NA), lambda b: (b, 0))],
        out_specs=pl.BlockSpec((L_CTX, 256), lambda b: (b, 0)),
        scratch_shapes=[pltpu.VMEM((L_CTX, 4 * LANES), BF16)],
        compiler_params=_cparams(("parallel",)),
        name="na_ctx",
    )(u_na)


NA_ROWS = L_LAT // GRID_W
NA_NQT = NA_ROWS // NA_QROWS


def _na_pattern(qt):
    return jnp.where(qt == 0, 0, jnp.where(qt == NA_NQT - 1, 2, 1))


def _na_pattern_offsets(p, a):
    q_off = jnp.where(p == 0, 0, jnp.where(p == 1, NA_KH // 2, NA_KROWS - NA_QROWS))
    rs_rel = jnp.where(p == 0, 0, jnp.where(p == 1, a, NA_KROWS - NA_KH))
    return q_off, rs_rel


def _check_na_patterns():
    for qt in range(NA_NQT):
        ks = int(np.clip(NA_QROWS * qt - NA_KH // 2, 0, NA_ROWS - NA_KROWS))
        p = 0 if qt == 0 else (2 if qt == NA_NQT - 1 else 1)
        for a in range(NA_QROWS):
            r = NA_QROWS * qt + a
            rs = int(np.clip(r - NA_KH // 2, 0, NA_ROWS - NA_KH))
            q_off = (0, NA_KH // 2, NA_KROWS - NA_QROWS)[p]
            rs_rel = (0, a, NA_KROWS - NA_KH)[p]
            assert NA_QROWS * qt - ks == q_off and rs - ks == rs_rel and rs + NA_KH <= ks + NA_KROWS


_check_na_patterns()


def _na_bias_kernel(rb_ref, o_ref):
    p = pl.program_id(0)
    qc = _iota((GRID_W, GRID_W), 0)
    kc = _iota((GRID_W, GRID_W), 1)
    cs = jnp.clip(qc - NA_KW // 2, 0, GRID_W - NA_KW)
    col_ok = (kc >= cs) & (kc < cs + NA_KW)
    for a in range(NA_QROWS):
        q_off, rs_rel = _na_pattern_offsets(p, a)
        for b in range(NA_KROWS):
            row_ok = (b >= rs_rel) & (b < rs_rel + NA_KH)
            dr = jnp.clip(b - q_off - a + NA_KH - 1, 0, 2 * NA_KH - 2)
            v = jnp.broadcast_to(rb_ref[0, pl.ds(dr, 1), :], (GRID_W, LANES))
            t = pltpu.roll(v, LANES - (NA_KW - 1), 1, stride=1, stride_axis=0)[:, :GRID_W]
            o_ref[0, 0, GRID_W * a:GRID_W * a + GRID_W, GRID_W * b:GRID_W * b + GRID_W] = jnp.where(
                col_ok & row_ok, t * LOG2E, -1e30)


def _na_bias_table(rel_bias):
    rb = jnp.pad(rel_bias.astype(F32), ((0, 0), (0, 0), (0, LANES - (2 * NA_KW - 1))))
    return pl.pallas_call(
        _na_bias_kernel,
        out_shape=jax.ShapeDtypeStruct((3, N_HEADS, TQ, NA_NK), F32),
        grid=(3, N_HEADS),
        in_specs=[pl.BlockSpec((1, 2 * NA_KH - 1, LANES), lambda p, h: (h, 0, 0))],
        out_specs=pl.BlockSpec((1, 1, TQ, NA_NK), lambda p, h: (p, h, 0, 0)),
        compiler_params=_cparams(("parallel", "parallel")),
        name="na_bias",
    )(rb)


def _na_lat_kernel(u_ref, kc_ref, vc_ref, bias_ref, o_ref, v_s, vc_s):
    qt = pl.program_id(1)
    c = 64 ** -0.5 * LOG2E
    ks = jnp.clip(NA_QROWS * qt - NA_KH // 2, 0, NA_ROWS - NA_KROWS)
    kstart = pl.multiple_of(ks * GRID_W, GRID_W)
    qstart = pl.multiple_of(qt * TQ, TQ)
    _store_v_ext(v_s, 0, u_ref[pl.ds(kstart, NA_NK), 512:768])
    _store_v_ext(vc_s, 0, vc_ref[0, 0])
    for h in range(N_HEADS):
        q = u_ref[pl.ds(qstart, TQ), 64 * h:64 * h + 64].astype(BF16)
        k = u_ref[pl.ds(kstart, NA_NK), 256 + 64 * h:256 + 64 * h + 64].astype(BF16)
        kc = kc_ref[0, 0, :, 64 * h:64 * h + 64].astype(BF16)
        t_loc = _nt(q, k) * c + bias_ref[0, h]
        t_ctx = _nt(q, kc) * c
        m = jnp.maximum(jnp.max(t_loc, axis=-1, keepdims=True), jnp.max(t_ctx, axis=-1, keepdims=True))
        e_loc = jnp.exp2(t_loc - m).astype(BF16)
        e_ctx = jnp.exp2(t_ctx - m).astype(BF16)
        o = _dot(e_loc, v_s[:, 128 * h:128 * h + 128]) + _dot(e_ctx, vc_s[:, 128 * h:128 * h + 128])
        o_ref[:, 64 * h:64 * h + 64] = (o[:, 0:64] * (1.0 / o[:, 64:65])).astype(BF16)


def _na_lat(u_na, k_ctx, v_ctx, bias, layer):
    nqt = L_LAT // TQ
    cache = pl.BlockSpec((1, 1, PAST, 256), lambda b, t: (b, layer, 0, 0))
    return pl.pallas_call(
        _na_lat_kernel,
        out_shape=jax.ShapeDtypeStruct((R_LAT, 256), BF16),
        grid=(B_LAT, nqt),
        in_specs=[pl.BlockSpec((L_LAT, W_NA), lambda b, t: (R_CTX // L_LAT + b, 0)),
                  cache, cache,
                  pl.BlockSpec((1, N_HEADS, TQ, NA_NK), lambda b, t: (_na_pattern(t), 0, 0, 0))],
        out_specs=pl.BlockSpec((TQ, 256), lambda b, t: (b * nqt + t, 0)),
        scratch_shapes=[pltpu.VMEM((NA_NK, 4 * LANES), BF16), pltpu.VMEM((PAST, 4 * LANES), BF16)],
        compiler_params=_cparams(("parallel", "arbitrary")),
        name="na_lat",
    )(u_na, k_ctx, v_ctx, bias)


def _mla_kernel(*refs, latent, seq):
    if latent:
        (u_ref, gq_ref, gkv_ref, wuq_ref, wuk_ref, wuv_ref, cos_ref, sa_ref, sb_ref,
         ckv_c_ref, kr_c_ref, o_ref, k_s, v_s) = refs
    else:
        (u_ref, gq_ref, gkv_ref, wuq_ref, wuk_ref, wuv_ref, o_ref, ckv_o_ref, k_s, v_s) = refs
    scale = (MLA_NOPE + MLA_ROPE) ** -0.5
    ckv = _rms(u_ref[:, 384:640], gkv_ref[...])
    kr = u_ref[:, 640:768]
    if latent:
        kr = _rope(kr, cos_ref[...], sa_ref[...], sb_ref[...])
    else:
        ckv_o_ref[...] = ckv
    ckv_b = ckv.astype(BF16)
    for h in range(N_HEADS):
        k_s[0:seq, 128 * h:128 * h + 128] = (_dot(ckv_b, wuk_ref[:, 128 * h:128 * h + 128]) + kr).astype(BF16)
    ones_col = jnp.where((_iota((1, 4 * LANES), 1) & (LANES - 1)) == MLA_V, 1.0, 0.0)
    v_s[0:seq, :] = (_dot(ckv_b, wuv_ref[...]) + ones_col).astype(BF16)
    if latent:
        cc = ckv_c_ref[0, 0].astype(BF16)
        krc = kr_c_ref[0]
        for h in range(N_HEADS):
            k_s[seq:seq + PAST, 128 * h:128 * h + 128] = (
                _dot(cc, wuk_ref[:, 128 * h:128 * h + 128]) + krc).astype(BF16)
        v_s[seq:seq + PAST, :] = (_dot(cc, wuv_ref[...]) + ones_col).astype(BF16)

    tq = min(ATT_TQ, seq)

    def q_tile(t, carry):
        r0 = pl.multiple_of(t * tq, tq)
        cq = _rms(u_ref[pl.ds(r0, tq), 0:384], gq_ref[...]).astype(BF16)
        for h in range(N_HEADS):
            q = _dot(cq, wuq_ref[:, 128 * h:128 * h + 128])
            q = _rope(q, cos_ref[pl.ds(r0, tq), :], sa_ref[pl.ds(r0, tq), :], sb_ref[pl.ds(r0, tq), :])
            e = _exp_only(_nt(q.astype(BF16), k_s[:, 128 * h:128 * h + 128]), scale)
            o_ref[pl.ds(r0, tq), 64 * h:64 * h + 64] = _pv_normalised(e, v_s[:, 128 * h:128 * h + 128]).astype(BF16)
        return carry

    if latent:
        lax.fori_loop(0, seq // tq, q_tile, 0)
    else:
        cq = _rms(u_ref[:, 0:384], gq_ref[...]).astype(BF16)
        s = jnp.concatenate([_nt(_dot(cq, wuq_ref[:, 128 * h:128 * h + 128]).astype(BF16),
                                 k_s[:, 128 * h:128 * h + 128]) for h in range(N_HEADS)], axis=0)
        e = _exp_only(s, scale)
        for h in range(N_HEADS):
            o_ref[:, 64 * h:64 * h + 64] = _pv_normalised(e[seq * h:seq * h + seq],
                                                          v_s[:, 128 * h:128 * h + 128]).astype(BF16)


def _mla(u_mla, gq, gkv, wuq, wuk, wuv, rope128=None, ckv_ctx=None, kr_ctx=None, layer=0):
    latent = rope128 is not None
    seq = L_LAT if latent else L_CTX
    nb = B_LAT if latent else B_CTX
    off = R_CTX // L_LAT if latent else 0
    lk = seq + PAST if latent else seq
    const = lambda b: (0, 0)
    in_specs = [pl.BlockSpec((seq, W_MLA), lambda b: (off + b, 0)),
                pl.BlockSpec((1, MLA_Q_RANK), const),
                pl.BlockSpec((1, MLA_KV_RANK), const),
                pl.BlockSpec((MLA_Q_RANK, 512), const),
                pl.BlockSpec((MLA_KV_RANK, 512), const),
                pl.BlockSpec((MLA_KV_RANK, 512), const)]
    args = [u_mla, gq, gkv, wuq, wuk, wuv]
    out_shape = [jax.ShapeDtypeStruct((nb * seq, 256), BF16)]
    out_specs = [pl.BlockSpec((seq, 256), lambda b: (b, 0))]
    if latent:
        in_specs += [pl.BlockSpec((seq, LANES), const)] * 3
        in_specs += [pl.BlockSpec((1, 1, PAST, MLA_KV_RANK), lambda b: (b, layer, 0, 0)),
                     pl.BlockSpec((1, PAST, LANES), lambda b: (b, 0, 0))]
        args += list(rope128) + [ckv_ctx, kr_ctx]
    else:
        out_shape.append(jax.ShapeDtypeStruct((nb * seq, MLA_KV_RANK), F32))
        out_specs.append(pl.BlockSpec((seq, MLA_KV_RANK), lambda b: (b, 0)))
    return pl.pallas_call(
        functools.partial(_mla_kernel, latent=latent, seq=seq),
        out_shape=out_shape,
        grid=(nb,),
        in_specs=in_specs,
        out_specs=out_specs,
        scratch_shapes=[pltpu.VMEM((lk, 512), BF16), pltpu.VMEM((lk, 512), BF16)],
        compiler_params=_cparams(("parallel",)),
        name="mla_lat" if latent else "mla_ctx",
    )(*args)


def _df_kernel(*refs, latent, seq, lam_init):
    if latent:
        (u_ref, lv_ref, gs_ref, cos_ref, sa_ref, sb_ref, kc_ref, vc_ref, o_ref, k_s, v_s) = refs
    else:
        (u_ref, lv_ref, gs_ref, o_ref, k_s, v_s) = refs
    scale = DF_HD ** -0.5
    lv = lv_ref[...]
    lam = (jnp.exp(jnp.sum(lv[0:1] * lv[1:2], axis=1, keepdims=True))
           - jnp.exp(jnp.sum(lv[2:3] * lv[3:4], axis=1, keepdims=True)) + lam_init)
    k = u_ref[:, 256:512]
    if latent:
        k = _rope(k, cos_ref[...], sa_ref[...], sb_ref[...])
        k_s[seq:seq + PAST, :] = kc_ref[0, 0].astype(BF16)
        _store_v_ext(v_s, seq, vc_ref[0, 0])
    k_s[0:seq, :] = k.astype(BF16)
    _store_v_ext(v_s, 0, u_ref[:, 512:768])
    tq = min(ATT_TQ, seq)
    first = _iota((tq, 64), 1) < DF_HD

    def q_tile(t, carry):
        r0 = pl.multiple_of(t * tq, tq)
        q = u_ref[pl.ds(r0, tq), 0:256]
        if latent:
            q = _rope(q, cos_ref[pl.ds(r0, tq), :], sa_ref[pl.ds(r0, tq), :], sb_ref[pl.ds(r0, tq), :])
        for h in range(N_HEADS):
            qh = q[:, 64 * h:64 * h + 64]
            kh = k_s[:, 64 * h:64 * h + 64]
            q0 = jnp.where(first, qh, 0.0).astype(BF16)
            q1 = jnp.where(first, 0.0, qh).astype(BF16)
            vh = v_s[:, 128 * h:128 * h + 128]
            o = (_pv_normalised(_exp_only(_nt(q0, kh), scale), vh)
                 - lam * _pv_normalised(_exp_only(_nt(q1, kh), scale), vh))
            o_ref[pl.ds(r0, tq), 64 * h:64 * h + 64] = (_rms(o, gs_ref[...]) * (1.0 - lam_init)).astype(BF16)
        return carry

    if latent:
        lax.fori_loop(0, seq // tq, q_tile, 0)
    else:
        q = u_ref[:, 0:256]
        blocks = []
        for h in range(N_HEADS):
            qh = q[:, 64 * h:64 * h + 64]
            kh = k_s[:, 64 * h:64 * h + 64]
            blocks.append(_nt(jnp.where(first, qh, 0.0).astype(BF16), kh))
            blocks.append(_nt(jnp.where(first, 0.0, qh).astype(BF16), kh))
        e = _exp_only(jnp.concatenate(blocks, axis=0), scale)
        for h in range(N_HEADS):
            vh = v_s[:, 128 * h:128 * h + 128]
            o = (_pv_normalised(e[2 * h * seq:(2 * h + 1) * seq], vh)
                 - lam * _pv_normalised(e[(2 * h + 1) * seq:(2 * h + 2) * seq], vh))
            o_ref[:, 64 * h:64 * h + 64] = (_rms(o, gs_ref[...]) * (1.0 - lam_init)).astype(BF16)


def _df(u_df, lam_vec, g_sub, lam_init, rope256=None, k_ctx=None, v_ctx=None, layer=0):
    latent = rope256 is not None
    seq = L_LAT if latent else L_CTX
    nb = B_LAT if latent else B_CTX
    off = R_CTX // L_LAT if latent else 0
    lk = seq + PAST if latent else seq
    const = lambda b: (0, 0)
    in_specs = [pl.BlockSpec((seq, W_DF), lambda b: (off + b, 0)),
                pl.BlockSpec((4, DF_HD), const),
                pl.BlockSpec((1, DF_V), const)]
    args = [u_df, lam_vec, g_sub]
    if latent:
        in_specs += [pl.BlockSpec((seq, 256), const)] * 3
        in_specs += [pl.BlockSpec((1, 1, PAST, 256), lambda b: (b, layer, 0, 0))] * 2
        args += list(rope256) + [k_ctx, v_ctx]
    return pl.pallas_call(
        functools.partial(_df_kernel, latent=latent, seq=seq, lam_init=lam_init),
        out_shape=jax.ShapeDtypeStruct((nb * seq, 256), BF16),
        grid=(nb,),
        in_specs=in_specs,
        out_specs=pl.BlockSpec((seq, 256), lambda b: (b, 0)),
        scratch_shapes=[pltpu.VMEM((lk, 256), BF16), pltpu.VMEM((lk, 512), BF16)],
        compiler_params=_cparams(("parallel",)),
        name="df_lat" if latent else "df_ctx",
    )(*args)


def _softplus(x):
    return jnp.maximum(x, 0.0) + jnp.log1p(jnp.exp(-jnp.abs(x)))


def _ssd_kernel(*refs, latent, seq):
    if latent:
        (u_ref, dtt_ref, cw_ref, cb_ref, dtb_c_ref, dtb_r_ref, a_c_ref, a_r_ref, dvec_ref, gn_ref, h0_ref,
         o_ref, xs_s, bm_s, cm_s, dtc_s, dtr_s, y_s, st_s) = refs
    else:
        (u_ref, dtt_ref, cw_ref, cb_ref, dtb_c_ref, dtb_r_ref, a_c_ref, a_r_ref, dvec_ref, gn_ref,
         o_ref, st_o_ref, xs_s, bm_s, cm_s, dtc_s, dtr_s, y_s, st_s) = refs
    q = SSD_Q
    nchunk = seq // q

    def conv(a, w, b):
        row = _iota(a.shape, 0)
        prv = jnp.where(row == 0, 0.0, pltpu.roll(a, 1, 0))
        nxt = jnp.where(row == seq - 1, 0.0, pltpu.roll(a, seq - 1, 0))
        return _silu(w[0:1] * prv + w[1:2] * a + w[2:3] * nxt + b)

    cw = cw_ref[...]
    cb = cb_ref[...]
    xs_s[...] = conv(u_ref[:, 0:256], cw[:, 0:256], cb[:, 0:256])
    bm_s[...] = conv(u_ref[:, 512:640], cw[:, 256:384], cb[:, 256:384])
    cm_s[...] = conv(u_ref[:, 640:768], cw[:, 384:512], cb[:, 384:512])
    dtc_s[...] = _softplus(u_ref[:, 768:776] + dtb_c_ref[...])
    dtr_s[...] = _softplus(dtt_ref[...] + dtb_r_ref[...])
    eye_n = jnp.where(_iota((SSM_N, SSM_N), 0) == _iota((SSM_N, SSM_N), 1), 1.0, 0.0).astype(BF16)

    def transpose64(a):
        return sum(_nt(eye_n, p) for p in _split3(a))

    hpg = N_HEADS // SSM_GROUPS
    st_s[...] = jnp.zeros(st_s.shape, F32)
    if latent:
        for d in range(2):
            for h in range(N_HEADS):
                g = h // hpg
                st_s[d, 64 * g:64 * g + 64, 64 * h:64 * h + 64] = transpose64(h0_ref[0, 0, d, h])

    ri = _iota((q, q), 0)
    ci = _iota((q, q), 1)
    lower = ri >= ci
    upper = ri <= ci
    a_col = -jnp.exp(a_c_ref[...])
    a_row = -jnp.exp(a_r_ref[...])
    head_of_lane = _iota((1, 256), 1) >> 6
    own_group = (_iota((2 * SSM_N, 256), 0) >> 6) == (_iota((2 * SSM_N, 256), 1) >> 7)

    def chunk(c, d):
        c0 = pl.multiple_of(c * q, q)
        xs = xs_s[pl.ds(c0, q), :]
        bm_b = bm_s[pl.ds(c0, q), :].astype(BF16)
        cm_b = cm_s[pl.ds(c0, q), :].astype(BF16)
        dtc = dtc_s[pl.ds(c0, q), :]
        mask = lower if d == 0 else upper
        tri = jnp.where(mask, 1.0, 0.0).astype(BF16)
        tri_t = jnp.where(upper if d == 0 else lower, 1.0, 0.0).astype(BF16)
        acum_c = sum(_dot(tri, p) for p in _split3(dtc * a_col))
        acum_r = sum(_dot(p, tri_t) for p in _split3(dtr_s[:, pl.ds(c0, q)] * a_row))
        spread = jnp.where(_iota((8, 256), 0) == N_HEADS * d + (_iota((8, 256), 1) >> 6), 1.0, 0.0).astype(BF16)
        a_exp = sum(_dot(p, spread) for p in _split3(acum_c))
        dt_exp = sum(_dot(p, spread) for p in _split3(dtc))
        a_end = a_exp[q - 1:q, :] if d == 0 else a_exp[0:1, :]
        xdt = xs * dt_exp
        xdt_b = xdt.astype(BF16)
        st = st_s[d]
        y = _dot(cm_b, st.astype(BF16)) * jnp.exp(a_exp)
        for g in range(SSM_GROUPS):
            cb_g = _nt(cm_b[:, 64 * g:64 * g + 64], bm_b[:, 64 * g:64 * g + 64])
            for hh in range(hpg):
                h = g * hpg + hh
                j = N_HEADS * d + h
                seg = acum_c[:, j:j + 1] - acum_r[j:j + 1, :]
                decay = jnp.where(mask, jnp.exp(jnp.where(mask, seg, 0.0)), 0.0)
                x_h = jnp.where(head_of_lane == h, xdt_b, jnp.zeros_like(xdt_b))
                y = y + _dot((cb_g * decay).astype(BF16), x_h)
        if d == 0:
            y_s[pl.ds(c0, q), :] = y
        else:
            y_s[pl.ds(c0, q), :] = y_s[pl.ds(c0, q), :] + y
        xw = (xdt * jnp.exp(a_end - a_exp)).astype(BF16)
        upd = _dot(bm_s[pl.ds(c0, q), :].T.astype(BF16), xw)
        st_s[d] = st * jnp.exp(a_end) + jnp.where(own_group, upd, 0.0)

    def fwd(c, carry):
        chunk(c, 0)
        return carry

    def bwd(c, carry):
        chunk(nchunk - 1 - c, 1)
        return carry

    lax.fori_loop(0, nchunk, fwd, 0)
    lax.fori_loop(0, nchunk, bwd, 0)
    y = y_s[...] + dvec_ref[...] * xs_s[...]
    o_ref[...] = _rms(y * _silu(u_ref[:, 256:512]), gn_ref[...]).astype(BF16)
    if not latent:
        for d in range(2):
            for h in range(N_HEADS):
                g = h // hpg
                st_o_ref[0, d, h] = transpose64(st_s[d, 64 * g:64 * g + 64, 64 * h:64 * h + 64])


def _ssd(u_ssm, dt_t, conv_w, conv_b, dt_bias, a_log, d_vec, g_norm, h0=None, layer=0):
    latent = h0 is not None
    seq = L_LAT if latent else L_CTX
    nb = B_LAT if latent else B_CTX
    off = R_CTX // L_LAT if latent else 0
    const = lambda b: (0, 0)
    dtb = dt_bias.reshape(1, 8)
    alg = a_log.reshape(1, 8)
    in_specs = [pl.BlockSpec((seq, W_SSM), lambda b: (off + b, 0)),
                pl.BlockSpec((8, seq), lambda b: (0, off + b)),
                pl.BlockSpec((3, 512), const), pl.BlockSpec((1, 512), const),
                pl.BlockSpec((1, 8), const), pl.BlockSpec((8, 1), const),
                pl.BlockSpec((1, 8), const), pl.BlockSpec((8, 1), const),
                pl.BlockSpec((1, 256), const), pl.BlockSpec((1, 256), const)]
    args = [u_ssm, dt_t, conv_w, conv_b.reshape(1, 512), dtb, dtb.reshape(8, 1), alg, alg.reshape(8, 1),
            d_vec, g_norm]
    out_shape = [jax.ShapeDtypeStruct((nb * seq, 256), BF16)]
    out_specs = [pl.BlockSpec((seq, 256), lambda b: (b, 0))]
    if latent:
        in_specs.append(pl.BlockSpec((1, 1, 2, N_HEADS, SSM_P, SSM_N), lambda b: (b, layer, 0, 0, 0, 0)))
        args.append(h0)
    else:
        out_shape.append(jax.ShapeDtypeStruct((nb, 2, N_HEADS, SSM_P, SSM_N), F32))
        out_specs.append(pl.BlockSpec((1, 2, N_HEADS, SSM_P, SSM_N), lambda b: (b, 0, 0, 0, 0)))
    scratch = [pltpu.VMEM((seq, 256), F32), pltpu.VMEM((seq, 128), F32), pltpu.VMEM((seq, 128), F32),
               pltpu.VMEM((seq, 8), F32), pltpu.VMEM((8, seq), F32), pltpu.VMEM((seq, 256), F32),
               pltpu.VMEM((2, SSM_GROUPS * SSM_N, 256), F32)]
    return pl.pallas_call(
        functools.partial(_ssd_kernel, latent=latent, seq=seq),
        out_shape=out_shape,
        grid=(nb,),
        in_specs=in_specs,
        out_specs=out_specs,
        scratch_shapes=scratch,
        compiler_params=_cparams(("parallel",)),
        name="ssd_lat" if latent else "ssd_ctx",
    )(*args)


def _merge_kernel(xc_ref, xl_ref, mod_ref, c0, l0, c1, l1, c2, l2, c3, l3, gate_ref, wb_ref, wo_ref, g2_ref,
                  wr_ref, xo_ref, h2_ref, aff_ref, afft_ref):
    i = pl.program_id(0)
    acc = None
    for b, (bc, bl) in enumerate(((c0, l0), (c1, l1), (c2, l2), (c3, l3))):
        proj = _dot(_pick_group(i, bc, bl).astype(BF16), wb_ref[b])
        term = gate_ref[:, D * b:D * b + D].astype(F32) * proj
        acc = term if acc is None else acc + term
    x = _pick_group(i, xc_ref, xl_ref) + mod_ref[0, 2:3, :] * _dot(acc.astype(BF16), wo_ref[...])
    xo_ref[...] = x
    h2 = _rms(x, g2_ref[...]) * (1.0 + mod_ref[0, 4:5, :]) + mod_ref[0, 3:4, :]
    hb = h2.astype(BF16)
    h2_ref[...] = hb
    hl = (h2 - hb.astype(F32)).astype(BF16)
    wr = wr_ref[...]
    wh = wr.astype(BF16)
    wl = (wr - wh.astype(F32)).astype(BF16)
    logits = _dot(hb, wh) + _dot(hl, wh) + _dot(hb, wl)
    aff = _softmax_rows(logits)
    aff_ref[...] = aff
    eye = jnp.where(_iota((N_EXPERTS, N_EXPERTS), 0) == _iota((N_EXPERTS, N_EXPERTS), 1), 1.0, 0.0).astype(BF16)
    afft_ref[...] = sum(_nt(eye, p) for p in _split3(aff))


def _merge(x_ctx, x_lat, mod_l, branches, gates, wb, wo, g2, wr):
    const = lambda i: (0, 0)
    row = lambda i: (i, 0)
    ctx_row = lambda i: (_ctx_tile(i), 0)
    lat_row = lambda i: (_lat_tile(i), 0)
    return pl.pallas_call(
        _merge_kernel,
        out_shape=[jax.ShapeDtypeStruct((R_ALL, D), F32), jax.ShapeDtypeStruct((R_ALL, D), BF16),
                   jax.ShapeDtypeStruct((R_ALL, N_EXPERTS), F32), jax.ShapeDtypeStruct((N_EXPERTS, R_ALL), F32)],
        grid=(NT_ALL,),
        in_specs=[pl.BlockSpec((TM, D), ctx_row), pl.BlockSpec((TM, D), lat_row),
                  pl.BlockSpec((1, 6, D), lambda i: (_mod_row(i), 0, 0))]
                 + [pl.BlockSpec((TM, 256), ctx_row), pl.BlockSpec((TM, 256), lat_row)] * 4
                 + [pl.BlockSpec((TM, W_GATE), row),
                    pl.BlockSpec((4, 256, D), lambda i: (0, 0, 0), pipeline_mode=pl.Buffered(1)),
                    pl.BlockSpec((D, D), const, pipeline_mode=pl.Buffered(1)),
                    pl.BlockSpec((1, D), const),
                    pl.BlockSpec((D, N_EXPERTS), const)],
        out_specs=[pl.BlockSpec((TM, D), row), pl.BlockSpec((TM, D), row),
                   pl.BlockSpec((TM, N_EXPERTS), row), pl.BlockSpec((N_EXPERTS, TM), lambda i: (0, i))],
        compiler_params=_cparams(("parallel",)),
        name="merge_router",
    )(x_ctx, x_lat, mod_l, *[a for pair in branches for a in pair], gates, wb, wo, g2, wr)


def _select_kernel(afft_ref, lpos_t_ref, lpos_ref, tstart_ref, cnt_ref, gt_s, eq_s, need_s, carry_s, *, cap):
    t = pl.program_id(0)
    ne = N_EXPERTS

    @pl.when(t == 0)
    def _():
        aff = afft_ref[...]

        def count_ge(v):
            return jnp.sum(jnp.where(aff >= v, 1.0, 0.0), axis=1, keepdims=True)

        def bisect(_, lh):
            lo, hi = lh
            mid = jnp.where(lo > 0.0, jnp.sqrt(lo) * jnp.sqrt(hi), hi * 2.0 ** -32)
            mid = jnp.clip(mid, lo, hi)
            ok = count_ge(mid) >= cap
            return jnp.where(ok, mid, lo), jnp.where(ok, hi, mid)

        _, hi = lax.fori_loop(0, SELECT_BISECT_STEPS, bisect,
                              (jnp.zeros((ne, 1), F32), jnp.full((ne, 1), 2.0, F32)))

        def short(st):
            return jnp.min(st[1]) < cap

        def peel(st):
            bound, cnt = st
            nxt = jnp.max(jnp.where(aff < bound, aff, -1.0), axis=1, keepdims=True)
            upd = cnt < cap
            return jnp.where(upd, nxt, bound), jnp.where(upd, count_ge(nxt), cnt)

        thr, _ = lax.while_loop(short, peel, (hi, count_ge(hi)))
        gt = jnp.where(aff > thr, 1.0, 0.0)
        gt_s[...] = gt
        eq_s[...] = jnp.where(aff == thr, 1.0, 0.0)
        need_col = cap - jnp.sum(gt, axis=1, keepdims=True)
        eye = _iota((ne, ne), 0) == _iota((ne, ne), 1)
        need_s[...] = jnp.sum(jnp.where(eye, need_col, 0.0), axis=0, keepdims=True)
        carry_s[...] = jnp.zeros(carry_s.shape, F32)

    sl = pl.ds(pl.multiple_of(t * TM, TM), TM)
    eye_t = jnp.where(_iota((TM, TM), 0) == _iota((TM, TM), 1), 1.0, 0.0).astype(BF16)
    eye_e = jnp.where(_iota((ne, ne), 0) == _iota((ne, ne), 1), 1.0, 0.0).astype(BF16)
    before = jnp.where(_iota((TM, TM), 0) > _iota((TM, TM), 1), 1.0, 0.0).astype(BF16)
    gtm = _nt(eye_t, gt_s[:, sl].astype(BF16))
    eqm = _nt(eye_t, eq_s[:, sl].astype(BF16))
    eq_seen = carry_s[0:1, :]
    pos0 = carry_s[1:2, :]
    eq_rank = _dot(before, eqm.astype(BF16)) + eq_seen
    sel = jnp.maximum(gtm, eqm * jnp.where(eq_rank < need_s[...], 1.0, 0.0))
    lp = _dot(before, sel.astype(BF16))
    cnt = jnp.sum(sel, axis=0, keepdims=True)
    lpos_ref[...] = jnp.where(sel > 0.0, lp, -1.0)
    lp_t = _nt(eye_e, lp.astype(BF16))
    sel_t = _nt(eye_e, sel.astype(BF16))
    lpos_t_ref[...] = jnp.where(sel_t > 0.0, lp_t, -1.0)
    tstart_ref[0] = pos0.astype(jnp.int32)
    cnt_ref[0] = cnt.astype(jnp.int32)
    carry_s[0:1, :] = eq_seen + jnp.sum(eqm, axis=0, keepdims=True)
    carry_s[1:2, :] = pos0 + cnt


def _select(afft, cap):
    n = afft.shape[1]
    nt = n // TM
    ne = N_EXPERTS
    return pl.pallas_call(
        functools.partial(_select_kernel, cap=cap),
        out_shape=[jax.ShapeDtypeStruct((ne, n), F32), jax.ShapeDtypeStruct((n, ne), F32),
                   jax.ShapeDtypeStruct((nt, 1, ne), jnp.int32), jax.ShapeDtypeStruct((nt, 1, ne), jnp.int32)],
        grid=(nt,),
        in_specs=[pl.BlockSpec((ne, n), lambda t: (0, 0))],
        out_specs=[pl.BlockSpec((ne, TM), lambda t: (0, t)), pl.BlockSpec((TM, ne), lambda t: (t, 0)),
                   pl.BlockSpec((1, 1, ne), lambda t: (t, 0, 0)), pl.BlockSpec((1, 1, ne), lambda t: (t, 0, 0))],
        scratch_shapes=[pltpu.VMEM((ne, n), F32), pltpu.VMEM((ne, n), F32),
                        pltpu.VMEM((1, ne), F32), pltpu.VMEM((8, ne), F32)],
        compiler_params=_cparams(("arbitrary",)),
        name="moe_select",
    )(afft)


def _win_index(idx):
    e = jnp.zeros_like(idx)
    for k in range(1, N_EXPERTS):
        e = e + jnp.where(idx >= k * MOE_WIN, 1, 0)
    return e, idx - e * MOE_WIN


def _gather_consts():
    ne, win, al = N_EXPERTS, MOE_WIN, MOE_ALIGN
    rows = np.arange(ne * win)
    ex = (rows[:, None] // win == np.arange(ne)[None, :]).astype(np.float32)
    rows16 = np.arange(ne * al)
    ex16 = (rows16[:, None] // al == np.arange(ne)[None, :]).astype(np.float32)
    return (jnp.asarray(np.concatenate([ex, ex], axis=1), BF16),
            jnp.asarray(np.broadcast_to((rows % win)[:, None], (ne * win, TM)), F32),
            jnp.asarray(np.concatenate([ex16, ex16], axis=1), BF16),
            jnp.asarray(np.broadcast_to((rows16 % al)[:, None], (ne * al, TM)), F32))


def _gather_kernel(tstart_sm, cnt_sm, rounds_sm, h2_ref, lpos_t_ref, ex_ref, row_ref, ex16_ref, row16_ref,
                   xe_ref, stage, pend, sem, *, nt, cap):
    i = pl.program_id(0)
    slot = lax.rem(i, 2)
    ne, w, win, al = N_EXPERTS, MOE_W, MOE_WIN, MOE_ALIGN

    @pl.when(i == 0)
    def _():
        pend[...] = jnp.zeros(pend.shape, F32)

    def first_slot(e):
        return tstart_sm[i * ne + e]

    def below(e):
        return first_slot(e) & (al - 1)

    lp_t = lpos_t_ref[...]
    lp_t = jnp.where(lp_t < 0.0, -1e6, lp_t)
    sub = _iota((ne, TM), 0)
    shift = jnp.zeros((ne, TM), F32)
    shift16 = jnp.zeros((ne, TM), F32)
    for e in range(ne):
        shift = jnp.where(sub == e, below(e).astype(F32), shift)
        filled = below(e) + cnt_sm[i * ne + e]
        shift16 = jnp.where(sub == e, (below(e) - ((filled >> 4) << 4)).astype(F32), shift16)
    tgt = _dot(ex_ref[...], jnp.concatenate([lp_t, shift], axis=0).astype(BF16))

    def build(k, sl):
        oh = jnp.where(tgt == row_ref[...] + (k * w).astype(F32), 1.0, 0.0).astype(BF16)
        return _dot(oh, h2_ref[...])

    def copy(e, start, sl):
        return pltpu.make_async_copy(
            stage.at[sl, pl.ds(e * win, win)],
            xe_ref.at[e, pl.ds(pl.multiple_of(start, al), win)],
            sem.at[sl])

    def issue(k, sl):
        for e in range(ne):
            copy(e, jnp.minimum(first_slot(e) - below(e) + k * w, cap), sl).start()

    def wait_all(sl):
        for e in range(ne):
            copy(e, 0, sl).wait()

    zero = jnp.int32(0)
    g = build(zero, slot)
    stage[slot] = g.astype(BF16)
    for e in range(ne):
        stage[slot, e * win:e * win + al, :] = (g[e * win:e * win + al] + pend[e]).astype(BF16)

    tgt16 = _dot(ex16_ref[...], jnp.concatenate([lp_t, shift16], axis=0).astype(BF16))
    oh16 = jnp.where(tgt16 == row16_ref[...], 1.0, 0.0).astype(BF16)
    new_pend = _dot(oh16, h2_ref[...])
    for e in range(ne):
        keep = jnp.where(below(e) + cnt_sm[i * ne + e] < al, 1.0, 0.0)
        pend[e] = new_pend[e * al:(e + 1) * al] + keep * pend[e]

    @pl.when(i > 0)
    def _():
        wait_all(1 - slot)

    issue(zero, slot)

    def extra(k, carry):
        wait_all(slot)
        stage[slot] = build(k, slot).astype(BF16)
        issue(k, slot)
        return carry

    lax.fori_loop(1, rounds_sm[i], extra, 0)

    @pl.when(i == nt - 1)
    def _():
        wait_all(slot)
        stage[1 - slot, 0:win, :] = jnp.zeros((win, D), BF16)
        for e in range(ne):
            pltpu.make_async_copy(
                stage.at[1 - slot, pl.ds(0, win)], xe_ref.at[e, pl.ds(cap, win)], sem.at[1 - slot]).start()
        wait_all(1 - slot)


def _gather(h2b, lpos_t, tstart, cnt, rounds, cap, tile_off):
    n = lpos_t.shape[1]
    nt = n // TM
    ne, win = N_EXPERTS, MOE_WIN
    consts = _gather_consts()
    grid_spec = pltpu.PrefetchScalarGridSpec(
        num_scalar_prefetch=3,
        grid=(nt,),
        in_specs=[pl.BlockSpec((TM, D), lambda i, a, b, c: (tile_off + i, 0)),
                  pl.BlockSpec((ne, TM), lambda i, a, b, c: (0, i))]
                 + [pl.BlockSpec(t.shape, lambda i, a, b, c: (0, 0)) for t in consts],
        out_specs=pl.BlockSpec(memory_space=pl.ANY),
        scratch_shapes=[pltpu.VMEM((2, ne * win, D), BF16), pltpu.VMEM((ne, MOE_ALIGN, D), F32),
                        pltpu.SemaphoreType.DMA((2,))],
    )
    return pl.pallas_call(
        functools.partial(_gather_kernel, nt=nt, cap=cap),
        out_shape=jax.ShapeDtypeStruct((ne, cap + win, D), BF16),
        grid_spec=grid_spec,
        compiler_params=_cparams(("arbitrary",)),
        name="moe_gather",
    )(tstart, cnt, rounds, h2b, lpos_t, *consts)


FFN_TILES_CTX = (CAP_FACTOR * R_CTX // N_EXPERTS) // FFN_TM
FFN_TILES_LAT = (CAP_FACTOR * R_LAT // N_EXPERTS) // FFN_TM


def _ffn_kernel(xc_ref, xl_ref, wg_ref, wu_ref, wd_ref, yc_ref, yl_ref, wg_s, wu_s, wd_s):
    j = pl.program_id(1)

    @pl.when(j == 0)
    def _():
        wg_s[...] = wg_ref[0, 0].astype(BF16)
        wu_s[...] = wu_ref[0, 0].astype(BF16)
        wd_s[...] = wd_ref[0, 0].astype(BF16)

    x = jnp.where(j < FFN_TILES_CTX, xc_ref[0], xl_ref[0])
    hid = (_silu(_dot(x, wg_s[...])) * _dot(x, wu_s[...])).astype(BF16)
    y = _dot(hid, wd_s[...]).astype(BF16)

    @pl.when(j < FFN_TILES_CTX)
    def _():
        yc_ref[0] = y

    @pl.when(j >= FFN_TILES_CTX)
    def _():
        yl_ref[0] = y


def _ffn(xe_ctx, xe_lat, wg, wu, wd, layer):
    ne = N_EXPERTS
    blk = (1, FFN_TM, D)
    wspec = pl.BlockSpec((1, 1, D, D), lambda e, j: (layer, e, 0, 0))
    ctx_map = lambda e, j: (e, jnp.minimum(j, FFN_TILES_CTX - 1), 0)
    lat_map = lambda e, j: (e, jnp.maximum(j - FFN_TILES_CTX, 0), 0)
    return pl.pallas_call(
        _ffn_kernel,
        out_shape=[jax.ShapeDtypeStruct((ne, FFN_TILES_CTX * FFN_TM, D), BF16),
                   jax.ShapeDtypeStruct((ne, FFN_TILES_LAT * FFN_TM, D), BF16)],
        grid=(ne, FFN_TILES_CTX + FFN_TILES_LAT),
        in_specs=[pl.BlockSpec(blk, ctx_map), pl.BlockSpec(blk, lat_map), wspec, wspec, wspec],
        out_specs=[pl.BlockSpec(blk, ctx_map), pl.BlockSpec(blk, lat_map)],
        scratch_shapes=[pltpu.VMEM((D, D), BF16)] * 3,
        compiler_params=_cparams(("arbitrary", "arbitrary")),
        name="moe_ffn",
    )(xe_ctx, xe_lat, wg, wu, wd)


def _combine_kernel(tstart_sm, rounds_sm, lpos_ref, aff_ref, x_ref, mod_ref, fg_ref, ye_ref, o_ref,
                    ybuf, acc_s, sem, *, nt, cap, final):
    i = pl.program_id(0)
    slot = lax.rem(i, COMBINE_BUFS)
    ne, w, win, al = N_EXPERTS, MOE_W, MOE_WIN, MOE_ALIGN
    m = ne * win

    def win_start(tile, e, k):
        first = tstart_sm[tile * ne + e] + k * w
        return jnp.minimum((first >> 4) << 4, cap - win)

    def copy(e, start, sl):
        return pltpu.make_async_copy(
            ye_ref.at[e, pl.ds(pl.multiple_of(start, al), win)],
            ybuf.at[sl, pl.ds(e * win, win)],
            sem.at[sl])

    def fetch(tile, k, sl):
        for e in range(ne):
            copy(e, win_start(tile, e, k), sl).start()

    def wait_all(sl):
        for e in range(ne):
            copy(e, 0, sl).wait()

    zero = jnp.int32(0)

    ahead = COMBINE_BUFS - 1

    @pl.when(i == 0)
    def _():
        for t in range(min(ahead, nt)):
            fetch(jnp.int32(t), zero, jnp.int32(t))

    wait_all(slot)

    @pl.when(i + ahead < nt)
    def _():
        fetch(i + ahead, zero, lax.rem(i + ahead, COMBINE_BUFS))

    e_lane, r_lane = _win_index(_iota((1, m), 1))
    expand = jnp.where(_win_index(_iota((ne, m), 1))[0] == _iota((ne, m), 0), 1.0, 0.0).astype(BF16)
    aff = aff_ref[...]
    lpx = _dot(lpos_ref[...].astype(BF16), expand)
    affx = _dot(aff.astype(BF16), expand)
    rl = r_lane.astype(F32)

    def compute(k, sl):
        shift = jnp.zeros((1, m), F32)
        for e in range(ne):
            sh = (tstart_sm[i * ne + e] - win_start(i, e, k)).astype(F32)
            shift = jnp.where(e_lane == e, sh, shift)
        lo = (k * w).astype(F32)
        hit = jnp.where(lpx + shift == rl, 1.0, 0.0) * jnp.where(lpx >= lo, 1.0, 0.0) * jnp.where(lpx < lo + w, 1.0, 0.0)
        return _dot((hit * affx).astype(BF16), ybuf[sl])

    acc_s[...] = compute(zero, slot)

    def extra(k, carry):
        fetch(i, k, slot)
        wait_all(slot)
        acc_s[...] = acc_s[...] + compute(k, slot)
        return carry

    lax.fori_loop(1, rounds_sm[i], extra, 0)
    x = x_ref[...] + mod_ref[0, 5:6, :] * acc_s[...]
    if final:
        x = _rms(x, fg_ref[...])
    o_ref[...] = x


def _combine(x, mod_l, lpos, aff, ye, tstart, rounds, fg, cap, tile_off, final):
    n = lpos.shape[0]
    nt = n // TM
    ne = N_EXPERTS
    grid_spec = pltpu.PrefetchScalarGridSpec(
        num_scalar_prefetch=2,
        grid=(nt,),
        in_specs=[pl.BlockSpec((TM, ne), lambda i, a, b: (i, 0)),
                  pl.BlockSpec((TM, ne), lambda i, a, b: (tile_off + i, 0)),
                  pl.BlockSpec((TM, D), lambda i, a, b: (tile_off + i, 0)),
                  pl.BlockSpec((1, 6, D), lambda i, a, b: (_mod_row(tile_off + i), 0, 0)),
                  pl.BlockSpec((1, D), lambda i, a, b: (0, 0)),
                  pl.BlockSpec(memory_space=pl.ANY)],
        out_specs=pl.BlockSpec((TM, D), lambda i, a, b: (i, 0)),
        scratch_shapes=[pltpu.VMEM((COMBINE_BUFS, ne * MOE_WIN, D), BF16), pltpu.VMEM((TM, D), F32),
                        pltpu.SemaphoreType.DMA((COMBINE_BUFS,))],
    )
    return pl.pallas_call(
        functools.partial(_combine_kernel, nt=nt, cap=cap, final=final),
        out_shape=jax.ShapeDtypeStruct((n, D), F32),
        grid_spec=grid_spec,
        compiler_params=_cparams(("arbitrary",)),
        name="moe_combine",
    )(tstart, rounds, lpos, aff, x, mod_l, fg, ye)


def _moe_layer(x, mod_l, h2b, aff, afft, wg, wu, wd, fg, layer):
    final = layer == DEPTH - 1
    groups = []
    for n, tile_off, afft_g in ((R_CTX, 0, afft[:, :R_CTX]), (R_LAT, NT_CTX, afft[:, R_CTX:])):
        cap = (CAP_FACTOR * n) // N_EXPERTS
        lpos_t, lpos, tstart, cnt = _select(afft_g, cap)
        tstart = tstart.reshape(-1)
        rounds = jnp.maximum((jnp.max(cnt.reshape(-1, N_EXPERTS), axis=1) + MOE_W - 1) // MOE_W, 1).astype(jnp.int32)
        xe = _gather(h2b, lpos_t, tstart, cnt.reshape(-1), rounds, cap, tile_off)
        groups.append((lpos, tstart, rounds, cap, tile_off, xe))
    ye = _ffn(groups[0][-1], groups[1][-1], wg, wu, wd, layer)
    return [_combine(x, mod_l, lpos, aff, ye_g, tstart, rounds, fg, cap, tile_off, final)
            for (lpos, tstart, rounds, cap, tile_off, _), ye_g in zip(groups, ye)]


def _layout_w_in(w):
    o = np.cumsum([0, 256, 256, 256, 384, 256, 32, 256, 256, 256, 256, 256, 128, 128, 8, 4096])
    wb = w.astype(BF16)
    zeros = lambda n: jnp.zeros((D, n), BF16)
    w_all = jnp.concatenate([wb[:, o[0]:o[5]], zeros(64), wb[:, o[5]:o[6]], zeros(32), wb[:, o[6]:o[14]],
                             zeros(120), wb[:, o[14]:o[15]]], axis=1)
    return w_all, wb[:, o[13]:o[14]].T


def _layout_mla(w_uq, w_ukv):
    uq = w_uq.reshape(MLA_Q_RANK, N_HEADS, MLA_NOPE + MLA_ROPE)
    uq = jnp.pad(uq, ((0, 0), (0, 0), (0, LANES - MLA_NOPE - MLA_ROPE))).reshape(MLA_Q_RANK, N_HEADS * LANES)
    ukv = w_ukv.reshape(MLA_KV_RANK, N_HEADS, MLA_NOPE + MLA_V)
    uk = jnp.pad(ukv[:, :, :MLA_NOPE], ((0, 0), (0, 0), (0, LANES - MLA_NOPE))).reshape(MLA_KV_RANK, N_HEADS * LANES)
    uv = jnp.pad(ukv[:, :, MLA_NOPE:], ((0, 0), (0, 0), (0, LANES - MLA_V))).reshape(MLA_KV_RANK, N_HEADS * LANES)
    return uq.astype(BF16), uk.astype(BF16), uv.astype(BF16)


def kernel(x_prompt, x_sample, cache_na_k, cache_na_v, cache_mla_ckv, cache_mla_krope, cache_df_k, cache_df_v,
           state_ssm, c, c_ctx, mod_w, mod_b, norm1_g, norm2_g, w_in, na_rel_bias, mla_q_norm_g, mla_kv_norm_g,
           mla_w_uq, mla_w_ukv, df_lambda, df_subln_g, ssm_conv_w, ssm_conv_b, ssm_dt_bias, ssm_a_log, ssm_d,
           ssm_norm_g, w_branch, w_out, router_w, exp_w_gate, exp_w_up, exp_w_down, final_norm_g):
    x_ctx = x_prompt.reshape(R_CTX, D)
    x_lat = x_sample.reshape(R_LAT, D)
    cvec = jnp.concatenate([c_ctx[None, :], c, jnp.zeros((16 - 1 - B_LAT, D), F32)], axis=0)
    mod = _modulation(cvec, mod_w, mod_b).reshape(DEPTH, 16, 6, D)

    cos32, sa32, sb32 = _rope_tables32()
    pad128 = lambda a, fill: np.pad(a, ((0, 0), (64, 32)), constant_values=fill)
    rope128 = (pad128(cos32, 1.0), pad128(sa32, 0.0), pad128(sb32, 0.0))
    rope256 = tuple(np.tile(a, (1, 8)) for a in (cos32, sa32, sb32))
    fg = final_norm_g.reshape(1, D)
    cna_k = cache_na_k.reshape(B_LAT, DEPTH, PAST, 256)
    cna_v = cache_na_v.reshape(B_LAT, DEPTH, PAST, 256)
    cdf_k = cache_df_k.reshape(B_LAT, DEPTH, PAST, 256)
    cdf_v = cache_df_v.reshape(B_LAT, DEPTH, PAST, 256)

    outs = {k: [] for k in ("na_k", "na_v", "ckv", "krope", "df_k", "df_v", "ssm")}
    for l in range(DEPTH):
        mod_l = mod[l]
        (u_na, u_mla, u_df, u_ssm, gates, dt_t, na_k, na_v, df_k, df_v, krope) = _in_proj(
            x_ctx, x_lat, mod_l, norm1_g[l].reshape(1, D), *_layout_w_in(w_in[l]))
        bias = _na_bias_table(na_rel_bias[l])
        br_na = (_attn_ctx(u_na), _na_lat(u_na, cna_k, cna_v, bias, l))
        wuq, wuk, wuv = _layout_mla(mla_w_uq[l], mla_w_ukv[l])
        gq = mla_q_norm_g[l].reshape(1, MLA_Q_RANK)
        gkv = mla_kv_norm_g[l].reshape(1, MLA_KV_RANK)
        mla_c, ckv_new = _mla(u_mla, gq, gkv, wuq, wuk, wuv)
        kr_ctx = jnp.pad(cache_mla_krope[:, l], ((0, 0), (0, 0), (64, 32)))
        (mla_l,) = _mla(u_mla, gq, gkv, wuq, wuk, wuv, rope128, cache_mla_ckv, kr_ctx, l)
        lam_init = 0.8 - 0.6 * math.exp(-0.3 * l)
        gs = df_subln_g[l].reshape(1, DF_V)
        br_df = (_df(u_df, df_lambda[l], gs, lam_init),
                 _df(u_df, df_lambda[l], gs, lam_init, rope256, cdf_k, cdf_v, l))
        d_vec = jnp.repeat(ssm_d[l], SSM_P).reshape(1, 256)
        gn = ssm_norm_g[l].reshape(1, 256)
        ssm_c, st_new = _ssd(u_ssm, dt_t, ssm_conv_w[l], ssm_conv_b[l], ssm_dt_bias[l], ssm_a_log[l], d_vec, gn)
        (ssm_l,) = _ssd(u_ssm, dt_t, ssm_conv_w[l], ssm_conv_b[l], ssm_dt_bias[l], ssm_a_log[l], d_vec, gn,
                        state_ssm, l)
        x_mid, h2b, aff, afft = _merge(x_ctx, x_lat, mod_l, (br_na, (mla_c, mla_l), br_df, (ssm_c, ssm_l)), gates,
                                       w_branch[l].astype(BF16), w_out[l].astype(BF16),
                                       norm2_g[l].reshape(1, D), router_w[l])
        x_ctx, x_lat = _moe_layer(x_mid, mod_l, h2b, aff, afft, exp_w_gate, exp_w_up, exp_w_down, fg, l)
        outs["na_k"].append(na_k.reshape(B_CTX, L_CTX, N_HEADS, 64))
        outs["na_v"].append(na_v.reshape(B_CTX, L_CTX, N_HEADS, 64))
        outs["ckv"].append(ckv_new.reshape(B_CTX, L_CTX, MLA_KV_RANK))
        outs["krope"].append(krope.reshape(B_CTX, L_CTX, MLA_ROPE))
        outs["df_k"].append(df_k.reshape(B_CTX, L_CTX, N_HEADS, 2, DF_HD))
        outs["df_v"].append(df_v.reshape(B_CTX, L_CTX, N_HEADS, DF_V))
        outs["ssm"].append(st_new)
    stack = lambda k: jnp.stack(outs[k], axis=1)
    return (x_ctx.reshape(B_CTX, L_CTX, D), x_lat.reshape(B_LAT, L_LAT, D), stack("na_k"), stack("na_v"),
            stack("ckv"), stack("krope"), stack("df_k"), stack("df_v"), stack("ssm"))
```

```python
import functools
import math

import numpy as np
import jax
import jax.numpy as jnp
from jax import lax
from jax.experimental import pallas as pl
from jax.experimental.pallas import tpu as pltpu

F32 = jnp.float32
BF16 = jnp.bfloat16

D = 1024
B_CTX, L_CTX = 32, 256
B_LAT, L_LAT = 8, 2048
PAST = 256
DEPTH = 2
GRID_W = 64
EPS = 1e-6
ROPE_BASE = 10000.0
N_HEADS = 4
NA_KH, NA_KW = 8, 16
MLA_NOPE, MLA_ROPE, MLA_V = 64, 32, 64
MLA_Q_RANK, MLA_KV_RANK = 384, 256
DF_HD, DF_V = 32, 64
SSM_P, SSM_N, SSM_GROUPS = 64, 64, 2
N_EXPERTS = 16
CAP_FACTOR = 2

LANES = 128
SUBLANES = 8
TM = 256
R_CTX = B_CTX * L_CTX
R_LAT = B_LAT * L_LAT
R_ALL = R_CTX + R_LAT
NT_CTX = R_CTX // TM
NT_LAT = R_LAT // TM
NT_ALL = R_ALL // TM
TILES_PER_LAT_BATCH = L_LAT // TM
TQ = 256
ATT_TQ = 512
SSD_Q = 256
MOE_W = 64
MOE_ALIGN = 16
MOE_WIN = MOE_W + MOE_ALIGN
COMBINE_BUFS = 3
SELECT_BISECT_STEPS = 40
FFN_TM = 512
NA_QROWS = 4
NA_KROWS = NA_QROWS + NA_KH - 1
NA_NK = NA_KROWS * GRID_W
VMEM_LIMIT = 56 * 1024 * 1024


def _cparams(sem):
    return pltpu.CompilerParams(dimension_semantics=sem, vmem_limit_bytes=VMEM_LIMIT)


def _nt(a, b):
    return lax.dot_general(a, b, (((1,), (1,)), ((), ())), preferred_element_type=F32)


def _dot(a, b):
    return jnp.dot(a, b, preferred_element_type=F32)


def _rms(x, g):
    return x * lax.rsqrt(jnp.mean(x * x, axis=-1, keepdims=True) + EPS) * g


def _silu(x):
    return x * jax.nn.sigmoid(x)


def _softmax_rows(s):
    m = jnp.max(s, axis=-1, keepdims=True)
    e = jnp.exp(s - m)
    return e * (1.0 / jnp.sum(e, axis=-1, keepdims=True))


LOG2E = math.log2(math.e)


def _exp_rows(s, scale):
    m = jnp.max(s, axis=-1, keepdims=True)
    e = jnp.exp2((s - m) * (scale * LOG2E))
    return e, jnp.sum(e, axis=-1, keepdims=True)


def _exp_only(s, scale):
    return jnp.exp2((s - jnp.max(s, axis=-1, keepdims=True)) * (scale * LOG2E))


def _pv_normalised(e, v_ext):
    o = _dot(e.astype(BF16), v_ext)
    return o[:, 0:64] * (1.0 / o[:, 64:65])


def _store_v_ext(v_s, row0, v):
    rows = v.shape[0]
    lane = _iota((rows, LANES), 1)
    ones_col = jnp.where(lane == 64, 1.0, 0.0)
    for pair in range(2):
        blk = v[:, LANES * pair:LANES * pair + LANES]
        even = jnp.where(lane < 64, blk, ones_col)
        odd = jnp.where(lane < 64, pltpu.roll(blk, 64, 1), ones_col)
        v_s[row0:row0 + rows, 2 * LANES * pair:2 * LANES * pair + LANES] = even.astype(BF16)
        v_s[row0:row0 + rows, 2 * LANES * pair + LANES:2 * LANES * pair + 2 * LANES] = odd.astype(BF16)


def _split3(a):
    a1 = a.astype(BF16)
    r1 = a - a1.astype(F32)
    a2 = r1.astype(BF16)
    a3 = (r1 - a2.astype(F32)).astype(BF16)
    return a1, a2, a3


def _iota(shape, dim):
    return lax.broadcasted_iota(jnp.int32, shape, dim)


def _mod_row(i):
    return jnp.where(i < NT_CTX, 0, 1 + (i - NT_CTX) // TILES_PER_LAT_BATCH)


MOD_TN = 1536


def _mod_kernel(c_ref, w_ref, b_ref, o_ref):
    s = _silu(c_ref[...]).astype(BF16)
    o_ref[0] = _dot(s, w_ref[0].astype(BF16)) + b_ref[0]


def _modulation(cvec, mod_w, mod_b):
    n = 6 * D
    return pl.pallas_call(
        _mod_kernel,
        out_shape=jax.ShapeDtypeStruct((DEPTH, 16, n), F32),
        grid=(DEPTH, n // MOD_TN),
        in_specs=[pl.BlockSpec((16, D), lambda l, j: (0, 0)),
                  pl.BlockSpec((1, D, MOD_TN), lambda l, j: (l, 0, j)),
                  pl.BlockSpec((1, 1, MOD_TN), lambda l, j: (l, 0, j))],
        out_specs=pl.BlockSpec((1, 16, MOD_TN), lambda l, j: (l, 0, j)),
        compiler_params=_cparams(("parallel", "parallel")),
        name="modulation",
    )(cvec, mod_w, mod_b.reshape(DEPTH, 1, n))


W_NA, W_MLA, W_DF, W_SSM, W_GATE = 768, 768, 768, 896, 4 * D


W_IN_COLS = np.cumsum([0, W_NA, W_MLA, W_DF, W_SSM, W_GATE])
W_IN_ALL = int(W_IN_COLS[-1])


def _ctx_tile(i):
    return jnp.minimum(i, NT_CTX - 1)


def _lat_tile(i):
    return jnp.maximum(i - NT_CTX, 0)


def _pick_group(i, ctx_ref, lat_ref):
    return jnp.where(i < NT_CTX, ctx_ref[...], lat_ref[...])


def _in_kernel(xc_ref, xl_ref, mod_ref, g_ref, w_ref, wdt_t,
               ona, omla, odf, ossm, ogate, odt_t, onak, onav, odfk, odfv, okr):
    i = pl.program_id(0)
    x = _pick_group(i, xc_ref, xl_ref)
    h = _rms(x, g_ref[...]) * (1.0 + mod_ref[0, 1:2, :]) + mod_ref[0, 0:1, :]
    hb = h.astype(BF16)
    c = W_IN_COLS
    una = _dot(hb, w_ref[:, c[0]:c[1]])
    umla = _dot(hb, w_ref[:, c[1]:c[2]])
    udf = _dot(hb, w_ref[:, c[2]:c[3]])
    ona[...] = una
    omla[...] = umla
    odf[...] = udf
    ossm[...] = _dot(hb, w_ref[:, c[3]:c[4]])
    ogate[...] = jax.nn.sigmoid(_dot(hb, w_ref[:, c[4]:c[5]])).astype(BF16)
    odt_t[...] = _nt(wdt_t[...], hb)

    @pl.when(i < NT_CTX)
    def _():
        onak[...] = una[:, 256:512]
        onav[...] = una[:, 512:768]
        odfk[...] = udf[:, 256:512]
        odfv[...] = udf[:, 512:768]
        okr[...] = umla[:, 704:736]


def _in_proj(x_ctx, x_lat, mod_l, norm_g, w_all, w_dt_t):
    widths = (W_NA, W_MLA, W_DF, W_SSM)
    const = lambda i: (0, 0)
    row = lambda i: (i, 0)
    ctx_row = lambda i: (_ctx_tile(i), 0)
    out_shape = [jax.ShapeDtypeStruct((R_ALL, w), F32) for w in widths]
    out_shape += [jax.ShapeDtypeStruct((R_ALL, W_GATE), BF16), jax.ShapeDtypeStruct((8, R_ALL), F32)]
    out_shape += [jax.ShapeDtypeStruct((R_CTX, 256), F32)] * 4 + [jax.ShapeDtypeStruct((R_CTX, MLA_ROPE), F32)]
    out_specs = [pl.BlockSpec((TM, w), row) for w in widths]
    out_specs += [pl.BlockSpec((TM, W_GATE), row), pl.BlockSpec((8, TM), lambda i: (0, i))]
    out_specs += [pl.BlockSpec((TM, 256), ctx_row)] * 4 + [pl.BlockSpec((TM, MLA_ROPE), ctx_row)]
    return pl.pallas_call(
        _in_kernel,
        out_shape=out_shape,
        grid=(NT_ALL,),
        in_specs=[pl.BlockSpec((TM, D), ctx_row),
                  pl.BlockSpec((TM, D), lambda i: (_lat_tile(i), 0)),
                  pl.BlockSpec((1, 6, D), lambda i: (_mod_row(i), 0, 0)),
                  pl.BlockSpec((1, D), const),
                  pl.BlockSpec((D, W_IN_ALL), const, pipeline_mode=pl.Buffered(1)),
                  pl.BlockSpec((8, D), const, pipeline_mode=pl.Buffered(1))],
        out_specs=out_specs,
        compiler_params=_cparams(("arbitrary",)),
        name="in_proj",
    )(x_ctx, x_lat, mod_l, norm_g, w_all, w_dt_t)


def _rope(x, cos, sin_a, sin_b):
    n = x.shape[-1]
    nxt = pltpu.roll(x, n - 1, 1)
    prv = pltpu.roll(x, 1, 1)
    return x * cos + nxt * sin_a + prv * sin_b


def _rope_tables32():
    t = np.arange(L_LAT)
    quarter = 8
    inv = ROPE_BASE ** (-np.arange(quarter, dtype=np.float64) / quarter)
    rows = (t // GRID_W).astype(np.float64)[:, None]
    cols = (t % GRID_W).astype(np.float64)[:, None]
    ang = np.concatenate([rows * inv, cols * inv], axis=-1)
    cos = np.repeat(np.cos(ang), 2, axis=-1)
    sin = np.repeat(np.sin(ang), 2, axis=-1)
    even = (np.arange(32) % 2 == 0)[None, :]
    sin_a = np.where(even, -sin, 0.0)
    sin_b = np.where(even, 0.0, sin)
    return tuple(np.asarray(a, np.float32) for a in (cos, sin_a, sin_b))


def _attn_ctx_kernel(u_ref, o_ref, v_s):
    scale = 64 ** -0.5
    _store_v_ext(v_s, 0, u_ref[:, 512:768])
    s = jnp.concatenate([_nt(u_ref[:, 64 * h:64 * h + 64].astype(BF16),
                             u_ref[:, 256 + 64 * h:256 + 64 * h + 64].astype(BF16)) for h in range(N_HEADS)], axis=0)
    e = _exp_only(s, scale)
    for h in range(N_HEADS):
        o_ref[:, 64 * h:64 * h + 64] = _pv_normalised(e[L_CTX * h:L_CTX * h + L_CTX],
                                                      v_s[:, 128 * h:128 * h + 128]).astype(BF16)


def _attn_ctx(u_na):
    return pl.pallas_call(
        _attn_ctx_kernel,
        out_shape=jax.ShapeDtypeStruct((R_CTX, 256), BF16),
        grid=(B_CTX,),
        in_specs=[pl.BlockSpec((L_CTX, W_NA), lambda b: (b, 0))],
        out_specs=pl.BlockSpec((L_CTX, 256), lambda b: (b, 0)),
        scratch_shapes=[pltpu.VMEM((L_CTX, 4 * LANES), BF16)],
        compiler_params=_cparams(("parallel",)),
        name="na_ctx",
    )(u_na)


NA_ROWS = L_LAT // GRID_W
NA_NQT = NA_ROWS // NA_QROWS


def _na_pattern(qt):
    return jnp.where(qt == 0, 0, jnp.where(qt == NA_NQT - 1, 2, 1))


def _na_pattern_offsets(p, a):
    q_off = jnp.where(p == 0, 0, jnp.where(p == 1, NA_KH // 2, NA_KROWS - NA_QROWS))
    rs_rel = jnp.where(p == 0, 0, jnp.where(p == 1, a, NA_KROWS - NA_KH))
    return q_off, rs_rel


def _check_na_patterns():
    for qt in range(NA_NQT):
        ks = int(np.clip(NA_QROWS * qt - NA_KH // 2, 0, NA_ROWS - NA_KROWS))
        p = 0 if qt == 0 else (2 if qt == NA_NQT - 1 else 1)
        for a in range(NA_QROWS):
            r = NA_QROWS * qt + a
            rs = int(np.clip(r - NA_KH // 2, 0, NA_ROWS - NA_KH))
            q_off = (0, NA_KH // 2, NA_KROWS - NA_QROWS)[p]
            rs_rel = (0, a, NA_KROWS - NA_KH)[p]
            assert NA_QROWS * qt - ks == q_off and rs - ks == rs_rel and rs + NA_KH <= ks + NA_KROWS


_check_na_patterns()


def _na_bias_kernel(rb_ref, o_ref):
    p = pl.program_id(0)
    qc = _iota((GRID_W, GRID_W), 0)
    kc = _iota((GRID_W, GRID_W), 1)
    cs = jnp.clip(qc - NA_KW // 2, 0, GRID_W - NA_KW)
    col_ok = (kc >= cs) & (kc < cs + NA_KW)
    for a in range(NA_QROWS):
        q_off, rs_rel = _na_pattern_offsets(p, a)
        for b in range(NA_KROWS):
            row_ok = (b >= rs_rel) & (b < rs_rel + NA_KH)
            dr = jnp.clip(b - q_off - a + NA_KH - 1, 0, 2 * NA_KH - 2)
            v = jnp.broadcast_to(rb_ref[0, pl.ds(dr, 1), :], (GRID_W, LANES))
            t = pltpu.roll(v, LANES - (NA_KW - 1), 1, stride=1, stride_axis=0)[:, :GRID_W]
            o_ref[0, 0, GRID_W * a:GRID_W * a + GRID_W, GRID_W * b:GRID_W * b + GRID_W] = jnp.where(
                col_ok & row_ok, t * LOG2E, -1e30)


def _na_bias_table(rel_bias):
    rb = jnp.pad(rel_bias.astype(F32), ((0, 0), (0, 0), (0, LANES - (2 * NA_KW - 1))))
    return pl.pallas_call(
        _na_bias_kernel,
        out_shape=jax.ShapeDtypeStruct((3, N_HEADS, TQ, NA_NK), F32),
        grid=(3, N_HEADS),
        in_specs=[pl.BlockSpec((1, 2 * NA_KH - 1, LANES), lambda p, h: (h, 0, 0))],
        out_specs=pl.BlockSpec((1, 1, TQ, NA_NK), lambda p, h: (p, h, 0, 0)),
        compiler_params=_cparams(("parallel", "parallel")),
        name="na_bias",
    )(rb)


def _na_lat_kernel(u_ref, kc_ref, vc_ref, bias_ref, o_ref, v_s, vc_s):
    qt = pl.program_id(1)
    c = 64 ** -0.5 * LOG2E
    ks = jnp.clip(NA_QROWS * qt - NA_KH // 2, 0, NA_ROWS - NA_KROWS)
    kstart = pl.multiple_of(ks * GRID_W, GRID_W)
    qstart = pl.multiple_of(qt * TQ, TQ)
    _store_v_ext(v_s, 0, u_ref[pl.ds(kstart, NA_NK), 512:768])
    _store_v_ext(vc_s, 0, vc_ref[0, 0])
    for h in range(N_HEADS):
        q = u_ref[pl.ds(qstart, TQ), 64 * h:64 * h + 64].astype(BF16)
        k = u_ref[pl.ds(kstart, NA_NK), 256 + 64 * h:256 + 64 * h + 64].astype(BF16)
        kc = kc_ref[0, 0, :, 64 * h:64 * h + 64].astype(BF16)
        t_loc = _nt(q, k) * c + bias_ref[0, h]
        t_ctx = _nt(q, kc) * c
        m = jnp.maximum(jnp.max(t_loc, axis=-1, keepdims=True), jnp.max(t_ctx, axis=-1, keepdims=True))
        e_loc = jnp.exp2(t_loc - m).astype(BF16)
        e_ctx = jnp.exp2(t_ctx - m).astype(BF16)
        o = _dot(e_loc, v_s[:, 128 * h:128 * h + 128]) + _dot(e_ctx, vc_s[:, 128 * h:128 * h + 128])
        o_ref[:, 64 * h:64 * h + 64] = (o[:, 0:64] * (1.0 / o[:, 64:65])).astype(BF16)


def _na_lat(u_na, k_ctx, v_ctx, bias, layer):
    nqt = L_LAT // TQ
    cache = pl.BlockSpec((1, 1, PAST, 256), lambda b, t: (b, layer, 0, 0))
    return pl.pallas_call(
        _na_lat_kernel,
        out_shape=jax.ShapeDtypeStruct((R_LAT, 256), BF16),
        grid=(B_LAT, nqt),
        in_specs=[pl.BlockSpec((L_LAT, W_NA), lambda b, t: (R_CTX // L_LAT + b, 0)),
                  cache, cache,
                  pl.BlockSpec((1, N_HEADS, TQ, NA_NK), lambda b, t: (_na_pattern(t), 0, 0, 0))],
        out_specs=pl.BlockSpec((TQ, 256), lambda b, t: (b * nqt + t, 0)),
        scratch_shapes=[pltpu.VMEM((NA_NK, 4 * LANES), BF16), pltpu.VMEM((PAST, 4 * LANES), BF16)],
        compiler_params=_cparams(("parallel", "arbitrary")),
        name="na_lat",
    )(u_na, k_ctx, v_ctx, bias)


def _mla_kernel(*refs, latent, seq):
    if latent:
        (u_ref, gq_ref, gkv_ref, wuq_ref, wuk_ref, wuv_ref, cos_ref, sa_ref, sb_ref,
         ckv_c_ref, kr_c_ref, o_ref, k_s, v_s) = refs
    else:
        (u_ref, gq_ref, gkv_ref, wuq_ref, wuk_ref, wuv_ref, o_ref, ckv_o_ref, k_s, v_s) = refs
    scale = (MLA_NOPE + MLA_ROPE) ** -0.5
    ckv = _rms(u_ref[:, 384:640], gkv_ref[...])
    kr = u_ref[:, 640:768]
    if latent:
        kr = _rope(kr, cos_ref[...], sa_ref[...], sb_ref[...])
    else:
        ckv_o_ref[...] = ckv
    ckv_b = ckv.astype(BF16)
    for h in range(N_HEADS):
        k_s[0:seq, 128 * h:128 * h + 128] = (_dot(ckv_b, wuk_ref[:, 128 * h:128 * h + 128]) + kr).astype(BF16)
    ones_col = jnp.where((_iota((1, 4 * LANES), 1) & (LANES - 1)) == MLA_V, 1.0, 0.0)
    v_s[0:seq, :] = (_dot(ckv_b, wuv_ref[...]) + ones_col).astype(BF16)
    if latent:
        cc = ckv_c_ref[0, 0].astype(BF16)
        krc = kr_c_ref[0]
        for h in range(N_HEADS):
            k_s[seq:seq + PAST, 128 * h:128 * h + 128] = (
                _dot(cc, wuk_ref[:, 128 * h:128 * h + 128]) + krc).astype(BF16)
        v_s[seq:seq + PAST, :] = (_dot(cc, wuv_ref[...]) + ones_col).astype(BF16)

    tq = min(ATT_TQ, seq)

    def q_tile(t, carry):
        r0 = pl.multiple_of(t * tq, tq)
        cq = _rms(u_ref[pl.ds(r0, tq), 0:384], gq_ref[...]).astype(BF16)
        for h in range(N_HEADS):
            q = _dot(cq, wuq_ref[:, 128 * h:128 * h + 128])
            q = _rope(q, cos_ref[pl.ds(r0, tq), :], sa_ref[pl.ds(r0, tq), :], sb_ref[pl.ds(r0, tq), :])
            e = _exp_only(_nt(q.astype(BF16), k_s[:, 128 * h:128 * h + 128]), scale)
            o_ref[pl.ds(r0, tq), 64 * h:64 * h + 64] = _pv_normalised(e, v_s[:, 128 * h:128 * h + 128]).astype(BF16)
        return carry

    if latent:
        lax.fori_loop(0, seq // tq, q_tile, 0)
    else:
        cq = _rms(u_ref[:, 0:384], gq_ref[...]).astype(BF16)
        s = jnp.concatenate([_nt(_dot(cq, wuq_ref[:, 128 * h:128 * h + 128]).astype(BF16),
                                 k_s[:, 128 * h:128 * h + 128]) for h in range(N_HEADS)], axis=0)
        e = _exp_only(s, scale)
        for h in range(N_HEADS):
            o_ref[:, 64 * h:64 * h + 64] = _pv_normalised(e[seq * h:seq * h + seq],
                                                          v_s[:, 128 * h:128 * h + 128]).astype(BF16)


def _mla(u_mla, gq, gkv, wuq, wuk, wuv, rope128=None, ckv_ctx=None, kr_ctx=None, layer=0):
    latent = rope128 is not None
    seq = L_LAT if latent else L_CTX
    nb = B_LAT if latent else B_CTX
    off = R_CTX // L_LAT if latent else 0
    lk = seq + PAST if latent else seq
    const = lambda b: (0, 0)
    in_specs = [pl.BlockSpec((seq, W_MLA), lambda b: (off + b, 0)),
                pl.BlockSpec((1, MLA_Q_RANK), const),
                pl.BlockSpec((1, MLA_KV_RANK), const),
                pl.BlockSpec((MLA_Q_RANK, 512), const),
                pl.BlockSpec((MLA_KV_RANK, 512), const),
                pl.BlockSpec((MLA_KV_RANK, 512), const)]
    args = [u_mla, gq, gkv, wuq, wuk, wuv]
    out_shape = [jax.ShapeDtypeStruct((nb * seq, 256), BF16)]
    out_specs = [pl.BlockSpec((seq, 256), lambda b: (b, 0))]
    if latent:
        in_specs += [pl.BlockSpec((seq, LANES), const)] * 3
        in_specs += [pl.BlockSpec((1, 1, PAST, MLA_KV_RANK), lambda b: (b, layer, 0, 0)),
                     pl.BlockSpec((1, PAST, LANES), lambda b: (b, 0, 0))]
        args += list(rope128) + [ckv_ctx, kr_ctx]
    else:
        out_shape.append(jax.ShapeDtypeStruct((nb * seq, MLA_KV_RANK), F32))
        out_specs.append(pl.BlockSpec((seq, MLA_KV_RANK), lambda b: (b, 0)))
    return pl.pallas_call(
        functools.partial(_mla_kernel, latent=latent, seq=seq),
        out_shape=out_shape,
        grid=(nb,),
        in_specs=in_specs,
        out_specs=out_specs,
        scratch_shapes=[pltpu.VMEM((lk, 512), BF16), pltpu.VMEM((lk, 512), BF16)],
        compiler_params=_cparams(("parallel",)),
        name="mla_lat" if latent else "mla_ctx",
    )(*args)


def _df_kernel(*refs, latent, seq, lam_init):
    if latent:
        (u_ref, lv_ref, gs_ref, cos_ref, sa_ref, sb_ref, kc_ref, vc_ref, o_ref, k_s, v_s) = refs
    else:
        (u_ref, lv_ref, gs_ref, o_ref, k_s, v_s) = refs
    scale = DF_HD ** -0.5
    lv = lv_ref[...]
    lam = (jnp.exp(jnp.sum(lv[0:1] * lv[1:2], axis=1, keepdims=True))
           - jnp.exp(jnp.sum(lv[2:3] * lv[3:4], axis=1, keepdims=True)) + lam_init)
    k = u_ref[:, 256:512]
    if latent:
        k = _rope(k, cos_ref[...], sa_ref[...], sb_ref[...])
        k_s[seq:seq + PAST, :] = kc_ref[0, 0].astype(BF16)
        _store_v_ext(v_s, seq, vc_ref[0, 0])
    k_s[0:seq, :] = k.astype(BF16)
    _store_v_ext(v_s, 0, u_ref[:, 512:768])
    tq = min(ATT_TQ, seq)
    first = _iota((tq, 64), 1) < DF_HD

    def q_tile(t, carry):
        r0 = pl.multiple_of(t * tq, tq)
        q = u_ref[pl.ds(r0, tq), 0:256]
        if latent:
            q = _rope(q, cos_ref[pl.ds(r0, tq), :], sa_ref[pl.ds(r0, tq), :], sb_ref[pl.ds(r0, tq), :])
        for h in range(N_HEADS):
            qh = q[:, 64 * h:64 * h + 64]
            kh = k_s[:, 64 * h:64 * h + 64]
            q0 = jnp.where(first, qh, 0.0).astype(BF16)
            q1 = jnp.where(first, 0.0, qh).astype(BF16)
            vh = v_s[:, 128 * h:128 * h + 128]
            o = (_pv_normalised(_exp_only(_nt(q0, kh), scale), vh)
                 - lam * _pv_normalised(_exp_only(_nt(q1, kh), scale), vh))
            o_ref[pl.ds(r0, tq), 64 * h:64 * h + 64] = (_rms(o, gs_ref[...]) * (1.0 - lam_init)).astype(BF16)
        return carry

    if latent:
        lax.fori_loop(0, seq // tq, q_tile, 0)
    else:
        q = u_ref[:, 0:256]
        blocks = []
        for h in range(N_HEADS):
            qh = q[:, 64 * h:64 * h + 64]
            kh = k_s[:, 64 * h:64 * h + 64]
            blocks.append(_nt(jnp.where(first, qh, 0.0).astype(BF16), kh))
            blocks.append(_nt(jnp.where(first, 0.0, qh).astype(BF16), kh))
        e = _exp_only(jnp.concatenate(blocks, axis=0), scale)
        for h in range(N_HEADS):
            vh = v_s[:, 128 * h:128 * h + 128]
            o = (_pv_normalised(e[2 * h * seq:(2 * h + 1) * seq], vh)
                 - lam * _pv_normalised(e[(2 * h + 1) * seq:(2 * h + 2) * seq], vh))
            o_ref[:, 64 * h:64 * h + 64] = (_rms(o, gs_ref[...]) * (1.0 - lam_init)).astype(BF16)


def _df(u_df, lam_vec, g_sub, lam_init, rope256=None, k_ctx=None, v_ctx=None, layer=0):
    latent = rope256 is not None
    seq = L_LAT if latent else L_CTX
    nb = B_LAT if latent else B_CTX
    off = R_CTX // L_LAT if latent else 0
    lk = seq + PAST if latent else seq
    const = lambda b: (0, 0)
    in_specs = [pl.BlockSpec((seq, W_DF), lambda b: (off + b, 0)),
                pl.BlockSpec((4, DF_HD), const),
                pl.BlockSpec((1, DF_V), const)]
    args = [u_df, lam_vec, g_sub]
    if latent:
        in_specs += [pl.BlockSpec((seq, 256), const)] * 3
        in_specs += [pl.BlockSpec((1, 1, PAST, 256), lambda b: (b, layer, 0, 0))] * 2
        args += list(rope256) + [k_ctx, v_ctx]
    return pl.pallas_call(
        functools.partial(_df_kernel, latent=latent, seq=seq, lam_init=lam_init),
        out_shape=jax.ShapeDtypeStruct((nb * seq, 256), BF16),
        grid=(nb,),
        in_specs=in_specs,
        out_specs=pl.BlockSpec((seq, 256), lambda b: (b, 0)),
        scratch_shapes=[pltpu.VMEM((lk, 256), BF16), pltpu.VMEM((lk, 512), BF16)],
        compiler_params=_cparams(("parallel",)),
        name="df_lat" if latent else "df_ctx",
    )(*args)


def _softplus(x):
    return jnp.maximum(x, 0.0) + jnp.log1p(jnp.exp(-jnp.abs(x)))


def _ssd_kernel(*refs, latent, seq):
    if latent:
        (u_ref, dtt_ref, cw_ref, cb_ref, dtb_c_ref, dtb_r_ref, a_c_ref, a_r_ref, dvec_ref, gn_ref, h0_ref,
         o_ref, xs_s, bm_s, cm_s, dtc_s, dtr_s, y_s, st_s) = refs
    else:
        (u_ref, dtt_ref, cw_ref, cb_ref, dtb_c_ref, dtb_r_ref, a_c_ref, a_r_ref, dvec_ref, gn_ref,
         o_ref, st_o_ref, xs_s, bm_s, cm_s, dtc_s, dtr_s, y_s, st_s) = refs
    q = SSD_Q
    nchunk = seq // q

    def conv(a, w, b):
        row = _iota(a.shape, 0)
        prv = jnp.where(row == 0, 0.0, pltpu.roll(a, 1, 0))
        nxt = jnp.where(row == seq - 1, 0.0, pltpu.roll(a, seq - 1, 0))
        return _silu(w[0:1] * prv + w[1:2] * a + w[2:3] * nxt + b)

    cw = cw_ref[...]
    cb = cb_ref[...]
    xs_s[...] = conv(u_ref[:, 0:256], cw[:, 0:256], cb[:, 0:256])
    bm_s[...] = conv(u_ref[:, 512:640], cw[:, 256:384], cb[:, 256:384])
    cm_s[...] = conv(u_ref[:, 640:768], cw[:, 384:512], cb[:, 384:512])
    dtc_s[...] = _softplus(u_ref[:, 768:776] + dtb_c_ref[...])
    dtr_s[...] = _softplus(dtt_ref[...] + dtb_r_ref[...])
    eye_n = jnp.where(_iota((SSM_N, SSM_N), 0) == _iota((SSM_N, SSM_N), 1), 1.0, 0.0).astype(BF16)

    def transpose64(a):
        return sum(_nt(eye_n, p) for p in _split3(a))

    hpg = N_HEADS // SSM_GROUPS
    st_s[...] = jnp.zeros(st_s.shape, F32)
    if latent:
        for d in range(2):
            for h in range(N_HEADS):
                g = h // hpg
                st_s[d, 64 * g:64 * g + 64, 64 * h:64 * h + 64] = transpose64(h0_ref[0, 0, d, h])

    ri = _iota((q, q), 0)
    ci = _iota((q, q), 1)
    lower = ri >= ci
    upper = ri <= ci
    a_col = -jnp.exp(a_c_ref[...])
    a_row = -jnp.exp(a_r_ref[...])
    head_of_lane = _iota((1, 256), 1) >> 6
    own_group = (_iota((2 * SSM_N, 256), 0) >> 6) == (_iota((2 * SSM_N, 256), 1) >> 7)

    def chunk(c, d):
        c0 = pl.multiple_of(c * q, q)
        xs = xs_s[pl.ds(c0, q), :]
        bm_b = bm_s[pl.ds(c0, q), :].astype(BF16)
        cm_b = cm_s[pl.ds(c0, q), :].astype(BF16)
        dtc = dtc_s[pl.ds(c0, q), :]
        mask = lower if d == 0 else upper
        tri = jnp.where(mask, 1.0, 0.0).astype(BF16)
        tri_t = jnp.where(upper if d == 0 else lower, 1.0, 0.0).astype(BF16)
        acum_c = sum(_dot(tri, p) for p in _split3(dtc * a_col))
        acum_r = sum(_dot(p, tri_t) for p in _split3(dtr_s[:, pl.ds(c0, q)] * a_row))
        spread = jnp.where(_iota((8, 256), 0) == N_HEADS * d + (_iota((8, 256), 1) >> 6), 1.0, 0.0).astype(BF16)
        a_exp = sum(_dot(p, spread) for p in _split3(acum_c))
        dt_exp = sum(_dot(p, spread) for p in _split3(dtc))
        a_end = a_exp[q - 1:q, :] if d == 0 else a_exp[0:1, :]
        xdt = xs * dt_exp
        xdt_b = xdt.astype(BF16)
        st = st_s[d]
        y = _dot(cm_b, st.astype(BF16)) * jnp.exp(a_exp)
        for g in range(SSM_GROUPS):
            cb_g = _nt(cm_b[:, 64 * g:64 * g + 64], bm_b[:, 64 * g:64 * g + 64])
            for hh in range(hpg):
                h = g * hpg + hh
                j = N_HEADS * d + h
                seg = acum_c[:, j:j + 1] - acum_r[j:j + 1, :]
                decay = jnp.where(mask, jnp.exp(jnp.where(mask, seg, 0.0)), 0.0)
                x_h = jnp.where(head_of_lane == h, xdt_b, jnp.zeros_like(xdt_b))
                y = y + _dot((cb_g * decay).astype(BF16), x_h)
        if d == 0:
            y_s[pl.ds(c0, q), :] = y
        else:
            y_s[pl.ds(c0, q), :] = y_s[pl.ds(c0, q), :] + y
        xw = (xdt * jnp.exp(a_end - a_exp)).astype(BF16)
        upd = _dot(bm_s[pl.ds(c0, q), :].T.astype(BF16), xw)
        st_s[d] = st * jnp.exp(a_end) + jnp.where(own_group, upd, 0.0)

    def fwd(c, carry):
        chunk(c, 0)
        return carry

    def bwd(c, carry):
        chunk(nchunk - 1 - c, 1)
        return carry

    lax.fori_loop(0, nchunk, fwd, 0)
    lax.fori_loop(0, nchunk, bwd, 0)
    y = y_s[...] + dvec_ref[...] * xs_s[...]
    o_ref[...] = _rms(y * _silu(u_ref[:, 256:512]), gn_ref[...]).astype(BF16)
    if not latent:
        for d in range(2):
            for h in range(N_HEADS):
                g = h // hpg
                st_o_ref[0, d, h] = transpose64(st_s[d, 64 * g:64 * g + 64, 64 * h:64 * h + 64])


def _ssd(u_ssm, dt_t, conv_w, conv_b, dt_bias, a_log, d_vec, g_norm, h0=None, layer=0):
    latent = h0 is not None
    seq = L_LAT if latent else L_CTX
    nb = B_LAT if latent else B_CTX
    off = R_CTX // L_LAT if latent else 0
    const = lambda b: (0, 0)
    dtb = dt_bias.reshape(1, 8)
    alg = a_log.reshape(1, 8)
    in_specs = [pl.BlockSpec((seq, W_SSM), lambda b: (off + b, 0)),
                pl.BlockSpec((8, seq), lambda b: (0, off + b)),
                pl.BlockSpec((3, 512), const), pl.BlockSpec((1, 512), const),
                pl.BlockSpec((1, 8), const), pl.BlockSpec((8, 1), const),
                pl.BlockSpec((1, 8), const), pl.BlockSpec((8, 1), const),
                pl.BlockSpec((1, 256), const), pl.BlockSpec((1, 256), const)]
    args = [u_ssm, dt_t, conv_w, conv_b.reshape(1, 512), dtb, dtb.reshape(8, 1), alg, alg.reshape(8, 1),
            d_vec, g_norm]
    out_shape = [jax.ShapeDtypeStruct((nb * seq, 256), BF16)]
    out_specs = [pl.BlockSpec((seq, 256), lambda b: (b, 0))]
    if latent:
        in_specs.append(pl.BlockSpec((1, 1, 2, N_HEADS, SSM_P, SSM_N), lambda b: (b, layer, 0, 0, 0, 0)))
        args.append(h0)
    else:
        out_shape.append(jax.ShapeDtypeStruct((nb, 2, N_HEADS, SSM_P, SSM_N), F32))
        out_specs.append(pl.BlockSpec((1, 2, N_HEADS, SSM_P, SSM_N), lambda b: (b, 0, 0, 0, 0)))
    scratch = [pltpu.VMEM((seq, 256), F32), pltpu.VMEM((seq, 128), F32), pltpu.VMEM((seq, 128), F32),
               pltpu.VMEM((seq, 8), F32), pltpu.VMEM((8, seq), F32), pltpu.VMEM((seq, 256), F32),
               pltpu.VMEM((2, SSM_GROUPS * SSM_N, 256), F32)]
    return pl.pallas_call(
        functools.partial(_ssd_kernel, latent=latent, seq=seq),
        out_shape=out_shape,
        grid=(nb,),
        in_specs=in_specs,
        out_specs=out_specs,
        scratch_shapes=scratch,
        compiler_params=_cparams(("parallel",)),
        name="ssd_lat" if latent else "ssd_ctx",
    )(*args)


def _merge_kernel(xc_ref, xl_ref, mod_ref, c0, l0, c1, l1, c2, l2, c3, l3, gate_ref, wb_ref, wo_ref, g2_ref,
                  wr_ref, xo_ref, h2_ref, aff_ref, afft_ref):
    i = pl.program_id(0)
    acc = None
    for b, (bc, bl) in enumerate(((c0, l0), (c1, l1), (c2, l2), (c3, l3))):
        proj = _dot(_pick_group(i, bc, bl).astype(BF16), wb_ref[b])
        term = gate_ref[:, D * b:D * b + D].astype(F32) * proj
        acc = term if acc is None else acc + term
    x = _pick_group(i, xc_ref, xl_ref) + mod_ref[0, 2:3, :] * _dot(acc.astype(BF16), wo_ref[...])
    xo_ref[...] = x
    h2 = _rms(x, g2_ref[...]) * (1.0 + mod_ref[0, 4:5, :]) + mod_ref[0, 3:4, :]
    hb = h2.astype(BF16)
    h2_ref[...] = hb
    hl = (h2 - hb.astype(F32)).astype(BF16)
    wr = wr_ref[...]
    wh = wr.astype(BF16)
    wl = (wr - wh.astype(F32)).astype(BF16)
    logits = _dot(hb, wh) + _dot(hl, wh) + _dot(hb, wl)
    aff = _softmax_rows(logits)
    aff_ref[...] = aff
    eye = jnp.where(_iota((N_EXPERTS, N_EXPERTS), 0) == _iota((N_EXPERTS, N_EXPERTS), 1), 1.0, 0.0).astype(BF16)
    afft_ref[...] = sum(_nt(eye, p) for p in _split3(aff))


def _merge(x_ctx, x_lat, mod_l, branches, gates, wb, wo, g2, wr):
    const = lambda i: (0, 0)
    row = lambda i: (i, 0)
    ctx_row = lambda i: (_ctx_tile(i), 0)
    lat_row = lambda i: (_lat_tile(i), 0)
    return pl.pallas_call(
        _merge_kernel,
        out_shape=[jax.ShapeDtypeStruct((R_ALL, D), F32), jax.ShapeDtypeStruct((R_ALL, D), BF16),
                   jax.ShapeDtypeStruct((R_ALL, N_EXPERTS), F32), jax.ShapeDtypeStruct((N_EXPERTS, R_ALL), F32)],
        grid=(NT_ALL,),
        in_specs=[pl.BlockSpec((TM, D), ctx_row), pl.BlockSpec((TM, D), lat_row),
                  pl.BlockSpec((1, 6, D), lambda i: (_mod_row(i), 0, 0))]
                 + [pl.BlockSpec((TM, 256), ctx_row), pl.BlockSpec((TM, 256), lat_row)] * 4
                 + [pl.BlockSpec((TM, W_GATE), row),
                    pl.BlockSpec((4, 256, D), lambda i: (0, 0, 0), pipeline_mode=pl.Buffered(1)),
                    pl.BlockSpec((D, D), const, pipeline_mode=pl.Buffered(1)),
                    pl.BlockSpec((1, D), const),
                    pl.BlockSpec((D, N_EXPERTS), const)],
        out_specs=[pl.BlockSpec((TM, D), row), pl.BlockSpec((TM, D), row),
                   pl.BlockSpec((TM, N_EXPERTS), row), pl.BlockSpec((N_EXPERTS, TM), lambda i: (0, i))],
        compiler_params=_cparams(("parallel",)),
        name="merge_router",
    )(x_ctx, x_lat, mod_l, *[a for pair in branches for a in pair], gates, wb, wo, g2, wr)


def _select_kernel(afft_ref, lpos_t_ref, lpos_ref, tstart_ref, cnt_ref, gt_s, eq_s, need_s, carry_s, *, cap):
    t = pl.program_id(0)
    ne = N_EXPERTS

    @pl.when(t == 0)
    def _():
        aff = afft_ref[...]

        def count_ge(v):
            return jnp.sum(jnp.where(aff >= v, 1.0, 0.0), axis=1, keepdims=True)

        def bisect(_, lh):
            lo, hi = lh
            mid = jnp.where(lo > 0.0, jnp.sqrt(lo) * jnp.sqrt(hi), hi * 2.0 ** -32)
            mid = jnp.clip(mid, lo, hi)
            ok = count_ge(mid) >= cap
            return jnp.where(ok, mid, lo), jnp.where(ok, hi, mid)

        _, hi = lax.fori_loop(0, SELECT_BISECT_STEPS, bisect,
                              (jnp.zeros((ne, 1), F32), jnp.full((ne, 1), 2.0, F32)))

        def short(st):
            return jnp.min(st[1]) < cap

        def peel(st):
            bound, cnt = st
            nxt = jnp.max(jnp.where(aff < bound, aff, -1.0), axis=1, keepdims=True)
            upd = cnt < cap
            return jnp.where(upd, nxt, bound), jnp.where(upd, count_ge(nxt), cnt)

        thr, _ = lax.while_loop(short, peel, (hi, count_ge(hi)))
        gt = jnp.where(aff > thr, 1.0, 0.0)
        gt_s[...] = gt
        eq_s[...] = jnp.where(aff == thr, 1.0, 0.0)
        need_col = cap - jnp.sum(gt, axis=1, keepdims=True)
        eye = _iota((ne, ne), 0) == _iota((ne, ne), 1)
        need_s[...] = jnp.sum(jnp.where(eye, need_col, 0.0), axis=0, keepdims=True)
        carry_s[...] = jnp.zeros(carry_s.shape, F32)

    sl = pl.ds(pl.multiple_of(t * TM, TM), TM)
    eye_t = jnp.where(_iota((TM, TM), 0) == _iota((TM, TM), 1), 1.0, 0.0).astype(BF16)
    eye_e = jnp.where(_iota((ne, ne), 0) == _iota((ne, ne), 1), 1.0, 0.0).astype(BF16)
    before = jnp.where(_iota((TM, TM), 0) > _iota((TM, TM), 1), 1.0, 0.0).astype(BF16)
    gtm = _nt(eye_t, gt_s[:, sl].astype(BF16))
    eqm = _nt(eye_t, eq_s[:, sl].astype(BF16))
    eq_seen = carry_s[0:1, :]
    pos0 = carry_s[1:2, :]
    eq_rank = _dot(before, eqm.astype(BF16)) + eq_seen
    sel = jnp.maximum(gtm, eqm * jnp.where(eq_rank < need_s[...], 1.0, 0.0))
    lp = _dot(before, sel.astype(BF16))
    cnt = jnp.sum(sel, axis=0, keepdims=True)
    lpos_ref[...] = jnp.where(sel > 0.0, lp, -1.0)
    lp_t = _nt(eye_e, lp.astype(BF16))
    sel_t = _nt(eye_e, sel.astype(BF16))
    lpos_t_ref[...] = jnp.where(sel_t > 0.0, lp_t, -1.0)
    tstart_ref[0] = pos0.astype(jnp.int32)
    cnt_ref[0] = cnt.astype(jnp.int32)
    carry_s[0:1, :] = eq_seen + jnp.sum(eqm, axis=0, keepdims=True)
    carry_s[1:2, :] = pos0 + cnt


def _select(afft, cap):
    n = afft.shape[1]
    nt = n // TM
    ne = N_EXPERTS
    return pl.pallas_call(
        functools.partial(_select_kernel, cap=cap),
        out_shape=[jax.ShapeDtypeStruct((ne, n), F32), jax.ShapeDtypeStruct((n, ne), F32),
                   jax.ShapeDtypeStruct((nt, 1, ne), jnp.int32), jax.ShapeDtypeStruct((nt, 1, ne), jnp.int32)],
        grid=(nt,),
        in_specs=[pl.BlockSpec((ne, n), lambda t: (0, 0))],
        out_specs=[pl.BlockSpec((ne, TM), lambda t: (0, t)), pl.BlockSpec((TM, ne), lambda t: (t, 0)),
                   pl.BlockSpec((1, 1, ne), lambda t: (t, 0, 0)), pl.BlockSpec((1, 1, ne), lambda t: (t, 0, 0))],
        scratch_shapes=[pltpu.VMEM((ne, n), F32), pltpu.VMEM((ne, n), F32),
                        pltpu.VMEM((1, ne), F32), pltpu.VMEM((8, ne), F32)],
        compiler_params=_cparams(("arbitrary",)),
        name="moe_select",
    )(afft)


def _win_index(idx):
    e = jnp.zeros_like(idx)
    for k in range(1, N_EXPERTS):
        e = e + jnp.where(idx >= k * MOE_WIN, 1, 0)
    return e, idx - e * MOE_WIN


def _gather_consts():
    ne, win, al = N_EXPERTS, MOE_WIN, MOE_ALIGN
    rows = np.arange(ne * win)
    ex = (rows[:, None] // win == np.arange(ne)[None, :]).astype(np.float32)
    rows16 = np.arange(ne * al)
    ex16 = (rows16[:, None] // al == np.arange(ne)[None, :]).astype(np.float32)
    return (jnp.asarray(np.concatenate([ex, ex], axis=1), BF16),
            jnp.asarray(np.broadcast_to((rows % win)[:, None], (ne * win, TM)), F32),
            jnp.asarray(np.concatenate([ex16, ex16], axis=1), BF16),
            jnp.asarray(np.broadcast_to((rows16 % al)[:, None], (ne * al, TM)), F32))


def _gather_kernel(tstart_sm, cnt_sm, rounds_sm, h2_ref, lpos_t_ref, ex_ref, row_ref, ex16_ref, row16_ref,
                   xe_ref, stage, pend, sem, *, nt, cap):
    i = pl.program_id(0)
    slot = lax.rem(i, 2)
    ne, w, win, al = N_EXPERTS, MOE_W, MOE_WIN, MOE_ALIGN

    @pl.when(i == 0)
    def _():
        pend[...] = jnp.zeros(pend.shape, F32)

    def first_slot(e):
        return tstart_sm[i * ne + e]

    def below(e):
        return first_slot(e) & (al - 1)

    lp_t = lpos_t_ref[...]
    lp_t = jnp.where(lp_t < 0.0, -1e6, lp_t)
    sub = _iota((ne, TM), 0)
    shift = jnp.zeros((ne, TM), F32)
    shift16 = jnp.zeros((ne, TM), F32)
    for e in range(ne):
        shift = jnp.where(sub == e, below(e).astype(F32), shift)
        filled = below(e) + cnt_sm[i * ne + e]
        shift16 = jnp.where(sub == e, (below(e) - ((filled >> 4) << 4)).astype(F32), shift16)
    tgt = _dot(ex_ref[...], jnp.concatenate([lp_t, shift], axis=0).astype(BF16))

    def build(k, sl):
        oh = jnp.where(tgt == row_ref[...] + (k * w).astype(F32), 1.0, 0.0).astype(BF16)
        return _dot(oh, h2_ref[...])

    def copy(e, start, sl):
        return pltpu.make_async_copy(
            stage.at[sl, pl.ds(e * win, win)],
            xe_ref.at[e, pl.ds(pl.multiple_of(start, al), win)],
            sem.at[sl])

    def issue(k, sl):
        for e in range(ne):
            copy(e, jnp.minimum(first_slot(e) - below(e) + k * w, cap), sl).start()

    def wait_all(sl):
        for e in range(ne):
            copy(e, 0, sl).wait()

    zero = jnp.int32(0)
    g = build(zero, slot)
    stage[slot] = g.astype(BF16)
    for e in range(ne):
        stage[slot, e * win:e * win + al, :] = (g[e * win:e * win + al] + pend[e]).astype(BF16)

    tgt16 = _dot(ex16_ref[...], jnp.concatenate([lp_t, shift16], axis=0).astype(BF16))
    oh16 = jnp.where(tgt16 == row16_ref[...], 1.0, 0.0).astype(BF16)
    new_pend = _dot(oh16, h2_ref[...])
    for e in range(ne):
        keep = jnp.where(below(e) + cnt_sm[i * ne + e] < al, 1.0, 0.0)
        pend[e] = new_pend[e * al:(e + 1) * al] + keep * pend[e]

    @pl.when(i > 0)
    def _():
        wait_all(1 - slot)

    issue(zero, slot)

    def extra(k, carry):
        wait_all(slot)
        stage[slot] = build(k, slot).astype(BF16)
        issue(k, slot)
        return carry

    lax.fori_loop(1, rounds_sm[i], extra, 0)

    @pl.when(i == nt - 1)
    def _():
        wait_all(slot)
        stage[1 - slot, 0:win, :] = jnp.zeros((win, D), BF16)
        for e in range(ne):
            pltpu.make_async_copy(
                stage.at[1 - slot, pl.ds(0, win)], xe_ref.at[e, pl.ds(cap, win)], sem.at[1 - slot]).start()
        wait_all(1 - slot)


def _gather(h2b, lpos_t, tstart, cnt, rounds, cap, tile_off):
    n = lpos_t.shape[1]
    nt = n // TM
    ne, win = N_EXPERTS, MOE_WIN
    consts = _gather_consts()
    grid_spec = pltpu.PrefetchScalarGridSpec(
        num_scalar_prefetch=3,
        grid=(nt,),
        in_specs=[pl.BlockSpec((TM, D), lambda i, a, b, c: (tile_off + i, 0)),
                  pl.BlockSpec((ne, TM), lambda i, a, b, c: (0, i))]
                 + [pl.BlockSpec(t.shape, lambda i, a, b, c: (0, 0)) for t in consts],
        out_specs=pl.BlockSpec(memory_space=pl.ANY),
        scratch_shapes=[pltpu.VMEM((2, ne * win, D), BF16), pltpu.VMEM((ne, MOE_ALIGN, D), F32),
                        pltpu.SemaphoreType.DMA((2,))],
    )
    return pl.pallas_call(
        functools.partial(_gather_kernel, nt=nt, cap=cap),
        out_shape=jax.ShapeDtypeStruct((ne, cap + win, D), BF16),
        grid_spec=grid_spec,
        compiler_params=_cparams(("arbitrary",)),
        name="moe_gather",
    )(tstart, cnt, rounds, h2b, lpos_t, *consts)


FFN_TILES_CTX = (CAP_FACTOR * R_CTX // N_EXPERTS) // FFN_TM
FFN_TILES_LAT = (CAP_FACTOR * R_LAT // N_EXPERTS) // FFN_TM


def _ffn_kernel(xc_ref, xl_ref, wg_ref, wu_ref, wd_ref, yc_ref, yl_ref, wbuf, wg_s, wu_s, wd_s, sem, *, layer):
    e = pl.program_id(0)
    j = pl.program_id(1)
    slot = lax.rem(e, 2)

    def wcopy(k, w_ref, expert, sl):
        return pltpu.make_async_copy(w_ref.at[layer, expert], wbuf.at[sl, k], sem.at[sl, k])

    def start_weights(expert, sl):
        for k, w_ref in enumerate((wg_ref, wu_ref, wd_ref)):
            wcopy(k, w_ref, expert, sl).start()

    @pl.when(j == 0)
    def _():
        @pl.when(e == 0)
        def _():
            start_weights(e, slot)

        for k, w_ref in enumerate((wg_ref, wu_ref, wd_ref)):
            wcopy(k, w_ref, e, slot).wait()
        wg_s[...] = wbuf[slot, 0].astype(BF16)
        wu_s[...] = wbuf[slot, 1].astype(BF16)
        wd_s[...] = wbuf[slot, 2].astype(BF16)

        @pl.when(e + 1 < N_EXPERTS)
        def _():
            start_weights(e + 1, 1 - slot)

    x = jnp.where(j < FFN_TILES_CTX, xc_ref[0], xl_ref[0])
    hid = (_silu(_dot(x, wg_s[...])) * _dot(x, wu_s[...])).astype(BF16)
    y = _dot(hid, wd_s[...]).astype(BF16)

    @pl.when(j < FFN_TILES_CTX)
    def _():
        yc_ref[0] = y

    @pl.when(j >= FFN_TILES_CTX)
    def _():
        yl_ref[0] = y


def _ffn(xe_ctx, xe_lat, wg, wu, wd, layer):
    ne = N_EXPERTS
    blk = (1, FFN_TM, D)
    wspec = pl.BlockSpec(memory_space=pl.ANY)
    ctx_map = lambda e, j: (e, jnp.minimum(j, FFN_TILES_CTX - 1), 0)
    lat_map = lambda e, j: (e, jnp.maximum(j - FFN_TILES_CTX, 0), 0)
    return pl.pallas_call(
        functools.partial(_ffn_kernel, layer=layer),
        out_shape=[jax.ShapeDtypeStruct((ne, FFN_TILES_CTX * FFN_TM, D), BF16),
                   jax.ShapeDtypeStruct((ne, FFN_TILES_LAT * FFN_TM, D), BF16)],
        grid=(ne, FFN_TILES_CTX + FFN_TILES_LAT),
        in_specs=[pl.BlockSpec(blk, ctx_map), pl.BlockSpec(blk, lat_map), wspec, wspec, wspec],
        out_specs=[pl.BlockSpec(blk, ctx_map), pl.BlockSpec(blk, lat_map)],
        scratch_shapes=[pltpu.VMEM((2, 3, D, D), F32)] + [pltpu.VMEM((D, D), BF16)] * 3
                       + [pltpu.SemaphoreType.DMA((2, 3))],
        compiler_params=_cparams(("arbitrary", "arbitrary")),
        name="moe_ffn",
    )(xe_ctx, xe_lat, wg, wu, wd)


def _combine_kernel(tstart_sm, rounds_sm, lpos_ref, aff_ref, x_ref, mod_ref, fg_ref, ye_ref, o_ref,
                    ybuf, acc_s, sem, *, nt, cap, final):
    i = pl.program_id(0)
    slot = lax.rem(i, COMBINE_BUFS)
    ne, w, win, al = N_EXPERTS, MOE_W, MOE_WIN, MOE_ALIGN
    m = ne * win

    def win_start(tile, e, k):
        first = tstart_sm[tile * ne + e] + k * w
        return jnp.minimum((first >> 4) << 4, cap - win)

    def copy(e, start, sl):
        return pltpu.make_async_copy(
            ye_ref.at[e, pl.ds(pl.multiple_of(start, al), win)],
            ybuf.at[sl, pl.ds(e * win, win)],
            sem.at[sl])

    def fetch(tile, k, sl):
        for e in range(ne):
            copy(e, win_start(tile, e, k), sl).start()

    def wait_all(sl):
        for e in range(ne):
            copy(e, 0, sl).wait()

    zero = jnp.int32(0)

    ahead = COMBINE_BUFS - 1

    @pl.when(i == 0)
    def _():
        for t in range(min(ahead, nt)):
            fetch(jnp.int32(t), zero, jnp.int32(t))

    wait_all(slot)

    @pl.when(i + ahead < nt)
    def _():
        fetch(i + ahead, zero, lax.rem(i + ahead, COMBINE_BUFS))

    e_lane, r_lane = _win_index(_iota((1, m), 1))
    expand = jnp.where(_win_index(_iota((ne, m), 1))[0] == _iota((ne, m), 0), 1.0, 0.0).astype(BF16)
    aff = aff_ref[...]
    lpx = _dot(lpos_ref[...].astype(BF16), expand)
    affx = _dot(aff.astype(BF16), expand)
    rl = r_lane.astype(F32)

    def compute(k, sl):
        shift = jnp.zeros((1, m), F32)
        for e in range(ne):
            sh = (tstart_sm[i * ne + e] - win_start(i, e, k)).astype(F32)
            shift = jnp.where(e_lane == e, sh, shift)
        lo = (k * w).astype(F32)
        hit = jnp.where(lpx + shift == rl, 1.0, 0.0) * jnp.where(lpx >= lo, 1.0, 0.0) * jnp.where(lpx < lo + w, 1.0, 0.0)
        return _dot((hit * affx).astype(BF16), ybuf[sl])

    acc_s[...] = compute(zero, slot)

    def extra(k, carry):
        fetch(i, k, slot)
        wait_all(slot)
        acc_s[...] = acc_s[...] + compute(k, slot)
        return carry

    lax.fori_loop(1, rounds_sm[i], extra, 0)
    x = x_ref[...] + mod_ref[0, 5:6, :] * acc_s[...]
    if final:
        x = _rms(x, fg_ref[...])
    o_ref[...] = x


def _combine(x, mod_l, lpos, aff, ye, tstart, rounds, fg, cap, tile_off, final):
    n = lpos.shape[0]
    nt = n // TM
    ne = N_EXPERTS
    grid_spec = pltpu.PrefetchScalarGridSpec(
        num_scalar_prefetch=2,
        grid=(nt,),
        in_specs=[pl.BlockSpec((TM, ne), lambda i, a, b: (i, 0)),
                  pl.BlockSpec((TM, ne), lambda i, a, b: (tile_off + i, 0)),
                  pl.BlockSpec((TM, D), lambda i, a, b: (tile_off + i, 0)),
                  pl.BlockSpec((1, 6, D), lambda i, a, b: (_mod_row(tile_off + i), 0, 0)),
                  pl.BlockSpec((1, D), lambda i, a, b: (0, 0)),
                  pl.BlockSpec(memory_space=pl.ANY)],
        out_specs=pl.BlockSpec((TM, D), lambda i, a, b: (i, 0)),
        scratch_shapes=[pltpu.VMEM((COMBINE_BUFS, ne * MOE_WIN, D), BF16), pltpu.VMEM((TM, D), F32),
                        pltpu.SemaphoreType.DMA((COMBINE_BUFS,))],
    )
    return pl.pallas_call(
        functools.partial(_combine_kernel, nt=nt, cap=cap, final=final),
        out_shape=jax.ShapeDtypeStruct((n, D), F32),
        grid_spec=grid_spec,
        compiler_params=_cparams(("arbitrary",)),
        name="moe_combine",
    )(tstart, rounds, lpos, aff, x, mod_l, fg, ye)


def _moe_layer(x, mod_l, h2b, aff, afft, wg, wu, wd, fg, layer):
    final = layer == DEPTH - 1
    groups = []
    for n, tile_off, afft_g in ((R_CTX, 0, afft[:, :R_CTX]), (R_LAT, NT_CTX, afft[:, R_CTX:])):
        cap = (CAP_FACTOR * n) // N_EXPERTS
        lpos_t, lpos, tstart, cnt = _select(afft_g, cap)
        tstart = tstart.reshape(-1)
        rounds = jnp.maximum((jnp.max(cnt.reshape(-1, N_EXPERTS), axis=1) + MOE_W - 1) // MOE_W, 1).astype(jnp.int32)
        xe = _gather(h2b, lpos_t, tstart, cnt.reshape(-1), rounds, cap, tile_off)
        groups.append((lpos, tstart, rounds, cap, tile_off, xe))
    ye = _ffn(groups[0][-1], groups[1][-1], wg, wu, wd, layer)
    return [_combine(x, mod_l, lpos, aff, ye_g, tstart, rounds, fg, cap, tile_off, final)
            for (lpos, tstart, rounds, cap, tile_off, _), ye_g in zip(groups, ye)]


def _layout_w_in(w):
    o = np.cumsum([0, 256, 256, 256, 384, 256, 32, 256, 256, 256, 256, 256, 128, 128, 8, 4096])
    wb = w.astype(BF16)
    zeros = lambda n: jnp.zeros((D, n), BF16)
    w_all = jnp.concatenate([wb[:, o[0]:o[5]], zeros(64), wb[:, o[5]:o[6]], zeros(32), wb[:, o[6]:o[14]],
                             zeros(120), wb[:, o[14]:o[15]]], axis=1)
    return w_all, wb[:, o[13]:o[14]].T


def _layout_mla(w_uq, w_ukv):
    uq = w_uq.reshape(MLA_Q_RANK, N_HEADS, MLA_NOPE + MLA_ROPE)
    uq = jnp.pad(uq, ((0, 0), (0, 0), (0, LANES - MLA_NOPE - MLA_ROPE))).reshape(MLA_Q_RANK, N_HEADS * LANES)
    ukv = w_ukv.reshape(MLA_KV_RANK, N_HEADS, MLA_NOPE + MLA_V)
    uk = jnp.pad(ukv[:, :, :MLA_NOPE], ((0, 0), (0, 0), (0, LANES - MLA_NOPE))).reshape(MLA_KV_RANK, N_HEADS * LANES)
    uv = jnp.pad(ukv[:, :, MLA_NOPE:], ((0, 0), (0, 0), (0, LANES - MLA_V))).reshape(MLA_KV_RANK, N_HEADS * LANES)
    return uq.astype(BF16), uk.astype(BF16), uv.astype(BF16)


def kernel(x_prompt, x_sample, cache_na_k, cache_na_v, cache_mla_ckv, cache_mla_krope, cache_df_k, cache_df_v,
           state_ssm, c, c_ctx, mod_w, mod_b, norm1_g, norm2_g, w_in, na_rel_bias, mla_q_norm_g, mla_kv_norm_g,
           mla_w_uq, mla_w_ukv, df_lambda, df_subln_g, ssm_conv_w, ssm_conv_b, ssm_dt_bias, ssm_a_log, ssm_d,
           ssm_norm_g, w_branch, w_out, router_w, exp_w_gate, exp_w_up, exp_w_down, final_norm_g):
    x_ctx = x_prompt.reshape(R_CTX, D)
    x_lat = x_sample.reshape(R_LAT, D)
    cvec = jnp.concatenate([c_ctx[None, :], c, jnp.zeros((16 - 1 - B_LAT, D), F32)], axis=0)
    mod = _modulation(cvec, mod_w, mod_b).reshape(DEPTH, 16, 6, D)

    cos32, sa32, sb32 = _rope_tables32()
    pad128 = lambda a, fill: np.pad(a, ((0, 0), (64, 32)), constant_values=fill)
    rope128 = (pad128(cos32, 1.0), pad128(sa32, 0.0), pad128(sb32, 0.0))
    rope256 = tuple(np.tile(a, (1, 8)) for a in (cos32, sa32, sb32))
    fg = final_norm_g.reshape(1, D)
    cna_k = cache_na_k.reshape(B_LAT, DEPTH, PAST, 256)
    cna_v = cache_na_v.reshape(B_LAT, DEPTH, PAST, 256)
    cdf_k = cache_df_k.reshape(B_LAT, DEPTH, PAST, 256)
    cdf_v = cache_df_v.reshape(B_LAT, DEPTH, PAST, 256)

    outs = {k: [] for k in ("na_k", "na_v", "ckv", "krope", "df_k", "df_v", "ssm")}
    for l in range(DEPTH):
        mod_l = mod[l]
        (u_na, u_mla, u_df, u_ssm, gates, dt_t, na_k, na_v, df_k, df_v, krope) = _in_proj(
            x_ctx, x_lat, mod_l, norm1_g[l].reshape(1, D), *_layout_w_in(w_in[l]))
        bias = _na_bias_table(na_rel_bias[l])
        br_na = (_attn_ctx(u_na), _na_lat(u_na, cna_k, cna_v, bias, l))
        wuq, wuk, wuv = _layout_mla(mla_w_uq[l], mla_w_ukv[l])
        gq = mla_q_norm_g[l].reshape(1, MLA_Q_RANK)
        gkv = mla_kv_norm_g[l].reshape(1, MLA_KV_RANK)
        mla_c, ckv_new = _mla(u_mla, gq, gkv, wuq, wuk, wuv)
        kr_ctx = jnp.pad(cache_mla_krope[:, l], ((0, 0), (0, 0), (64, 32)))
        (mla_l,) = _mla(u_mla, gq, gkv, wuq, wuk, wuv, rope128, cache_mla_ckv, kr_ctx, l)
        lam_init = 0.8 - 0.6 * math.exp(-0.3 * l)
        gs = df_subln_g[l].reshape(1, DF_V)
        br_df = (_df(u_df, df_lambda[l], gs, lam_init),
                 _df(u_df, df_lambda[l], gs, lam_init, rope256, cdf_k, cdf_v, l))
        d_vec = jnp.repeat(ssm_d[l], SSM_P).reshape(1, 256)
        gn = ssm_norm_g[l].reshape(1, 256)
        ssm_c, st_new = _ssd(u_ssm, dt_t, ssm_conv_w[l], ssm_conv_b[l], ssm_dt_bias[l], ssm_a_log[l], d_vec, gn)
        (ssm_l,) = _ssd(u_ssm, dt_t, ssm_conv_w[l], ssm_conv_b[l], ssm_dt_bias[l], ssm_a_log[l], d_vec, gn,
                        state_ssm, l)
        x_mid, h2b, aff, afft = _merge(x_ctx, x_lat, mod_l, (br_na, (mla_c, mla_l), br_df, (ssm_c, ssm_l)), gates,
                                       w_branch[l].astype(BF16), w_out[l].astype(BF16),
                                       norm2_g[l].reshape(1, D), router_w[l])
        x_ctx, x_lat = _moe_layer(x_mid, mod_l, h2b, aff, afft, exp_w_gate, exp_w_up, exp_w_down, fg, l)
        outs["na_k"].append(na_k.reshape(B_CTX, L_CTX, N_HEADS, 64))
        outs["na_v"].append(na_v.reshape(B_CTX, L_CTX, N_HEADS, 64))
        outs["ckv"].append(ckv_new.reshape(B_CTX, L_CTX, MLA_KV_RANK))
        outs["krope"].append(krope.reshape(B_CTX, L_CTX, MLA_ROPE))
        outs["df_k"].append(df_k.reshape(B_CTX, L_CTX, N_HEADS, 2, DF_HD))
        outs["df_v"].append(df_v.reshape(B_CTX, L_CTX, N_HEADS, DF_V))
        outs["ssm"].append(st_new)
    stack = lambda k: jnp.stack(outs[k], axis=1)
    return (x_ctx.reshape(B_CTX, L_CTX, D), x_lat.reshape(B_LAT, L_LAT, D), stack("na_k"), stack("na_v"),
            stack("ckv"), stack("krope"), stack("df_k"), stack("df_v"), stack("ssm"))
```

```python
import functools
import math

import numpy as np
import jax
import jax.numpy as jnp
from jax import lax
from jax.experimental import pallas as pl
from jax.experimental.pallas import tpu as pltpu

F32 = jnp.float32
BF16 = jnp.bfloat16

D = 1024
B_CTX, L_CTX = 32, 256
B_LAT, L_LAT = 8, 2048
PAST = 256
DEPTH = 2
GRID_W = 64
EPS = 1e-6
ROPE_BASE = 10000.0
N_HEADS = 4
NA_KH, NA_KW = 8, 16
MLA_NOPE, MLA_ROPE, MLA_V = 64, 32, 64
MLA_Q_RANK, MLA_KV_RANK = 384, 256
DF_HD, DF_V = 32, 64
SSM_P, SSM_N, SSM_GROUPS = 64, 64, 2
N_EXPERTS = 16
CAP_FACTOR = 2

LANES = 128
SUBLANES = 8
TM = 256
R_CTX = B_CTX * L_CTX
R_LAT = B_LAT * L_LAT
R_ALL = R_CTX + R_LAT
NT_CTX = R_CTX // TM
NT_LAT = R_LAT // TM
NT_ALL = R_ALL // TM
TILES_PER_LAT_BATCH = L_LAT // TM
TQ = 256
ATT_TQ = 512
MLA_TQ = 1024
SSD_Q = 256
MOE_W = 64
MOE_ALIGN = 16
MOE_WIN = MOE_W + MOE_ALIGN
COMBINE_BUFS = 3
SELECT_BISECT_STEPS = 40
FFN_TM = 512
NA_QROWS = 4
NA_KROWS = NA_QROWS + NA_KH - 1
NA_NK = NA_KROWS * GRID_W
VMEM_LIMIT = 56 * 1024 * 1024


def _cparams(sem):
    return pltpu.CompilerParams(dimension_semantics=sem, vmem_limit_bytes=VMEM_LIMIT)


def _nt(a, b):
    return lax.dot_general(a, b, (((1,), (1,)), ((), ())), preferred_element_type=F32)


def _dot(a, b):
    return jnp.dot(a, b, preferred_element_type=F32)


def _rms(x, g):
    return x * lax.rsqrt(jnp.mean(x * x, axis=-1, keepdims=True) + EPS) * g


def _silu(x):
    return x * jax.nn.sigmoid(x)


def _softmax_rows(s):
    m = jnp.max(s, axis=-1, keepdims=True)
    e = jnp.exp(s - m)
    return e * (1.0 / jnp.sum(e, axis=-1, keepdims=True))


LOG2E = math.log2(math.e)


def _exp_rows(s, scale):
    m = jnp.max(s, axis=-1, keepdims=True)
    e = jnp.exp2((s - m) * (scale * LOG2E))
    return e, jnp.sum(e, axis=-1, keepdims=True)


def _exp_only(s, scale):
    return jnp.exp2((s - jnp.max(s, axis=-1, keepdims=True)) * (scale * LOG2E))


def _pv_normalised(e, v_ext):
    o = _dot(e.astype(BF16), v_ext)
    return o[:, 0:64] * (1.0 / o[:, 64:65])


def _store_v_ext(v_s, row0, v):
    rows = v.shape[0]
    lane = _iota((rows, LANES), 1)
    ones_col = jnp.where(lane == 64, 1.0, 0.0)
    for pair in range(2):
        blk = v[:, LANES * pair:LANES * pair + LANES]
        even = jnp.where(lane < 64, blk, ones_col)
        odd = jnp.where(lane < 64, pltpu.roll(blk, 64, 1), ones_col)
        v_s[row0:row0 + rows, 2 * LANES * pair:2 * LANES * pair + LANES] = even.astype(BF16)
        v_s[row0:row0 + rows, 2 * LANES * pair + LANES:2 * LANES * pair + 2 * LANES] = odd.astype(BF16)


def _split3(a):
    a1 = a.astype(BF16)
    r1 = a - a1.astype(F32)
    a2 = r1.astype(BF16)
    a3 = (r1 - a2.astype(F32)).astype(BF16)
    return a1, a2, a3


def _iota(shape, dim):
    return lax.broadcasted_iota(jnp.int32, shape, dim)


def _mod_row(i):
    return jnp.where(i < NT_CTX, 0, 1 + (i - NT_CTX) // TILES_PER_LAT_BATCH)


MOD_TN = 1536


def _mod_kernel(c_ref, w_ref, b_ref, o_ref):
    s = _silu(c_ref[...]).astype(BF16)
    o_ref[0] = _dot(s, w_ref[0].astype(BF16)) + b_ref[0]


def _modulation(cvec, mod_w, mod_b):
    n = 6 * D
    return pl.pallas_call(
        _mod_kernel,
        out_shape=jax.ShapeDtypeStruct((DEPTH, 16, n), F32),
        grid=(DEPTH, n // MOD_TN),
        in_specs=[pl.BlockSpec((16, D), lambda l, j: (0, 0)),
                  pl.BlockSpec((1, D, MOD_TN), lambda l, j: (l, 0, j)),
                  pl.BlockSpec((1, 1, MOD_TN), lambda l, j: (l, 0, j))],
        out_specs=pl.BlockSpec((1, 16, MOD_TN), lambda l, j: (l, 0, j)),
        compiler_params=_cparams(("parallel", "parallel")),
        name="modulation",
    )(cvec, mod_w, mod_b.reshape(DEPTH, 1, n))


W_NA, W_MLA, W_DF, W_SSM, W_GATE = 768, 768, 768, 896, 4 * D


W_IN_COLS = np.cumsum([0, W_NA, W_MLA, W_DF, W_SSM, W_GATE])
W_IN_ALL = int(W_IN_COLS[-1])


def _ctx_tile(i):
    return jnp.minimum(i, NT_CTX - 1)


def _lat_tile(i):
    return jnp.maximum(i - NT_CTX, 0)


def _pick_group(i, ctx_ref, lat_ref):
    return jnp.where(i < NT_CTX, ctx_ref[...], lat_ref[...])


def _in_kernel(xc_ref, xl_ref, mod_ref, g_ref, w_ref, wdt_t,
               ona, omla, odf, ossm, ogate, odt_t, onak, onav, odfk, odfv, okr):
    i = pl.program_id(0)
    x = _pick_group(i, xc_ref, xl_ref)
    h = _rms(x, g_ref[...]) * (1.0 + mod_ref[0, 1:2, :]) + mod_ref[0, 0:1, :]
    hb = h.astype(BF16)
    c = W_IN_COLS
    una = _dot(hb, w_ref[:, c[0]:c[1]])
    umla = _dot(hb, w_ref[:, c[1]:c[2]])
    udf = _dot(hb, w_ref[:, c[2]:c[3]])
    ona[...] = una
    omla[...] = umla
    odf[...] = udf
    ossm[...] = _dot(hb, w_ref[:, c[3]:c[4]])
    ogate[...] = jax.nn.sigmoid(_dot(hb, w_ref[:, c[4]:c[5]])).astype(BF16)
    odt_t[...] = _nt(wdt_t[...], hb)

    @pl.when(i < NT_CTX)
    def _():
        onak[...] = una[:, 256:512]
        onav[...] = una[:, 512:768]
        odfk[...] = udf[:, 256:512]
        odfv[...] = udf[:, 512:768]
        okr[...] = umla[:, 704:736]


def _in_proj(x_ctx, x_lat, mod_l, norm_g, w_all, w_dt_t):
    widths = (W_NA, W_MLA, W_DF, W_SSM)
    const = lambda i: (0, 0)
    row = lambda i: (i, 0)
    ctx_row = lambda i: (_ctx_tile(i), 0)
    out_shape = [jax.ShapeDtypeStruct((R_ALL, w), F32) for w in widths]
    out_shape += [jax.ShapeDtypeStruct((R_ALL, W_GATE), BF16), jax.ShapeDtypeStruct((8, R_ALL), F32)]
    out_shape += [jax.ShapeDtypeStruct((R_CTX, 256), F32)] * 4 + [jax.ShapeDtypeStruct((R_CTX, MLA_ROPE), F32)]
    out_specs = [pl.BlockSpec((TM, w), row) for w in widths]
    out_specs += [pl.BlockSpec((TM, W_GATE), row), pl.BlockSpec((8, TM), lambda i: (0, i))]
    out_specs += [pl.BlockSpec((TM, 256), ctx_row)] * 4 + [pl.BlockSpec((TM, MLA_ROPE), ctx_row)]
    return pl.pallas_call(
        _in_kernel,
        out_shape=out_shape,
        grid=(NT_ALL,),
        in_specs=[pl.BlockSpec((TM, D), ctx_row),
                  pl.BlockSpec((TM, D), lambda i: (_lat_tile(i), 0)),
                  pl.BlockSpec((1, 6, D), lambda i: (_mod_row(i), 0, 0)),
                  pl.BlockSpec((1, D), const),
                  pl.BlockSpec((D, W_IN_ALL), const, pipeline_mode=pl.Buffered(1)),
                  pl.BlockSpec((8, D), const, pipeline_mode=pl.Buffered(1))],
        out_specs=out_specs,
        compiler_params=_cparams(("arbitrary",)),
        name="in_proj",
    )(x_ctx, x_lat, mod_l, norm_g, w_all, w_dt_t)


def _rope(x, cos, sin_a, sin_b):
    n = x.shape[-1]
    nxt = pltpu.roll(x, n - 1, 1)
    prv = pltpu.roll(x, 1, 1)
    return x * cos + nxt * sin_a + prv * sin_b


def _rope_tables32():
    t = np.arange(L_LAT)
    quarter = 8
    inv = ROPE_BASE ** (-np.arange(quarter, dtype=np.float64) / quarter)
    rows = (t // GRID_W).astype(np.float64)[:, None]
    cols = (t % GRID_W).astype(np.float64)[:, None]
    ang = np.concatenate([rows * inv, cols * inv], axis=-1)
    cos = np.repeat(np.cos(ang), 2, axis=-1)
    sin = np.repeat(np.sin(ang), 2, axis=-1)
    even = (np.arange(32) % 2 == 0)[None, :]
    sin_a = np.where(even, -sin, 0.0)
    sin_b = np.where(even, 0.0, sin)
    return tuple(np.asarray(a, np.float32) for a in (cos, sin_a, sin_b))


def _attn_ctx_kernel(u_ref, o_ref, v_s):
    scale = 64 ** -0.5
    _store_v_ext(v_s, 0, u_ref[:, 512:768])
    s = jnp.concatenate([_nt(u_ref[:, 64 * h:64 * h + 64].astype(BF16),
                             u_ref[:, 256 + 64 * h:256 + 64 * h + 64].astype(BF16)) for h in range(N_HEADS)], axis=0)
    e = _exp_only(s, scale)
    for h in range(N_HEADS):
        o_ref[:, 64 * h:64 * h + 64] = _pv_normalised(e[L_CTX * h:L_CTX * h + L_CTX],
                                                      v_s[:, 128 * h:128 * h + 128]).astype(BF16)


def _attn_ctx(u_na):
    return pl.pallas_call(
        _attn_ctx_kernel,
        out_shape=jax.ShapeDtypeStruct((R_CTX, 256), BF16),
        grid=(B_CTX,),
        in_specs=[pl.BlockSpec((L_CTX, W_NA), lambda b: (b, 0))],
        out_specs=pl.BlockSpec((L_CTX, 256), lambda b: (b, 0)),
        scratch_shapes=[pltpu.VMEM((L_CTX, 4 * LANES), BF16)],
        compiler_params=_cparams(("parallel",)),
        name="na_ctx",
    )(u_na)


NA_ROWS = L_LAT // GRID_W
NA_NQT = NA_ROWS // NA_QROWS


def _na_pattern(qt):
    return jnp.where(qt == 0, 0, jnp.where(qt == NA_NQT - 1, 2, 1))


def _na_pattern_offsets(p, a):
    q_off = jnp.where(p == 0, 0, jnp.where(p == 1, NA_KH // 2, NA_KROWS - NA_QROWS))
    rs_rel = jnp.where(p == 0, 0, jnp.where(p == 1, a, NA_KROWS - NA_KH))
    return q_off, rs_rel


def _check_na_patterns():
    for qt in range(NA_NQT):
        ks = int(np.clip(NA_QROWS * qt - NA_KH // 2, 0, NA_ROWS - NA_KROWS))
        p = 0 if qt == 0 else (2 if qt == NA_NQT - 1 else 1)
        for a in range(NA_QROWS):
            r = NA_QROWS * qt + a
            rs = int(np.clip(r - NA_KH // 2, 0, NA_ROWS - NA_KH))
            q_off = (0, NA_KH // 2, NA_KROWS - NA_QROWS)[p]
            rs_rel = (0, a, NA_KROWS - NA_KH)[p]
            assert NA_QROWS * qt - ks == q_off and rs - ks == rs_rel and rs + NA_KH <= ks + NA_KROWS


_check_na_patterns()


def _na_bias_kernel(rb_ref, o_ref):
    p = pl.program_id(0)
    qc = _iota((GRID_W, GRID_W), 0)
    kc = _iota((GRID_W, GRID_W), 1)
    cs = jnp.clip(qc - NA_KW // 2, 0, GRID_W - NA_KW)
    col_ok = (kc >= cs) & (kc < cs + NA_KW)
    for a in range(NA_QROWS):
        q_off, rs_rel = _na_pattern_offsets(p, a)
        for b in range(NA_KROWS):
            row_ok = (b >= rs_rel) & (b < rs_rel + NA_KH)
            dr = jnp.clip(b - q_off - a + NA_KH - 1, 0, 2 * NA_KH - 2)
            v = jnp.broadcast_to(rb_ref[0, pl.ds(dr, 1), :], (GRID_W, LANES))
            t = pltpu.roll(v, LANES - (NA_KW - 1), 1, stride=1, stride_axis=0)[:, :GRID_W]
            o_ref[0, 0, GRID_W * a:GRID_W * a + GRID_W, GRID_W * b:GRID_W * b + GRID_W] = jnp.where(
                col_ok & row_ok, t * LOG2E, -1e30)


def _na_bias_table(rel_bias):
    rb = jnp.pad(rel_bias.astype(F32), ((0, 0), (0, 0), (0, LANES - (2 * NA_KW - 1))))
    return pl.pallas_call(
        _na_bias_kernel,
        out_shape=jax.ShapeDtypeStruct((3, N_HEADS, TQ, NA_NK), F32),
        grid=(3, N_HEADS),
        in_specs=[pl.BlockSpec((1, 2 * NA_KH - 1, LANES), lambda p, h: (h, 0, 0))],
        out_specs=pl.BlockSpec((1, 1, TQ, NA_NK), lambda p, h: (p, h, 0, 0)),
        compiler_params=_cparams(("parallel", "parallel")),
        name="na_bias",
    )(rb)


def _na_lat_kernel(u_ref, kc_ref, vc_ref, bias_ref, o_ref, v_s, vc_s):
    qt = pl.program_id(1)
    c = 64 ** -0.5 * LOG2E
    ks = jnp.clip(NA_QROWS * qt - NA_KH // 2, 0, NA_ROWS - NA_KROWS)
    kstart = pl.multiple_of(ks * GRID_W, GRID_W)
    qstart = pl.multiple_of(qt * TQ, TQ)
    _store_v_ext(v_s, 0, u_ref[pl.ds(kstart, NA_NK), 512:768])
    _store_v_ext(vc_s, 0, vc_ref[0, 0])
    for h in range(N_HEADS):
        q = u_ref[pl.ds(qstart, TQ), 64 * h:64 * h + 64].astype(BF16)
        k = u_ref[pl.ds(kstart, NA_NK), 256 + 64 * h:256 + 64 * h + 64].astype(BF16)
        kc = kc_ref[0, 0, :, 64 * h:64 * h + 64].astype(BF16)
        t_loc = _nt(q, k) * c + bias_ref[0, h]
        t_ctx = _nt(q, kc) * c
        m = jnp.maximum(jnp.max(t_loc, axis=-1, keepdims=True), jnp.max(t_ctx, axis=-1, keepdims=True))
        e_loc = jnp.exp2(t_loc - m).astype(BF16)
        e_ctx = jnp.exp2(t_ctx - m).astype(BF16)
        o = _dot(e_loc, v_s[:, 128 * h:128 * h + 128]) + _dot(e_ctx, vc_s[:, 128 * h:128 * h + 128])
        o_ref[:, 64 * h:64 * h + 64] = (o[:, 0:64] * (1.0 / o[:, 64:65])).astype(BF16)


def _na_lat(u_na, k_ctx, v_ctx, bias, layer):
    nqt = L_LAT // TQ
    cache = pl.BlockSpec((1, 1, PAST, 256), lambda b, t: (b, layer, 0, 0))
    return pl.pallas_call(
        _na_lat_kernel,
        out_shape=jax.ShapeDtypeStruct((R_LAT, 256), BF16),
        grid=(B_LAT, nqt),
        in_specs=[pl.BlockSpec((L_LAT, W_NA), lambda b, t: (R_CTX // L_LAT + b, 0)),
                  cache, cache,
                  pl.BlockSpec((1, N_HEADS, TQ, NA_NK), lambda b, t: (_na_pattern(t), 0, 0, 0))],
        out_specs=pl.BlockSpec((TQ, 256), lambda b, t: (b * nqt + t, 0)),
        scratch_shapes=[pltpu.VMEM((NA_NK, 4 * LANES), BF16), pltpu.VMEM((PAST, 4 * LANES), BF16)],
        compiler_params=_cparams(("parallel", "arbitrary")),
        name="na_lat",
    )(u_na, k_ctx, v_ctx, bias)


def _mla_kernel(*refs, latent, seq):
    if latent:
        (u_ref, gq_ref, gkv_ref, wuq_ref, wuk_ref, wuv_ref, cos_ref, sa_ref, sb_ref,
         ckv_c_ref, kr_c_ref, o_ref, k_s, v_s) = refs
    else:
        (u_ref, gq_ref, gkv_ref, wuq_ref, wuk_ref, wuv_ref, o_ref, ckv_o_ref, k_s, v_s) = refs
    scale = (MLA_NOPE + MLA_ROPE) ** -0.5
    ckv = _rms(u_ref[:, 384:640], gkv_ref[...])
    kr = u_ref[:, 640:768]
    if latent:
        kr = _rope(kr, cos_ref[...], sa_ref[...], sb_ref[...])
    else:
        ckv_o_ref[...] = ckv
    ckv_b = ckv.astype(BF16)
    for h in range(N_HEADS):
        k_s[0:seq, 128 * h:128 * h + 128] = (_dot(ckv_b, wuk_ref[:, 128 * h:128 * h + 128]) + kr).astype(BF16)
    ones_col = jnp.where((_iota((1, 4 * LANES), 1) & (LANES - 1)) == MLA_V, 1.0, 0.0)
    v_s[0:seq, :] = (_dot(ckv_b, wuv_ref[...]) + ones_col).astype(BF16)
    if latent:
        cc = ckv_c_ref[0, 0].astype(BF16)
        krc = kr_c_ref[0]
        for h in range(N_HEADS):
            k_s[seq:seq + PAST, 128 * h:128 * h + 128] = (
                _dot(cc, wuk_ref[:, 128 * h:128 * h + 128]) + krc).astype(BF16)
        v_s[seq:seq + PAST, :] = (_dot(cc, wuv_ref[...]) + ones_col).astype(BF16)

    tq = min(MLA_TQ, seq)

    def q_tile(t, carry):
        r0 = pl.multiple_of(t * tq, tq)
        cq = _rms(u_ref[pl.ds(r0, tq), 0:384], gq_ref[...]).astype(BF16)
        for h in range(N_HEADS):
            q = _dot(cq, wuq_ref[:, 128 * h:128 * h + 128])
            q = _rope(q, cos_ref[pl.ds(r0, tq), :], sa_ref[pl.ds(r0, tq), :], sb_ref[pl.ds(r0, tq), :])
            e = _exp_only(_nt(q.astype(BF16), k_s[:, 128 * h:128 * h + 128]), scale)
            o_ref[pl.ds(r0, tq), 64 * h:64 * h + 64] = _pv_normalised(e, v_s[:, 128 * h:128 * h + 128]).astype(BF16)
        return carry

    if latent:
        lax.fori_loop(0, seq // tq, q_tile, 0)
    else:
        cq = _rms(u_ref[:, 0:384], gq_ref[...]).astype(BF16)
        s = jnp.concatenate([_nt(_dot(cq, wuq_ref[:, 128 * h:128 * h + 128]).astype(BF16),
                                 k_s[:, 128 * h:128 * h + 128]) for h in range(N_HEADS)], axis=0)
        e = _exp_only(s, scale)
        for h in range(N_HEADS):
            o_ref[:, 64 * h:64 * h + 64] = _pv_normalised(e[seq * h:seq * h + seq],
                                                          v_s[:, 128 * h:128 * h + 128]).astype(BF16)


def _mla(u_mla, gq, gkv, wuq, wuk, wuv, rope128=None, ckv_ctx=None, kr_ctx=None, layer=0):
    latent = rope128 is not None
    seq = L_LAT if latent else L_CTX
    nb = B_LAT if latent else B_CTX
    off = R_CTX // L_LAT if latent else 0
    lk = seq + PAST if latent else seq
    const = lambda b: (0, 0)
    in_specs = [pl.BlockSpec((seq, W_MLA), lambda b: (off + b, 0)),
                pl.BlockSpec((1, MLA_Q_RANK), const),
                pl.BlockSpec((1, MLA_KV_RANK), const),
                pl.BlockSpec((MLA_Q_RANK, 512), const),
                pl.BlockSpec((MLA_KV_RANK, 512), const),
                pl.BlockSpec((MLA_KV_RANK, 512), const)]
    args = [u_mla, gq, gkv, wuq, wuk, wuv]
    out_shape = [jax.ShapeDtypeStruct((nb * seq, 256), BF16)]
    out_specs = [pl.BlockSpec((seq, 256), lambda b: (b, 0))]
    if latent:
        in_specs += [pl.BlockSpec((seq, LANES), const)] * 3
        in_specs += [pl.BlockSpec((1, 1, PAST, MLA_KV_RANK), lambda b: (b, layer, 0, 0)),
                     pl.BlockSpec((1, PAST, LANES), lambda b: (b, 0, 0))]
        args += list(rope128) + [ckv_ctx, kr_ctx]
    else:
        out_shape.append(jax.ShapeDtypeStruct((nb * seq, MLA_KV_RANK), F32))
        out_specs.append(pl.BlockSpec((seq, MLA_KV_RANK), lambda b: (b, 0)))
    return pl.pallas_call(
        functools.partial(_mla_kernel, latent=latent, seq=seq),
        out_shape=out_shape,
        grid=(nb,),
        in_specs=in_specs,
        out_specs=out_specs,
        scratch_shapes=[pltpu.VMEM((lk, 512), BF16), pltpu.VMEM((lk, 512), BF16)],
        compiler_params=_cparams(("parallel",)),
        name="mla_lat" if latent else "mla_ctx",
    )(*args)


def _df_kernel(*refs, latent, seq, lam_init):
    if latent:
        (u_ref, lv_ref, gs_ref, cos_ref, sa_ref, sb_ref, kc_ref, vc_ref, o_ref, k_s, v_s) = refs
    else:
        (u_ref, lv_ref, gs_ref, o_ref, k_s, v_s) = refs
    scale = DF_HD ** -0.5
    lv = lv_ref[...]
    lam = (jnp.exp(jnp.sum(lv[0:1] * lv[1:2], axis=1, keepdims=True))
           - jnp.exp(jnp.sum(lv[2:3] * lv[3:4], axis=1, keepdims=True)) + lam_init)
    k = u_ref[:, 256:512]
    if latent:
        k = _rope(k, cos_ref[...], sa_ref[...], sb_ref[...])
        k_s[seq:seq + PAST, :] = kc_ref[0, 0].astype(BF16)
        _store_v_ext(v_s, seq, vc_ref[0, 0])
    k_s[0:seq, :] = k.astype(BF16)
    _store_v_ext(v_s, 0, u_ref[:, 512:768])
    tq = min(ATT_TQ, seq)
    first = _iota((tq, 64), 1) < DF_HD

    def q_tile(t, carry):
        r0 = pl.multiple_of(t * tq, tq)
        q = u_ref[pl.ds(r0, tq), 0:256]
        if latent:
            q = _rope(q, cos_ref[pl.ds(r0, tq), :], sa_ref[pl.ds(r0, tq), :], sb_ref[pl.ds(r0, tq), :])
        for h in range(N_HEADS):
            qh = q[:, 64 * h:64 * h + 64]
            kh = k_s[:, 64 * h:64 * h + 64]
            q0 = jnp.where(first, qh, 0.0).astype(BF16)
            q1 = jnp.where(first, 0.0, qh).astype(BF16)
            vh = v_s[:, 128 * h:128 * h + 128]
            o = (_pv_normalised(_exp_only(_nt(q0, kh), scale), vh)
                 - lam * _pv_normalised(_exp_only(_nt(q1, kh), scale), vh))
            o_ref[pl.ds(r0, tq), 64 * h:64 * h + 64] = (_rms(o, gs_ref[...]) * (1.0 - lam_init)).astype(BF16)
        return carry

    if latent:
        lax.fori_loop(0, seq // tq, q_tile, 0)
    else:
        q = u_ref[:, 0:256]
        blocks = []
        for h in range(N_HEADS):
            qh = q[:, 64 * h:64 * h + 64]
            kh = k_s[:, 64 * h:64 * h + 64]
            blocks.append(_nt(jnp.where(first, qh, 0.0).astype(BF16), kh))
            blocks.append(_nt(jnp.where(first, 0.0, qh).astype(BF16), kh))
        e = _exp_only(jnp.concatenate(blocks, axis=0), scale)
        for h in range(N_HEADS):
            vh = v_s[:, 128 * h:128 * h + 128]
            o = (_pv_normalised(e[2 * h * seq:(2 * h + 1) * seq], vh)
                 - lam * _pv_normalised(e[(2 * h + 1) * seq:(2 * h + 2) * seq], vh))
            o_ref[:, 64 * h:64 * h + 64] = (_rms(o, gs_ref[...]) * (1.0 - lam_init)).astype(BF16)


def _df(u_df, lam_vec, g_sub, lam_init, rope256=None, k_ctx=None, v_ctx=None, layer=0):
    latent = rope256 is not None
    seq = L_LAT if latent else L_CTX
    nb = B_LAT if latent else B_CTX
    off = R_CTX // L_LAT if latent else 0
    lk = seq + PAST if latent else seq
    const = lambda b: (0, 0)
    in_specs = [pl.BlockSpec((seq, W_DF), lambda b: (off + b, 0)),
                pl.BlockSpec((4, DF_HD), const),
                pl.BlockSpec((1, DF_V), const)]
    args = [u_df, lam_vec, g_sub]
    if latent:
        in_specs += [pl.BlockSpec((seq, 256), const)] * 3
        in_specs += [pl.BlockSpec((1, 1, PAST, 256), lambda b: (b, layer, 0, 0))] * 2
        args += list(rope256) + [k_ctx, v_ctx]
    return pl.pallas_call(
        functools.partial(_df_kernel, latent=latent, seq=seq, lam_init=lam_init),
        out_shape=jax.ShapeDtypeStruct((nb * seq, 256), BF16),
        grid=(nb,),
        in_specs=in_specs,
        out_specs=pl.BlockSpec((seq, 256), lambda b: (b, 0)),
        scratch_shapes=[pltpu.VMEM((lk, 256), BF16), pltpu.VMEM((lk, 512), BF16)],
        compiler_params=_cparams(("parallel",)),
        name="df_lat" if latent else "df_ctx",
    )(*args)


def _softplus(x):
    return jnp.maximum(x, 0.0) + jnp.log1p(jnp.exp(-jnp.abs(x)))


def _ssd_kernel(*refs, latent, seq):
    if latent:
        (u_ref, dtt_ref, cw_ref, cb_ref, dtb_c_ref, dtb_r_ref, a_c_ref, a_r_ref, dvec_ref, gn_ref, h0_ref,
         o_ref, xs_s, bm_s, cm_s, dtc_s, dtr_s, y_s, st_s) = refs
    else:
        (u_ref, dtt_ref, cw_ref, cb_ref, dtb_c_ref, dtb_r_ref, a_c_ref, a_r_ref, dvec_ref, gn_ref,
         o_ref, st_o_ref, xs_s, bm_s, cm_s, dtc_s, dtr_s, y_s, st_s) = refs
    q = SSD_Q
    nchunk = seq // q

    def conv(a, w, b):
        row = _iota(a.shape, 0)
        prv = jnp.where(row == 0, 0.0, pltpu.roll(a, 1, 0))
        nxt = jnp.where(row == seq - 1, 0.0, pltpu.roll(a, seq - 1, 0))
        return _silu(w[0:1] * prv + w[1:2] * a + w[2:3] * nxt + b)

    cw = cw_ref[...]
    cb = cb_ref[...]
    xs_s[...] = conv(u_ref[:, 0:256], cw[:, 0:256], cb[:, 0:256])
    bm_s[...] = conv(u_ref[:, 512:640], cw[:, 256:384], cb[:, 256:384])
    cm_s[...] = conv(u_ref[:, 640:768], cw[:, 384:512], cb[:, 384:512])
    dtc_s[...] = _softplus(u_ref[:, 768:776] + dtb_c_ref[...])
    dtr_s[...] = _softplus(dtt_ref[...] + dtb_r_ref[...])
    eye_n = jnp.where(_iota((SSM_N, SSM_N), 0) == _iota((SSM_N, SSM_N), 1), 1.0, 0.0).astype(BF16)

    def transpose64(a):
        return sum(_nt(eye_n, p) for p in _split3(a))

    hpg = N_HEADS // SSM_GROUPS
    st_s[...] = jnp.zeros(st_s.shape, F32)
    if latent:
        for d in range(2):
            for h in range(N_HEADS):
                g = h // hpg
                st_s[d, 64 * g:64 * g + 64, 64 * h:64 * h + 64] = transpose64(h0_ref[0, 0, d, h])

    ri = _iota((q, q), 0)
    ci = _iota((q, q), 1)
    lower = ri >= ci
    upper = ri <= ci
    a_col = -jnp.exp(a_c_ref[...])
    a_row = -jnp.exp(a_r_ref[...])
    head_of_lane = _iota((1, 256), 1) >> 6
    own_group = (_iota((2 * SSM_N, 256), 0) >> 6) == (_iota((2 * SSM_N, 256), 1) >> 7)

    def chunk(c, d):
        c0 = pl.multiple_of(c * q, q)
        xs = xs_s[pl.ds(c0, q), :]
        bm_b = bm_s[pl.ds(c0, q), :].astype(BF16)
        cm_b = cm_s[pl.ds(c0, q), :].astype(BF16)
        dtc = dtc_s[pl.ds(c0, q), :]
        mask = lower if d == 0 else upper
        tri = jnp.where(mask, 1.0, 0.0).astype(BF16)
        tri_t = jnp.where(upper if d == 0 else lower, 1.0, 0.0).astype(BF16)
        acum_c = sum(_dot(tri, p) for p in _split3(dtc * a_col))
        acum_r = sum(_dot(p, tri_t) for p in _split3(dtr_s[:, pl.ds(c0, q)] * a_row))
        spread = jnp.where(_iota((8, 256), 0) == N_HEADS * d + (_iota((8, 256), 1) >> 6), 1.0, 0.0).astype(BF16)
        a_exp = sum(_dot(p, spread) for p in _split3(acum_c))
        dt_exp = sum(_dot(p, spread) for p in _split3(dtc))
        a_end = a_exp[q - 1:q, :] if d == 0 else a_exp[0:1, :]
        xdt = xs * dt_exp
        xdt_b = xdt.astype(BF16)
        st = st_s[d]
        y = _dot(cm_b, st.astype(BF16)) * jnp.exp(a_exp)
        for g in range(SSM_GROUPS):
            cb_g = _nt(cm_b[:, 64 * g:64 * g + 64], bm_b[:, 64 * g:64 * g + 64])
            for hh in range(hpg):
                h = g * hpg + hh
                j = N_HEADS * d + h
                seg = acum_c[:, j:j + 1] - acum_r[j:j + 1, :]
                decay = jnp.where(mask, jnp.exp(jnp.where(mask, seg, 0.0)), 0.0)
                x_h = jnp.where(head_of_lane == h, xdt_b, jnp.zeros_like(xdt_b))
                y = y + _dot((cb_g * decay).astype(BF16), x_h)
        if d == 0:
            y_s[pl.ds(c0, q), :] = y
        else:
            y_s[pl.ds(c0, q), :] = y_s[pl.ds(c0, q), :] + y
        xw = (xdt * jnp.exp(a_end - a_exp)).astype(BF16)
        upd = _dot(bm_s[pl.ds(c0, q), :].T.astype(BF16), xw)
        st_s[d] = st * jnp.exp(a_end) + jnp.where(own_group, upd, 0.0)

    def fwd(c, carry):
        chunk(c, 0)
        return carry

    def bwd(c, carry):
        chunk(nchunk - 1 - c, 1)
        return carry

    lax.fori_loop(0, nchunk, fwd, 0)
    lax.fori_loop(0, nchunk, bwd, 0)
    y = y_s[...] + dvec_ref[...] * xs_s[...]
    o_ref[...] = _rms(y * _silu(u_ref[:, 256:512]), gn_ref[...]).astype(BF16)
    if not latent:
        for d in range(2):
            for h in range(N_HEADS):
                g = h // hpg
                st_o_ref[0, d, h] = transpose64(st_s[d, 64 * g:64 * g + 64, 64 * h:64 * h + 64])


def _ssd(u_ssm, dt_t, conv_w, conv_b, dt_bias, a_log, d_vec, g_norm, h0=None, layer=0):
    latent = h0 is not None
    seq = L_LAT if latent else L_CTX
    nb = B_LAT if latent else B_CTX
    off = R_CTX // L_LAT if latent else 0
    const = lambda b: (0, 0)
    dtb = dt_bias.reshape(1, 8)
    alg = a_log.reshape(1, 8)
    in_specs = [pl.BlockSpec((seq, W_SSM), lambda b: (off + b, 0)),
                pl.BlockSpec((8, seq), lambda b: (0, off + b)),
                pl.BlockSpec((3, 512), const), pl.BlockSpec((1, 512), const),
                pl.BlockSpec((1, 8), const), pl.BlockSpec((8, 1), const),
                pl.BlockSpec((1, 8), const), pl.BlockSpec((8, 1), const),
                pl.BlockSpec((1, 256), const), pl.BlockSpec((1, 256), const)]
    args = [u_ssm, dt_t, conv_w, conv_b.reshape(1, 512), dtb, dtb.reshape(8, 1), alg, alg.reshape(8, 1),
            d_vec, g_norm]
    out_shape = [jax.ShapeDtypeStruct((nb * seq, 256), BF16)]
    out_specs = [pl.BlockSpec((seq, 256), lambda b: (b, 0))]
    if latent:
        in_specs.append(pl.BlockSpec((1, 1, 2, N_HEADS, SSM_P, SSM_N), lambda b: (b, layer, 0, 0, 0, 0)))
        args.append(h0)
    else:
        out_shape.append(jax.ShapeDtypeStruct((nb, 2, N_HEADS, SSM_P, SSM_N), F32))
        out_specs.append(pl.BlockSpec((1, 2, N_HEADS, SSM_P, SSM_N), lambda b: (b, 0, 0, 0, 0)))
    scratch = [pltpu.VMEM((seq, 256), F32), pltpu.VMEM((seq, 128), F32), pltpu.VMEM((seq, 128), F32),
               pltpu.VMEM((seq, 8), F32), pltpu.VMEM((8, seq), F32), pltpu.VMEM((seq, 256), F32),
               pltpu.VMEM((2, SSM_GROUPS * SSM_N, 256), F32)]
    return pl.pallas_call(
        functools.partial(_ssd_kernel, latent=latent, seq=seq),
        out_shape=out_shape,
        grid=(nb,),
        in_specs=in_specs,
        out_specs=out_specs,
        scratch_shapes=scratch,
        compiler_params=_cparams(("parallel",)),
        name="ssd_lat" if latent else "ssd_ctx",
    )(*args)


def _merge_kernel(xc_ref, xl_ref, mod_ref, c0, l0, c1, l1, c2, l2, c3, l3, gate_ref, wb_ref, wo_ref, g2_ref,
                  wr_ref, xo_ref, h2_ref, aff_ref, afft_ref):
    i = pl.program_id(0)
    acc = None
    for b, (bc, bl) in enumerate(((c0, l0), (c1, l1), (c2, l2), (c3, l3))):
        proj = _dot(_pick_group(i, bc, bl).astype(BF16), wb_ref[b])
        term = gate_ref[:, D * b:D * b + D].astype(F32) * proj
        acc = term if acc is None else acc + term
    x = _pick_group(i, xc_ref, xl_ref) + mod_ref[0, 2:3, :] * _dot(acc.astype(BF16), wo_ref[...])
    xo_ref[...] = x
    h2 = _rms(x, g2_ref[...]) * (1.0 + mod_ref[0, 4:5, :]) + mod_ref[0, 3:4, :]
    hb = h2.astype(BF16)
    h2_ref[...] = hb
    hl = (h2 - hb.astype(F32)).astype(BF16)
    wr = wr_ref[...]
    wh = wr.astype(BF16)
    wl = (wr - wh.astype(F32)).astype(BF16)
    logits = _dot(hb, wh) + _dot(hl, wh) + _dot(hb, wl)
    aff = _softmax_rows(logits)
    aff_ref[...] = aff
    eye = jnp.where(_iota((N_EXPERTS, N_EXPERTS), 0) == _iota((N_EXPERTS, N_EXPERTS), 1), 1.0, 0.0).astype(BF16)
    afft_ref[...] = sum(_nt(eye, p) for p in _split3(aff))


def _merge(x_ctx, x_lat, mod_l, branches, gates, wb, wo, g2, wr):
    const = lambda i: (0, 0)
    row = lambda i: (i, 0)
    ctx_row = lambda i: (_ctx_tile(i), 0)
    lat_row = lambda i: (_lat_tile(i), 0)
    return pl.pallas_call(
        _merge_kernel,
        out_shape=[jax.ShapeDtypeStruct((R_ALL, D), F32), jax.ShapeDtypeStruct((R_ALL, D), BF16),
                   jax.ShapeDtypeStruct((R_ALL, N_EXPERTS), F32), jax.ShapeDtypeStruct((N_EXPERTS, R_ALL), F32)],
        grid=(NT_ALL,),
        in_specs=[pl.BlockSpec((TM, D), ctx_row), pl.BlockSpec((TM, D), lat_row),
                  pl.BlockSpec((1, 6, D), lambda i: (_mod_row(i), 0, 0))]
                 + [pl.BlockSpec((TM, 256), ctx_row), pl.BlockSpec((TM, 256), lat_row)] * 4
                 + [pl.BlockSpec((TM, W_GATE), row),
                    pl.BlockSpec((4, 256, D), lambda i: (0, 0, 0), pipeline_mode=pl.Buffered(1)),
                    pl.BlockSpec((D, D), const, pipeline_mode=pl.Buffered(1)),
                    pl.BlockSpec((1, D), const),
                    pl.BlockSpec((D, N_EXPERTS), const)],
        out_specs=[pl.BlockSpec((TM, D), row), pl.BlockSpec((TM, D), row),
                   pl.BlockSpec((TM, N_EXPERTS), row), pl.BlockSpec((N_EXPERTS, TM), lambda i: (0, i))],
        compiler_params=_cparams(("parallel",)),
        name="merge_router",
    )(x_ctx, x_lat, mod_l, *[a for pair in branches for a in pair], gates, wb, wo, g2, wr)


def _select_kernel(afft_ref, lpos_t_ref, lpos_ref, tstart_ref, cnt_ref, gt_s, eq_s, need_s, carry_s, *, cap):
    t = pl.program_id(0)
    ne = N_EXPERTS

    @pl.when(t == 0)
    def _():
        aff = afft_ref[...]

        def count_ge(v):
            return jnp.sum(jnp.where(aff >= v, 1.0, 0.0), axis=1, keepdims=True)

        def bisect(_, lh):
            lo, hi = lh
            mid = jnp.where(lo > 0.0, jnp.sqrt(lo) * jnp.sqrt(hi), hi * 2.0 ** -32)
            mid = jnp.clip(mid, lo, hi)
            ok = count_ge(mid) >= cap
            return jnp.where(ok, mid, lo), jnp.where(ok, hi, mid)

        _, hi = lax.fori_loop(0, SELECT_BISECT_STEPS, bisect,
                              (jnp.zeros((ne, 1), F32), jnp.full((ne, 1), 2.0, F32)))

        def short(st):
            return jnp.min(st[1]) < cap

        def peel(st):
            bound, cnt = st
            nxt = jnp.max(jnp.where(aff < bound, aff, -1.0), axis=1, keepdims=True)
            upd = cnt < cap
            return jnp.where(upd, nxt, bound), jnp.where(upd, count_ge(nxt), cnt)

        thr, _ = lax.while_loop(short, peel, (hi, count_ge(hi)))
        gt = jnp.where(aff > thr, 1.0, 0.0)
        gt_s[...] = gt
        eq_s[...] = jnp.where(aff == thr, 1.0, 0.0)
        need_col = cap - jnp.sum(gt, axis=1, keepdims=True)
        eye = _iota((ne, ne), 0) == _iota((ne, ne), 1)
        need_s[...] = jnp.sum(jnp.where(eye, need_col, 0.0), axis=0, keepdims=True)
        carry_s[...] = jnp.zeros(carry_s.shape, F32)

    sl = pl.ds(pl.multiple_of(t * TM, TM), TM)
    eye_t = jnp.where(_iota((TM, TM), 0) == _iota((TM, TM), 1), 1.0, 0.0).astype(BF16)
    eye_e = jnp.where(_iota((ne, ne), 0) == _iota((ne, ne), 1), 1.0, 0.0).astype(BF16)
    before = jnp.where(_iota((TM, TM), 0) > _iota((TM, TM), 1), 1.0, 0.0).astype(BF16)
    gtm = _nt(eye_t, gt_s[:, sl].astype(BF16))
    eqm = _nt(eye_t, eq_s[:, sl].astype(BF16))
    eq_seen = carry_s[0:1, :]
    pos0 = carry_s[1:2, :]
    eq_rank = _dot(before, eqm.astype(BF16)) + eq_seen
    sel = jnp.maximum(gtm, eqm * jnp.where(eq_rank < need_s[...], 1.0, 0.0))
    lp = _dot(before, sel.astype(BF16))
    cnt = jnp.sum(sel, axis=0, keepdims=True)
    lpos_ref[...] = jnp.where(sel > 0.0, lp, -1.0)
    lp_t = _nt(eye_e, lp.astype(BF16))
    sel_t = _nt(eye_e, sel.astype(BF16))
    lpos_t_ref[...] = jnp.where(sel_t > 0.0, lp_t, -1.0)
    tstart_ref[0] = pos0.astype(jnp.int32)
    cnt_ref[0] = cnt.astype(jnp.int32)
    carry_s[0:1, :] = eq_seen + jnp.sum(eqm, axis=0, keepdims=True)
    carry_s[1:2, :] = pos0 + cnt


def _select(afft, cap):
    n = afft.shape[1]
    nt = n // TM
    ne = N_EXPERTS
    return pl.pallas_call(
        functools.partial(_select_kernel, cap=cap),
        out_shape=[jax.ShapeDtypeStruct((ne, n), F32), jax.ShapeDtypeStruct((n, ne), F32),
                   jax.ShapeDtypeStruct((nt, 1, ne), jnp.int32), jax.ShapeDtypeStruct((nt, 1, ne), jnp.int32)],
        grid=(nt,),
        in_specs=[pl.BlockSpec((ne, n), lambda t: (0, 0))],
        out_specs=[pl.BlockSpec((ne, TM), lambda t: (0, t)), pl.BlockSpec((TM, ne), lambda t: (t, 0)),
                   pl.BlockSpec((1, 1, ne), lambda t: (t, 0, 0)), pl.BlockSpec((1, 1, ne), lambda t: (t, 0, 0))],
        scratch_shapes=[pltpu.VMEM((ne, n), F32), pltpu.VMEM((ne, n), F32),
                        pltpu.VMEM((1, ne), F32), pltpu.VMEM((8, ne), F32)],
        compiler_params=_cparams(("arbitrary",)),
        name="moe_select",
    )(afft)


def _win_index(idx):
    e = jnp.zeros_like(idx)
    for k in range(1, N_EXPERTS):
        e = e + jnp.where(idx >= k * MOE_WIN, 1, 0)
    return e, idx - e * MOE_WIN


def _gather_consts():
    ne, win, al = N_EXPERTS, MOE_WIN, MOE_ALIGN
    rows = np.arange(ne * win)
    ex = (rows[:, None] // win == np.arange(ne)[None, :]).astype(np.float32)
    rows16 = np.arange(ne * al)
    ex16 = (rows16[:, None] // al == np.arange(ne)[None, :]).astype(np.float32)
    return (jnp.asarray(np.concatenate([ex, ex], axis=1), BF16),
            jnp.asarray(np.broadcast_to((rows % win)[:, None], (ne * win, TM)), F32),
            jnp.asarray(np.concatenate([ex16, ex16], axis=1), BF16),
            jnp.asarray(np.broadcast_to((rows16 % al)[:, None], (ne * al, TM)), F32))


def _gather_kernel(tstart_sm, cnt_sm, rounds_sm, h2_ref, lpos_t_ref, ex_ref, row_ref, ex16_ref, row16_ref,
                   xe_ref, stage, pend, sem, *, nt, cap):
    i = pl.program_id(0)
    slot = lax.rem(i, 2)
    ne, w, win, al = N_EXPERTS, MOE_W, MOE_WIN, MOE_ALIGN

    @pl.when(i == 0)
    def _():
        pend[...] = jnp.zeros(pend.shape, F32)

    def first_slot(e):
        return tstart_sm[i * ne + e]

    def below(e):
        return first_slot(e) & (al - 1)

    lp_t = lpos_t_ref[...]
    lp_t = jnp.where(lp_t < 0.0, -1e6, lp_t)
    sub = _iota((ne, TM), 0)
    shift = jnp.zeros((ne, TM), F32)
    shift16 = jnp.zeros((ne, TM), F32)
    for e in range(ne):
        shift = jnp.where(sub == e, below(e).astype(F32), shift)
        filled = below(e) + cnt_sm[i * ne + e]
        shift16 = jnp.where(sub == e, (below(e) - ((filled >> 4) << 4)).astype(F32), shift16)
    tgt = _dot(ex_ref[...], jnp.concatenate([lp_t, shift], axis=0).astype(BF16))

    def build(k, sl):
        oh = jnp.where(tgt == row_ref[...] + (k * w).astype(F32), 1.0, 0.0).astype(BF16)
        return _dot(oh, h2_ref[...])

    def copy(e, start, sl):
        return pltpu.make_async_copy(
            stage.at[sl, pl.ds(e * win, win)],
            xe_ref.at[e, pl.ds(pl.multiple_of(start, al), win)],
            sem.at[sl])

    def issue(k, sl):
        for e in range(ne):
            copy(e, jnp.minimum(first_slot(e) - below(e) + k * w, cap), sl).start()

    def wait_all(sl):
        for e in range(ne):
            copy(e, 0, sl).wait()

    zero = jnp.int32(0)
    g = build(zero, slot)
    stage[slot] = g.astype(BF16)
    for e in range(ne):
        stage[slot, e * win:e * win + al, :] = (g[e * win:e * win + al] + pend[e]).astype(BF16)

    tgt16 = _dot(ex16_ref[...], jnp.concatenate([lp_t, shift16], axis=0).astype(BF16))
    oh16 = jnp.where(tgt16 == row16_ref[...], 1.0, 0.0).astype(BF16)
    new_pend = _dot(oh16, h2_ref[...])
    for e in range(ne):
        keep = jnp.where(below(e) + cnt_sm[i * ne + e] < al, 1.0, 0.0)
        pend[e] = new_pend[e * al:(e + 1) * al] + keep * pend[e]

    @pl.when(i > 0)
    def _():
        wait_all(1 - slot)

    issue(zero, slot)

    def extra(k, carry):
        wait_all(slot)
        stage[slot] = build(k, slot).astype(BF16)
        issue(k, slot)
        return carry

    lax.fori_loop(1, rounds_sm[i], extra, 0)

    @pl.when(i == nt - 1)
    def _():
        wait_all(slot)
        stage[1 - slot, 0:win, :] = jnp.zeros((win, D), BF16)
        for e in range(ne):
            pltpu.make_async_copy(
                stage.at[1 - slot, pl.ds(0, win)], xe_ref.at[e, pl.ds(cap, win)], sem.at[1 - slot]).start()
        wait_all(1 - slot)


def _gather(h2b, lpos_t, tstart, cnt, rounds, cap, tile_off):
    n = lpos_t.shape[1]
    nt = n // TM
    ne, win = N_EXPERTS, MOE_WIN
    consts = _gather_consts()
    grid_spec = pltpu.PrefetchScalarGridSpec(
        num_scalar_prefetch=3,
        grid=(nt,),
        in_specs=[pl.BlockSpec((TM, D), lambda i, a, b, c: (tile_off + i, 0)),
                  pl.BlockSpec((ne, TM), lambda i, a, b, c: (0, i))]
                 + [pl.BlockSpec(t.shape, lambda i, a, b, c: (0, 0)) for t in consts],
        out_specs=pl.BlockSpec(memory_space=pl.ANY),
        scratch_shapes=[pltpu.VMEM((2, ne * win, D), BF16), pltpu.VMEM((ne, MOE_ALIGN, D), F32),
                        pltpu.SemaphoreType.DMA((2,))],
    )
    return pl.pallas_call(
        functools.partial(_gather_kernel, nt=nt, cap=cap),
        out_shape=jax.ShapeDtypeStruct((ne, cap + win, D), BF16),
        grid_spec=grid_spec,
        compiler_params=_cparams(("arbitrary",)),
        name="moe_gather",
    )(tstart, cnt, rounds, h2b, lpos_t, *consts)


FFN_TILES_CTX = (CAP_FACTOR * R_CTX // N_EXPERTS) // FFN_TM
FFN_TILES_LAT = (CAP_FACTOR * R_LAT // N_EXPERTS) // FFN_TM


def _ffn_kernel(xc_ref, xl_ref, wg_ref, wu_ref, wd_ref, yc_ref, yl_ref, wbuf, wg_s, wu_s, wd_s, sem, *, layer):
    e = pl.program_id(0)
    j = pl.program_id(1)
    slot = lax.rem(e, 2)

    def wcopy(k, w_ref, expert, sl):
        return pltpu.make_async_copy(w_ref.at[layer, expert], wbuf.at[sl, k], sem.at[sl, k])

    def start_weights(expert, sl):
        for k, w_ref in enumerate((wg_ref, wu_ref, wd_ref)):
            wcopy(k, w_ref, expert, sl).start()

    @pl.when(j == 0)
    def _():
        @pl.when(e == 0)
        def _():
            start_weights(e, slot)

        for k, w_ref in enumerate((wg_ref, wu_ref, wd_ref)):
            wcopy(k, w_ref, e, slot).wait()
        wg_s[...] = wbuf[slot, 0].astype(BF16)
        wu_s[...] = wbuf[slot, 1].astype(BF16)
        wd_s[...] = wbuf[slot, 2].astype(BF16)

        @pl.when(e + 1 < N_EXPERTS)
        def _():
            start_weights(e + 1, 1 - slot)

    x = jnp.where(j < FFN_TILES_CTX, xc_ref[0], xl_ref[0])
    hid = (_silu(_dot(x, wg_s[...])) * _dot(x, wu_s[...])).astype(BF16)
    y = _dot(hid, wd_s[...]).astype(BF16)

    @pl.when(j < FFN_TILES_CTX)
    def _():
        yc_ref[0] = y

    @pl.when(j >= FFN_TILES_CTX)
    def _():
        yl_ref[0] = y


def _ffn(xe_ctx, xe_lat, wg, wu, wd, layer):
    ne = N_EXPERTS
    blk = (1, FFN_TM, D)
    wspec = pl.BlockSpec(memory_space=pl.ANY)
    ctx_map = lambda e, j: (e, jnp.minimum(j, FFN_TILES_CTX - 1), 0)
    lat_map = lambda e, j: (e, jnp.maximum(j - FFN_TILES_CTX, 0), 0)
    return pl.pallas_call(
        functools.partial(_ffn_kernel, layer=layer),
        out_shape=[jax.ShapeDtypeStruct((ne, FFN_TILES_CTX * FFN_TM, D), BF16),
                   jax.ShapeDtypeStruct((ne, FFN_TILES_LAT * FFN_TM, D), BF16)],
        grid=(ne, FFN_TILES_CTX + FFN_TILES_LAT),
        in_specs=[pl.BlockSpec(blk, ctx_map), pl.BlockSpec(blk, lat_map), wspec, wspec, wspec],
        out_specs=[pl.BlockSpec(blk, ctx_map), pl.BlockSpec(blk, lat_map)],
        scratch_shapes=[pltpu.VMEM((2, 3, D, D), F32)] + [pltpu.VMEM((D, D), BF16)] * 3
                       + [pltpu.SemaphoreType.DMA((2, 3))],
        compiler_params=_cparams(("arbitrary", "arbitrary")),
        name="moe_ffn",
    )(xe_ctx, xe_lat, wg, wu, wd)


def _combine_kernel(tstart_sm, rounds_sm, lpos_ref, aff_ref, x_ref, mod_ref, fg_ref, ye_ref, o_ref,
                    ybuf, acc_s, sem, *, nt, cap, final):
    i = pl.program_id(0)
    slot = lax.rem(i, COMBINE_BUFS)
    ne, w, win, al = N_EXPERTS, MOE_W, MOE_WIN, MOE_ALIGN
    m = ne * win

    def win_start(tile, e, k):
        first = tstart_sm[tile * ne + e] + k * w
        return jnp.minimum((first >> 4) << 4, cap - win)

    def copy(e, start, sl):
        return pltpu.make_async_copy(
            ye_ref.at[e, pl.ds(pl.multiple_of(start, al), win)],
            ybuf.at[sl, pl.ds(e * win, win)],
            sem.at[sl])

    def fetch(tile, k, sl):
        for e in range(ne):
            copy(e, win_start(tile, e, k), sl).start()

    def wait_all(sl):
        for e in range(ne):
            copy(e, 0, sl).wait()

    zero = jnp.int32(0)

    ahead = COMBINE_BUFS - 1

    @pl.when(i == 0)
    def _():
        for t in range(min(ahead, nt)):
            fetch(jnp.int32(t), zero, jnp.int32(t))

    wait_all(slot)

    @pl.when(i + ahead < nt)
    def _():
        fetch(i + ahead, zero, lax.rem(i + ahead, COMBINE_BUFS))

    e_lane, r_lane = _win_index(_iota((1, m), 1))
    expand = jnp.where(_win_index(_iota((ne, m), 1))[0] == _iota((ne, m), 0), 1.0, 0.0).astype(BF16)
    aff = aff_ref[...]
    lpx = _dot(lpos_ref[...].astype(BF16), expand)
    affx = _dot(aff.astype(BF16), expand)
    rl = r_lane.astype(F32)

    def compute(k, sl):
        shift = jnp.zeros((1, m), F32)
        for e in range(ne):
            sh = (tstart_sm[i * ne + e] - win_start(i, e, k)).astype(F32)
            shift = jnp.where(e_lane == e, sh, shift)
        lo = (k * w).astype(F32)
        hit = jnp.where(lpx + shift == rl, 1.0, 0.0) * jnp.where(lpx >= lo, 1.0, 0.0) * jnp.where(lpx < lo + w, 1.0, 0.0)
        return _dot((hit * affx).astype(BF16), ybuf[sl])

    acc_s[...] = compute(zero, slot)

    def extra(k, carry):
        fetch(i, k, slot)
        wait_all(slot)
        acc_s[...] = acc_s[...] + compute(k, slot)
        return carry

    lax.fori_loop(1, rounds_sm[i], extra, 0)
    x = x_ref[...] + mod_ref[0, 5:6, :] * acc_s[...]
    if final:
        x = _rms(x, fg_ref[...])
    o_ref[...] = x


def _combine(x, mod_l, lpos, aff, ye, tstart, rounds, fg, cap, tile_off, final):
    n = lpos.shape[0]
    nt = n // TM
    ne = N_EXPERTS
    grid_spec = pltpu.PrefetchScalarGridSpec(
        num_scalar_prefetch=2,
        grid=(nt,),
        in_specs=[pl.BlockSpec((TM, ne), lambda i, a, b: (i, 0)),
                  pl.BlockSpec((TM, ne), lambda i, a, b: (tile_off + i, 0)),
                  pl.BlockSpec((TM, D), lambda i, a, b: (tile_off + i, 0)),
                  pl.BlockSpec((1, 6, D), lambda i, a, b: (_mod_row(tile_off + i), 0, 0)),
                  pl.BlockSpec((1, D), lambda i, a, b: (0, 0)),
                  pl.BlockSpec(memory_space=pl.ANY)],
        out_specs=pl.BlockSpec((TM, D), lambda i, a, b: (i, 0)),
        scratch_shapes=[pltpu.VMEM((COMBINE_BUFS, ne * MOE_WIN, D), BF16), pltpu.VMEM((TM, D), F32),
                        pltpu.SemaphoreType.DMA((COMBINE_BUFS,))],
    )
    return pl.pallas_call(
        functools.partial(_combine_kernel, nt=nt, cap=cap, final=final),
        out_shape=jax.ShapeDtypeStruct((n, D), F32),
        grid_spec=grid_spec,
        compiler_params=_cparams(("arbitrary",)),
        name="moe_combine",
    )(tstart, rounds, lpos, aff, x, mod_l, fg, ye)


def _moe_layer(x, mod_l, h2b, aff, afft, wg, wu, wd, fg, layer):
    final = layer == DEPTH - 1
    groups = []
    for n, tile_off, afft_g in ((R_CTX, 0, afft[:, :R_CTX]), (R_LAT, NT_CTX, afft[:, R_CTX:])):
        cap = (CAP_FACTOR * n) // N_EXPERTS
        lpos_t, lpos, tstart, cnt = _select(afft_g, cap)
        tstart = tstart.reshape(-1)
        rounds = jnp.maximum((jnp.max(cnt.reshape(-1, N_EXPERTS), axis=1) + MOE_W - 1) // MOE_W, 1).astype(jnp.int32)
        xe = _gather(h2b, lpos_t, tstart, cnt.reshape(-1), rounds, cap, tile_off)
        groups.append((lpos, tstart, rounds, cap, tile_off, xe))
    ye = _ffn(groups[0][-1], groups[1][-1], wg, wu, wd, layer)
    return [_combine(x, mod_l, lpos, aff, ye_g, tstart, rounds, fg, cap, tile_off, final)
            for (lpos, tstart, rounds, cap, tile_off, _), ye_g in zip(groups, ye)]


def _layout_w_in(w):
    o = np.cumsum([0, 256, 256, 256, 384, 256, 32, 256, 256, 256, 256, 256, 128, 128, 8, 4096])
    wb = w.astype(BF16)
    zeros = lambda n: jnp.zeros((D, n), BF16)
    w_all = jnp.concatenate([wb[:, o[0]:o[5]], zeros(64), wb[:, o[5]:o[6]], zeros(32), wb[:, o[6]:o[14]],
                             zeros(120), wb[:, o[14]:o[15]]], axis=1)
    return w_all, wb[:, o[13]:o[14]].T


def _layout_mla(w_uq, w_ukv):
    uq = w_uq.reshape(MLA_Q_RANK, N_HEADS, MLA_NOPE + MLA_ROPE)
    uq = jnp.pad(uq, ((0, 0), (0, 0), (0, LANES - MLA_NOPE - MLA_ROPE))).reshape(MLA_Q_RANK, N_HEADS * LANES)
    ukv = w_ukv.reshape(MLA_KV_RANK, N_HEADS, MLA_NOPE + MLA_V)
    uk = jnp.pad(ukv[:, :, :MLA_NOPE], ((0, 0), (0, 0), (0, LANES - MLA_NOPE))).reshape(MLA_KV_RANK, N_HEADS * LANES)
    uv = jnp.pad(ukv[:, :, MLA_NOPE:], ((0, 0), (0, 0), (0, LANES - MLA_V))).reshape(MLA_KV_RANK, N_HEADS * LANES)
    return uq.astype(BF16), uk.astype(BF16), uv.astype(BF16)


def kernel(x_prompt, x_sample, cache_na_k, cache_na_v, cache_mla_ckv, cache_mla_krope, cache_df_k, cache_df_v,
           state_ssm, c, c_ctx, mod_w, mod_b, norm1_g, norm2_g, w_in, na_rel_bias, mla_q_norm_g, mla_kv_norm_g,
           mla_w_uq, mla_w_ukv, df_lambda, df_subln_g, ssm_conv_w, ssm_conv_b, ssm_dt_bias, ssm_a_log, ssm_d,
           ssm_norm_g, w_branch, w_out, router_w, exp_w_gate, exp_w_up, exp_w_down, final_norm_g):
    x_ctx = x_prompt.reshape(R_CTX, D)
    x_lat = x_sample.reshape(R_LAT, D)
    cvec = jnp.concatenate([c_ctx[None, :], c, jnp.zeros((16 - 1 - B_LAT, D), F32)], axis=0)
    mod = _modulation(cvec, mod_w, mod_b).reshape(DEPTH, 16, 6, D)

    cos32, sa32, sb32 = _rope_tables32()
    pad128 = lambda a, fill: np.pad(a, ((0, 0), (64, 32)), constant_values=fill)
    rope128 = (pad128(cos32, 1.0), pad128(sa32, 0.0), pad128(sb32, 0.0))
    rope256 = tuple(np.tile(a, (1, 8)) for a in (cos32, sa32, sb32))
    fg = final_norm_g.reshape(1, D)
    cna_k = cache_na_k.reshape(B_LAT, DEPTH, PAST, 256)
    cna_v = cache_na_v.reshape(B_LAT, DEPTH, PAST, 256)
    cdf_k = cache_df_k.reshape(B_LAT, DEPTH, PAST, 256)
    cdf_v = cache_df_v.reshape(B_LAT, DEPTH, PAST, 256)

    outs = {k: [] for k in ("na_k", "na_v", "ckv", "krope", "df_k", "df_v", "ssm")}
    for l in range(DEPTH):
        mod_l = mod[l]
        (u_na, u_mla, u_df, u_ssm, gates, dt_t, na_k, na_v, df_k, df_v, krope) = _in_proj(
            x_ctx, x_lat, mod_l, norm1_g[l].reshape(1, D), *_layout_w_in(w_in[l]))
        bias = _na_bias_table(na_rel_bias[l])
        br_na = (_attn_ctx(u_na), _na_lat(u_na, cna_k, cna_v, bias, l))
        wuq, wuk, wuv = _layout_mla(mla_w_uq[l], mla_w_ukv[l])
        gq = mla_q_norm_g[l].reshape(1, MLA_Q_RANK)
        gkv = mla_kv_norm_g[l].reshape(1, MLA_KV_RANK)
        mla_c, ckv_new = _mla(u_mla, gq, gkv, wuq, wuk, wuv)
        kr_ctx = jnp.pad(cache_mla_krope[:, l], ((0, 0), (0, 0), (64, 32)))
        (mla_l,) = _mla(u_mla, gq, gkv, wuq, wuk, wuv, rope128, cache_mla_ckv, kr_ctx, l)
        lam_init = 0.8 - 0.6 * math.exp(-0.3 * l)
        gs = df_subln_g[l].reshape(1, DF_V)
        br_df = (_df(u_df, df_lambda[l], gs, lam_init),
                 _df(u_df, df_lambda[l], gs, lam_init, rope256, cdf_k, cdf_v, l))
        d_vec = jnp.repeat(ssm_d[l], SSM_P).reshape(1, 256)
        gn = ssm_norm_g[l].reshape(1, 256)
        ssm_c, st_new = _ssd(u_ssm, dt_t, ssm_conv_w[l], ssm_conv_b[l], ssm_dt_bias[l], ssm_a_log[l], d_vec, gn)
        (ssm_l,) = _ssd(u_ssm, dt_t, ssm_conv_w[l], ssm_conv_b[l], ssm_dt_bias[l], ssm_a_log[l], d_vec, gn,
                        state_ssm, l)
        x_mid, h2b, aff, afft = _merge(x_ctx, x_lat, mod_l, (br_na, (mla_c, mla_l), br_df, (ssm_c, ssm_l)), gates,
                                       w_branch[l].astype(BF16), w_out[l].astype(BF16),
                                       norm2_g[l].reshape(1, D), router_w[l])
        x_ctx, x_lat = _moe_layer(x_mid, mod_l, h2b, aff, afft, exp_w_gate, exp_w_up, exp_w_down, fg, l)
        outs["na_k"].append(na_k.reshape(B_CTX, L_CTX, N_HEADS, 64))
        outs["na_v"].append(na_v.reshape(B_CTX, L_CTX, N_HEADS, 64))
        outs["ckv"].append(ckv_new.reshape(B_CTX, L_CTX, MLA_KV_RANK))
        outs["krope"].append(krope.reshape(B_CTX, L_CTX, MLA_ROPE))
        outs["df_k"].append(df_k.reshape(B_CTX, L_CTX, N_HEADS, 2, DF_HD))
        outs["df_v"].append(df_v.reshape(B_CTX, L_CTX, N_HEADS, DF_V))
        outs["ssm"].append(st_new)
    stack = lambda k: jnp.stack(outs[k], axis=1)
    return (x_ctx.reshape(B_CTX, L_CTX, D), x_lat.reshape(B_LAT, L_LAT, D), stack("na_k"), stack("na_v"),
            stack("ckv"), stack("krope"), stack("df_k"), stack("df_v"), stack("ssm"))
```

```python
import functools
import math

import numpy as np
import jax
import jax.numpy as jnp
from jax import lax
from jax.experimental import pallas as pl
from jax.experimental.pallas import tpu as pltpu

F32 = jnp.float32
BF16 = jnp.bfloat16

D = 1024
B_CTX, L_CTX = 32, 256
B_LAT, L_LAT = 8, 2048
PAST = 256
DEPTH = 2
GRID_W = 64
EPS = 1e-6
ROPE_BASE = 10000.0
N_HEADS = 4
NA_KH, NA_KW = 8, 16
MLA_NOPE, MLA_ROPE, MLA_V = 64, 32, 64
MLA_Q_RANK, MLA_KV_RANK = 384, 256
DF_HD, DF_V = 32, 64
SSM_P, SSM_N, SSM_GROUPS = 64, 64, 2
N_EXPERTS = 16
CAP_FACTOR = 2

LANES = 128
SUBLANES = 8
TM = 256
R_CTX = B_CTX * L_CTX
R_LAT = B_LAT * L_LAT
R_ALL = R_CTX + R_LAT
NT_CTX = R_CTX // TM
NT_LAT = R_LAT // TM
NT_ALL = R_ALL // TM
TILES_PER_LAT_BATCH = L_LAT // TM
TQ = 256
ATT_TQ = 512
MLA_TQ = 1024
SSD_Q = 256
MOE_W = 64
MOE_ALIGN = 16
MOE_WIN = MOE_W + MOE_ALIGN
COMBINE_BUFS = 3
SELECT_BISECT_STEPS = 40
FFN_TM = 512
NA_QROWS = 4
NA_KROWS = NA_QROWS + NA_KH - 1
NA_NK = NA_KROWS * GRID_W
VMEM_LIMIT = 56 * 1024 * 1024


def _cparams(sem):
    return pltpu.CompilerParams(dimension_semantics=sem, vmem_limit_bytes=VMEM_LIMIT)


def _nt(a, b):
    return lax.dot_general(a, b, (((1,), (1,)), ((), ())), preferred_element_type=F32)


def _dot(a, b):
    return jnp.dot(a, b, preferred_element_type=F32)


def _rms(x, g):
    return x * lax.rsqrt(jnp.mean(x * x, axis=-1, keepdims=True) + EPS) * g


def _silu(x):
    return x * jax.nn.sigmoid(x)


def _softmax_rows(s):
    m = jnp.max(s, axis=-1, keepdims=True)
    e = jnp.exp(s - m)
    return e * (1.0 / jnp.sum(e, axis=-1, keepdims=True))


LOG2E = math.log2(math.e)


def _exp_rows(s, scale):
    m = jnp.max(s, axis=-1, keepdims=True)
    e = jnp.exp2((s - m) * (scale * LOG2E))
    return e, jnp.sum(e, axis=-1, keepdims=True)


def _exp_only(s, scale):
    return jnp.exp2((s - jnp.max(s, axis=-1, keepdims=True)) * (scale * LOG2E))


def _pv_normalised(e, v_ext):
    o = _dot(e.astype(BF16), v_ext)
    return o[:, 0:64] * (1.0 / o[:, 64:65])


def _store_v_ext(v_s, row0, v):
    rows = v.shape[0]
    lane = _iota((rows, LANES), 1)
    ones_col = jnp.where(lane == 64, 1.0, 0.0)
    for pair in range(2):
        blk = v[:, LANES * pair:LANES * pair + LANES]
        even = jnp.where(lane < 64, blk, ones_col)
        odd = jnp.where(lane < 64, pltpu.roll(blk, 64, 1), ones_col)
        v_s[row0:row0 + rows, 2 * LANES * pair:2 * LANES * pair + LANES] = even.astype(BF16)
        v_s[row0:row0 + rows, 2 * LANES * pair + LANES:2 * LANES * pair + 2 * LANES] = odd.astype(BF16)


def _split3(a):
    a1 = a.astype(BF16)
    r1 = a - a1.astype(F32)
    a2 = r1.astype(BF16)
    a3 = (r1 - a2.astype(F32)).astype(BF16)
    return a1, a2, a3


def _iota(shape, dim):
    return lax.broadcasted_iota(jnp.int32, shape, dim)


def _mod_row(i):
    return jnp.where(i < NT_CTX, 0, 1 + (i - NT_CTX) // TILES_PER_LAT_BATCH)


MOD_TN = 1536


def _mod_kernel(c_ref, w_ref, b_ref, o_ref):
    s = _silu(c_ref[...]).astype(BF16)
    o_ref[0] = _dot(s, w_ref[0].astype(BF16)) + b_ref[0]


def _modulation(cvec, mod_w, mod_b):
    n = 6 * D
    return pl.pallas_call(
        _mod_kernel,
        out_shape=jax.ShapeDtypeStruct((DEPTH, 16, n), F32),
        grid=(DEPTH, n // MOD_TN),
        in_specs=[pl.BlockSpec((16, D), lambda l, j: (0, 0)),
                  pl.BlockSpec((1, D, MOD_TN), lambda l, j: (l, 0, j)),
                  pl.BlockSpec((1, 1, MOD_TN), lambda l, j: (l, 0, j))],
        out_specs=pl.BlockSpec((1, 16, MOD_TN), lambda l, j: (l, 0, j)),
        compiler_params=_cparams(("parallel", "parallel")),
        name="modulation",
    )(cvec, mod_w, mod_b.reshape(DEPTH, 1, n))


W_NA, W_MLA, W_DF, W_SSM, W_GATE = 768, 768, 768, 896, 4 * D


W_IN_COLS = np.cumsum([0, W_NA, W_MLA, W_DF, W_SSM, W_GATE])
W_IN_ALL = int(W_IN_COLS[-1])


def _ctx_tile(i):
    return jnp.minimum(i, NT_CTX - 1)


def _lat_tile(i):
    return jnp.maximum(i - NT_CTX, 0)


def _pick_group(i, ctx_ref, lat_ref):
    return jnp.where(i < NT_CTX, ctx_ref[...], lat_ref[...])


def _in_kernel(xc_ref, xl_ref, mod_ref, g_ref, w_ref, wdt_t,
               ona, omla, odf, ossm, ogate, odt_t, onak, onav, odfk, odfv, okr):
    i = pl.program_id(0)
    x = _pick_group(i, xc_ref, xl_ref)
    h = _rms(x, g_ref[...]) * (1.0 + mod_ref[0, 1:2, :]) + mod_ref[0, 0:1, :]
    hb = h.astype(BF16)
    c = W_IN_COLS
    una = _dot(hb, w_ref[:, c[0]:c[1]])
    umla = _dot(hb, w_ref[:, c[1]:c[2]])
    udf = _dot(hb, w_ref[:, c[2]:c[3]])
    ona[...] = una
    omla[...] = umla
    odf[...] = udf
    ossm[...] = _dot(hb, w_ref[:, c[3]:c[4]])
    ogate[...] = jax.nn.sigmoid(_dot(hb, w_ref[:, c[4]:c[5]])).astype(BF16)
    odt_t[...] = _nt(wdt_t[...], hb)

    @pl.when(i < NT_CTX)
    def _():
        onak[...] = una[:, 256:512]
        onav[...] = una[:, 512:768]
        odfk[...] = udf[:, 256:512]
        odfv[...] = udf[:, 512:768]
        okr[...] = umla[:, 704:736]


def _in_proj(x_ctx, x_lat, mod_l, norm_g, w_all, w_dt_t):
    widths = (W_NA, W_MLA, W_DF, W_SSM)
    const = lambda i: (0, 0)
    row = lambda i: (i, 0)
    ctx_row = lambda i: (_ctx_tile(i), 0)
    out_shape = [jax.ShapeDtypeStruct((R_ALL, w), F32) for w in widths]
    out_shape += [jax.ShapeDtypeStruct((R_ALL, W_GATE), BF16), jax.ShapeDtypeStruct((8, R_ALL), F32)]
    out_shape += [jax.ShapeDtypeStruct((R_CTX, 256), F32)] * 4 + [jax.ShapeDtypeStruct((R_CTX, MLA_ROPE), F32)]
    out_specs = [pl.BlockSpec((TM, w), row) for w in widths]
    out_specs += [pl.BlockSpec((TM, W_GATE), row), pl.BlockSpec((8, TM), lambda i: (0, i))]
    out_specs += [pl.BlockSpec((TM, 256), ctx_row)] * 4 + [pl.BlockSpec((TM, MLA_ROPE), ctx_row)]
    return pl.pallas_call(
        _in_kernel,
        out_shape=out_shape,
        grid=(NT_ALL,),
        in_specs=[pl.BlockSpec((TM, D), ctx_row),
                  pl.BlockSpec((TM, D), lambda i: (_lat_tile(i), 0)),
                  pl.BlockSpec((1, 6, D), lambda i: (_mod_row(i), 0, 0)),
                  pl.BlockSpec((1, D), const),
                  pl.BlockSpec((D, W_IN_ALL), const, pipeline_mode=pl.Buffered(1)),
                  pl.BlockSpec((8, D), const, pipeline_mode=pl.Buffered(1))],
        out_specs=out_specs,
        compiler_params=_cparams(("arbitrary",)),
        name="in_proj",
    )(x_ctx, x_lat, mod_l, norm_g, w_all, w_dt_t)


def _rope(x, cos, sin_a, sin_b):
    n = x.shape[-1]
    nxt = pltpu.roll(x, n - 1, 1)
    prv = pltpu.roll(x, 1, 1)
    return x * cos + nxt * sin_a + prv * sin_b


def _rope_tables32():
    t = np.arange(L_LAT)
    quarter = 8
    inv = ROPE_BASE ** (-np.arange(quarter, dtype=np.float64) / quarter)
    rows = (t // GRID_W).astype(np.float64)[:, None]
    cols = (t % GRID_W).astype(np.float64)[:, None]
    ang = np.concatenate([rows * inv, cols * inv], axis=-1)
    cos = np.repeat(np.cos(ang), 2, axis=-1)
    sin = np.repeat(np.sin(ang), 2, axis=-1)
    even = (np.arange(32) % 2 == 0)[None, :]
    sin_a = np.where(even, -sin, 0.0)
    sin_b = np.where(even, 0.0, sin)
    return tuple(np.asarray(a, np.float32) for a in (cos, sin_a, sin_b))


def _attn_ctx_kernel(u_ref, o_ref, v_s):
    scale = 64 ** -0.5
    _store_v_ext(v_s, 0, u_ref[:, 512:768])
    s = jnp.concatenate([_nt(u_ref[:, 64 * h:64 * h + 64].astype(BF16),
                             u_ref[:, 256 + 64 * h:256 + 64 * h + 64].astype(BF16)) for h in range(N_HEADS)], axis=0)
    e = _exp_only(s, scale)
    for h in range(N_HEADS):
        o_ref[:, 64 * h:64 * h + 64] = _pv_normalised(e[L_CTX * h:L_CTX * h + L_CTX],
                                                      v_s[:, 128 * h:128 * h + 128]).astype(BF16)


def _attn_ctx(u_na):
    return pl.pallas_call(
        _attn_ctx_kernel,
        out_shape=jax.ShapeDtypeStruct((R_CTX, 256), BF16),
        grid=(B_CTX,),
        in_specs=[pl.BlockSpec((L_CTX, W_NA), lambda b: (b, 0))],
        out_specs=pl.BlockSpec((L_CTX, 256), lambda b: (b, 0)),
        scratch_shapes=[pltpu.VMEM((L_CTX, 4 * LANES), BF16)],
        compiler_params=_cparams(("parallel",)),
        name="na_ctx",
    )(u_na)


NA_ROWS = L_LAT // GRID_W
NA_NQT = NA_ROWS // NA_QROWS


def _na_pattern(qt):
    return jnp.where(qt == 0, 0, jnp.where(qt == NA_NQT - 1, 2, 1))


def _na_pattern_offsets(p, a):
    q_off = jnp.where(p == 0, 0, jnp.where(p == 1, NA_KH // 2, NA_KROWS - NA_QROWS))
    rs_rel = jnp.where(p == 0, 0, jnp.where(p == 1, a, NA_KROWS - NA_KH))
    return q_off, rs_rel


def _check_na_patterns():
    for qt in range(NA_NQT):
        ks = int(np.clip(NA_QROWS * qt - NA_KH // 2, 0, NA_ROWS - NA_KROWS))
        p = 0 if qt == 0 else (2 if qt == NA_NQT - 1 else 1)
        for a in range(NA_QROWS):
            r = NA_QROWS * qt + a
            rs = int(np.clip(r - NA_KH // 2, 0, NA_ROWS - NA_KH))
            q_off = (0, NA_KH // 2, NA_KROWS - NA_QROWS)[p]
            rs_rel = (0, a, NA_KROWS - NA_KH)[p]
            assert NA_QROWS * qt - ks == q_off and rs - ks == rs_rel and rs + NA_KH <= ks + NA_KROWS


_check_na_patterns()


def _na_bias_kernel(rb_ref, o_ref):
    p = pl.program_id(0)
    qc = _iota((GRID_W, GRID_W), 0)
    kc = _iota((GRID_W, GRID_W), 1)
    cs = jnp.clip(qc - NA_KW // 2, 0, GRID_W - NA_KW)
    col_ok = (kc >= cs) & (kc < cs + NA_KW)
    for a in range(NA_QROWS):
        q_off, rs_rel = _na_pattern_offsets(p, a)
        for b in range(NA_KROWS):
            row_ok = (b >= rs_rel) & (b < rs_rel + NA_KH)
            dr = jnp.clip(b - q_off - a + NA_KH - 1, 0, 2 * NA_KH - 2)
            v = jnp.broadcast_to(rb_ref[0, pl.ds(dr, 1), :], (GRID_W, LANES))
            t = pltpu.roll(v, LANES - (NA_KW - 1), 1, stride=1, stride_axis=0)[:, :GRID_W]
            o_ref[0, 0, GRID_W * a:GRID_W * a + GRID_W, GRID_W * b:GRID_W * b + GRID_W] = jnp.where(
                col_ok & row_ok, t * LOG2E, -1e30)


def _na_bias_table(rel_bias):
    rb = jnp.pad(rel_bias.astype(F32), ((0, 0), (0, 0), (0, LANES - (2 * NA_KW - 1))))
    return pl.pallas_call(
        _na_bias_kernel,
        out_shape=jax.ShapeDtypeStruct((3, N_HEADS, TQ, NA_NK), F32),
        grid=(3, N_HEADS),
        in_specs=[pl.BlockSpec((1, 2 * NA_KH - 1, LANES), lambda p, h: (h, 0, 0))],
        out_specs=pl.BlockSpec((1, 1, TQ, NA_NK), lambda p, h: (p, h, 0, 0)),
        compiler_params=_cparams(("parallel", "parallel")),
        name="na_bias",
    )(rb)


def _na_lat_kernel(u_ref, kc_ref, vc_ref, bias_ref, o_ref, v_s, vc_s):
    qt = pl.program_id(1)
    c = 64 ** -0.5 * LOG2E
    ks = jnp.clip(NA_QROWS * qt - NA_KH // 2, 0, NA_ROWS - NA_KROWS)
    kstart = pl.multiple_of(ks * GRID_W, GRID_W)
    qstart = pl.multiple_of(qt * TQ, TQ)
    _store_v_ext(v_s, 0, u_ref[pl.ds(kstart, NA_NK), 512:768])
    _store_v_ext(vc_s, 0, vc_ref[0, 0])
    for h in range(N_HEADS):
        q = u_ref[pl.ds(qstart, TQ), 64 * h:64 * h + 64].astype(BF16)
        k = u_ref[pl.ds(kstart, NA_NK), 256 + 64 * h:256 + 64 * h + 64].astype(BF16)
        kc = kc_ref[0, 0, :, 64 * h:64 * h + 64].astype(BF16)
        t_loc = _nt(q, k) * c + bias_ref[0, h]
        t_ctx = _nt(q, kc) * c
        m = jnp.maximum(jnp.max(t_loc, axis=-1, keepdims=True), jnp.max(t_ctx, axis=-1, keepdims=True))
        e_loc = jnp.exp2(t_loc - m).astype(BF16)
        e_ctx = jnp.exp2(t_ctx - m).astype(BF16)
        o = _dot(e_loc, v_s[:, 128 * h:128 * h + 128]) + _dot(e_ctx, vc_s[:, 128 * h:128 * h + 128])
        o_ref[:, 64 * h:64 * h + 64] = (o[:, 0:64] * (1.0 / o[:, 64:65])).astype(BF16)


def _na_lat(u_na, k_ctx, v_ctx, bias, layer):
    nqt = L_LAT // TQ
    cache = pl.BlockSpec((1, 1, PAST, 256), lambda b, t: (b, layer, 0, 0))
    return pl.pallas_call(
        _na_lat_kernel,
        out_shape=jax.ShapeDtypeStruct((R_LAT, 256), BF16),
        grid=(B_LAT, nqt),
        in_specs=[pl.BlockSpec((L_LAT, W_NA), lambda b, t: (R_CTX // L_LAT + b, 0)),
                  cache, cache,
                  pl.BlockSpec((1, N_HEADS, TQ, NA_NK), lambda b, t: (_na_pattern(t), 0, 0, 0))],
        out_specs=pl.BlockSpec((TQ, 256), lambda b, t: (b * nqt + t, 0)),
        scratch_shapes=[pltpu.VMEM((NA_NK, 4 * LANES), BF16), pltpu.VMEM((PAST, 4 * LANES), BF16)],
        compiler_params=_cparams(("parallel", "arbitrary")),
        name="na_lat",
    )(u_na, k_ctx, v_ctx, bias)


def _mla_kernel(*refs, latent, seq):
    if latent:
        (u_ref, gq_ref, gkv_ref, wuq_ref, wuk_ref, wuv_ref, cos_ref, sa_ref, sb_ref,
         ckv_c_ref, kr_c_ref, o_ref, k_s, v_s) = refs
    else:
        (u_ref, gq_ref, gkv_ref, wuq_ref, wuk_ref, wuv_ref, o_ref, ckv_o_ref, k_s, v_s) = refs
    scale = (MLA_NOPE + MLA_ROPE) ** -0.5
    ckv = _rms(u_ref[:, 384:640], gkv_ref[...])
    kr = u_ref[:, 640:768]
    if latent:
        kr = _rope(kr, cos_ref[...], sa_ref[...], sb_ref[...])
    else:
        ckv_o_ref[...] = ckv
    ckv_b = ckv.astype(BF16)
    for h in range(N_HEADS):
        k_s[0:seq, 128 * h:128 * h + 128] = (_dot(ckv_b, wuk_ref[:, 128 * h:128 * h + 128]) + kr).astype(BF16)
    ones_col = jnp.where((_iota((1, 4 * LANES), 1) & (LANES - 1)) == MLA_V, 1.0, 0.0)
    v_s[0:seq, :] = (_dot(ckv_b, wuv_ref[...]) + ones_col).astype(BF16)
    if latent:
        cc = ckv_c_ref[0, 0].astype(BF16)
        krc = kr_c_ref[0]
        for h in range(N_HEADS):
            k_s[seq:seq + PAST, 128 * h:128 * h + 128] = (
                _dot(cc, wuk_ref[:, 128 * h:128 * h + 128]) + krc).astype(BF16)
        v_s[seq:seq + PAST, :] = (_dot(cc, wuv_ref[...]) + ones_col).astype(BF16)

    tq = min(MLA_TQ, seq)

    def q_tile(t, carry):
        r0 = pl.multiple_of(t * tq, tq)
        cq = _rms(u_ref[pl.ds(r0, tq), 0:384], gq_ref[...]).astype(BF16)
        for h in range(N_HEADS):
            q = _dot(cq, wuq_ref[:, 128 * h:128 * h + 128])
            q = _rope(q, cos_ref[pl.ds(r0, tq), :], sa_ref[pl.ds(r0, tq), :], sb_ref[pl.ds(r0, tq), :])
            e = _exp_only(_nt(q.astype(BF16), k_s[:, 128 * h:128 * h + 128]), scale)
            o_ref[pl.ds(r0, tq), 64 * h:64 * h + 64] = _pv_normalised(e, v_s[:, 128 * h:128 * h + 128]).astype(BF16)
        return carry

    if latent:
        lax.fori_loop(0, seq // tq, q_tile, 0)
    else:
        cq = _rms(u_ref[:, 0:384], gq_ref[...]).astype(BF16)
        s = jnp.concatenate([_nt(_dot(cq, wuq_ref[:, 128 * h:128 * h + 128]).astype(BF16),
                                 k_s[:, 128 * h:128 * h + 128]) for h in range(N_HEADS)], axis=0)
        e = _exp_only(s, scale)
        for h in range(N_HEADS):
            o_ref[:, 64 * h:64 * h + 64] = _pv_normalised(e[seq * h:seq * h + seq],
                                                          v_s[:, 128 * h:128 * h + 128]).astype(BF16)


def _mla(u_mla, gq, gkv, wuq, wuk, wuv, rope128=None, ckv_ctx=None, kr_ctx=None, layer=0):
    latent = rope128 is not None
    seq = L_LAT if latent else L_CTX
    nb = B_LAT if latent else B_CTX
    off = R_CTX // L_LAT if latent else 0
    lk = seq + PAST if latent else seq
    const = lambda b: (0, 0)
    in_specs = [pl.BlockSpec((seq, W_MLA), lambda b: (off + b, 0)),
                pl.BlockSpec((1, MLA_Q_RANK), const),
                pl.BlockSpec((1, MLA_KV_RANK), const),
                pl.BlockSpec((MLA_Q_RANK, 512), const),
                pl.BlockSpec((MLA_KV_RANK, 512), const),
                pl.BlockSpec((MLA_KV_RANK, 512), const)]
    args = [u_mla, gq, gkv, wuq, wuk, wuv]
    out_shape = [jax.ShapeDtypeStruct((nb * seq, 256), BF16)]
    out_specs = [pl.BlockSpec((seq, 256), lambda b: (b, 0))]
    if latent:
        in_specs += [pl.BlockSpec((seq, LANES), const)] * 3
        in_specs += [pl.BlockSpec((1, 1, PAST, MLA_KV_RANK), lambda b: (b, layer, 0, 0)),
                     pl.BlockSpec((1, PAST, LANES), lambda b: (b, 0, 0))]
        args += list(rope128) + [ckv_ctx, kr_ctx]
    else:
        out_shape.append(jax.ShapeDtypeStruct((nb * seq, MLA_KV_RANK), F32))
        out_specs.append(pl.BlockSpec((seq, MLA_KV_RANK), lambda b: (b, 0)))
    return pl.pallas_call(
        functools.partial(_mla_kernel, latent=latent, seq=seq),
        out_shape=out_shape,
        grid=(nb,),
        in_specs=in_specs,
        out_specs=out_specs,
        scratch_shapes=[pltpu.VMEM((lk, 512), BF16), pltpu.VMEM((lk, 512), BF16)],
        compiler_params=_cparams(("parallel",)),
        name="mla_lat" if latent else "mla_ctx",
    )(*args)


def _df_kernel(*refs, latent, seq, lam_init):
    if latent:
        (u_ref, lv_ref, gs_ref, cos_ref, sa_ref, sb_ref, kc_ref, vc_ref, o_ref, k_s, v_s) = refs
    else:
        (u_ref, lv_ref, gs_ref, o_ref, k_s, v_s) = refs
    scale = DF_HD ** -0.5
    lv = lv_ref[...]
    lam = (jnp.exp(jnp.sum(lv[0:1] * lv[1:2], axis=1, keepdims=True))
           - jnp.exp(jnp.sum(lv[2:3] * lv[3:4], axis=1, keepdims=True)) + lam_init)
    k = u_ref[:, 256:512]
    if latent:
        k = _rope(k, cos_ref[...], sa_ref[...], sb_ref[...])
        k_s[seq:seq + PAST, :] = kc_ref[0, 0].astype(BF16)
        _store_v_ext(v_s, seq, vc_ref[0, 0])
    k_s[0:seq, :] = k.astype(BF16)
    _store_v_ext(v_s, 0, u_ref[:, 512:768])
    tq = min(ATT_TQ, seq)
    first = _iota((tq, 64), 1) < DF_HD

    def q_tile(t, carry):
        r0 = pl.multiple_of(t * tq, tq)
        q = u_ref[pl.ds(r0, tq), 0:256]
        if latent:
            q = _rope(q, cos_ref[pl.ds(r0, tq), :], sa_ref[pl.ds(r0, tq), :], sb_ref[pl.ds(r0, tq), :])
        for h in range(N_HEADS):
            qh = q[:, 64 * h:64 * h + 64]
            kh = k_s[:, 64 * h:64 * h + 64]
            q0 = jnp.where(first, qh, 0.0).astype(BF16)
            q1 = jnp.where(first, 0.0, qh).astype(BF16)
            vh = v_s[:, 128 * h:128 * h + 128]
            o = (_pv_normalised(_exp_only(_nt(q0, kh), scale), vh)
                 - lam * _pv_normalised(_exp_only(_nt(q1, kh), scale), vh))
            o_ref[pl.ds(r0, tq), 64 * h:64 * h + 64] = (_rms(o, gs_ref[...]) * (1.0 - lam_init)).astype(BF16)
        return carry

    if latent:
        lax.fori_loop(0, seq // tq, q_tile, 0)
    else:
        q = u_ref[:, 0:256]
        blocks = []
        for h in range(N_HEADS):
            qh = q[:, 64 * h:64 * h + 64]
            kh = k_s[:, 64 * h:64 * h + 64]
            blocks.append(_nt(jnp.where(first, qh, 0.0).astype(BF16), kh))
            blocks.append(_nt(jnp.where(first, 0.0, qh).astype(BF16), kh))
        e = _exp_only(jnp.concatenate(blocks, axis=0), scale)
        for h in range(N_HEADS):
            vh = v_s[:, 128 * h:128 * h + 128]
            o = (_pv_normalised(e[2 * h * seq:(2 * h + 1) * seq], vh)
                 - lam * _pv_normalised(e[(2 * h + 1) * seq:(2 * h + 2) * seq], vh))
            o_ref[:, 64 * h:64 * h + 64] = (_rms(o, gs_ref[...]) * (1.0 - lam_init)).astype(BF16)


def _df(u_df, lam_vec, g_sub, lam_init, rope256=None, k_ctx=None, v_ctx=None, layer=0):
    latent = rope256 is not None
    seq = L_LAT if latent else L_CTX
    nb = B_LAT if latent else B_CTX
    off = R_CTX // L_LAT if latent else 0
    lk = seq + PAST if latent else seq
    const = lambda b: (0, 0)
    in_specs = [pl.BlockSpec((seq, W_DF), lambda b: (off + b, 0)),
                pl.BlockSpec((4, DF_HD), const),
                pl.BlockSpec((1, DF_V), const)]
    args = [u_df, lam_vec, g_sub]
    if latent:
        in_specs += [pl.BlockSpec((seq, 256), const)] * 3
        in_specs += [pl.BlockSpec((1, 1, PAST, 256), lambda b: (b, layer, 0, 0))] * 2
        args += list(rope256) + [k_ctx, v_ctx]
    return pl.pallas_call(
        functools.partial(_df_kernel, latent=latent, seq=seq, lam_init=lam_init),
        out_shape=jax.ShapeDtypeStruct((nb * seq, 256), BF16),
        grid=(nb,),
        in_specs=in_specs,
        out_specs=pl.BlockSpec((seq, 256), lambda b: (b, 0)),
        scratch_shapes=[pltpu.VMEM((lk, 256), BF16), pltpu.VMEM((lk, 512), BF16)],
        compiler_params=_cparams(("parallel",)),
        name="df_lat" if latent else "df_ctx",
    )(*args)


def _softplus(x):
    return jnp.maximum(x, 0.0) + jnp.log1p(jnp.exp(-jnp.abs(x)))


def _ssd_kernel(*refs, latent, seq):
    if latent:
        (u_ref, dtt_ref, cw_ref, cb_ref, dtb_c_ref, dtb_r_ref, a_c_ref, a_r_ref, dvec_ref, gn_ref, h0_ref,
         o_ref, xs_s, bm_s, cm_s, dtc_s, dtr_s, y_s, st_s) = refs
    else:
        (u_ref, dtt_ref, cw_ref, cb_ref, dtb_c_ref, dtb_r_ref, a_c_ref, a_r_ref, dvec_ref, gn_ref,
         o_ref, st_o_ref, xs_s, bm_s, cm_s, dtc_s, dtr_s, y_s, st_s) = refs
    q = SSD_Q
    nchunk = seq // q

    def conv(a, w, b):
        row = _iota(a.shape, 0)
        prv = jnp.where(row == 0, 0.0, pltpu.roll(a, 1, 0))
        nxt = jnp.where(row == seq - 1, 0.0, pltpu.roll(a, seq - 1, 0))
        return _silu(w[0:1] * prv + w[1:2] * a + w[2:3] * nxt + b)

    cw = cw_ref[...]
    cb = cb_ref[...]
    xs_s[...] = conv(u_ref[:, 0:256], cw[:, 0:256], cb[:, 0:256])
    bm_s[...] = conv(u_ref[:, 512:640], cw[:, 256:384], cb[:, 256:384])
    cm_s[...] = conv(u_ref[:, 640:768], cw[:, 384:512], cb[:, 384:512])
    dtc_s[...] = _softplus(u_ref[:, 768:776] + dtb_c_ref[...])
    dtr_s[...] = _softplus(dtt_ref[...] + dtb_r_ref[...])
    eye_n = jnp.where(_iota((SSM_N, SSM_N), 0) == _iota((SSM_N, SSM_N), 1), 1.0, 0.0).astype(BF16)

    def transpose64(a):
        return sum(_nt(eye_n, p) for p in _split3(a))

    hpg = N_HEADS // SSM_GROUPS
    st_s[...] = jnp.zeros(st_s.shape, F32)
    if latent:
        for d in range(2):
            for h in range(N_HEADS):
                g = h // hpg
                st_s[d, 64 * g:64 * g + 64, 64 * h:64 * h + 64] = transpose64(h0_ref[0, 0, d, h])

    ri = _iota((q, q), 0)
    ci = _iota((q, q), 1)
    lower = ri >= ci
    upper = ri <= ci
    a_col = -jnp.exp(a_c_ref[...])
    a_row = -jnp.exp(a_r_ref[...])
    head_of_lane = _iota((1, 256), 1) >> 6
    own_group = (_iota((2 * SSM_N, 256), 0) >> 6) == (_iota((2 * SSM_N, 256), 1) >> 7)

    def chunk(c, d):
        c0 = pl.multiple_of(c * q, q)
        xs = xs_s[pl.ds(c0, q), :]
        bm_b = bm_s[pl.ds(c0, q), :].astype(BF16)
        cm_b = cm_s[pl.ds(c0, q), :].astype(BF16)
        dtc = dtc_s[pl.ds(c0, q), :]
        mask = lower if d == 0 else upper
        tri = jnp.where(mask, 1.0, 0.0).astype(BF16)
        tri_t = jnp.where(upper if d == 0 else lower, 1.0, 0.0).astype(BF16)
        acum_c = sum(_dot(tri, p) for p in _split3(dtc * a_col))
        acum_r = sum(_dot(p, tri_t) for p in _split3(dtr_s[:, pl.ds(c0, q)] * a_row))
        spread = jnp.where(_iota((8, 256), 0) == N_HEADS * d + (_iota((8, 256), 1) >> 6), 1.0, 0.0).astype(BF16)
        a_exp = sum(_dot(p, spread) for p in _split3(acum_c))
        dt_exp = sum(_dot(p, spread) for p in _split3(dtc))
        a_end = a_exp[q - 1:q, :] if d == 0 else a_exp[0:1, :]
        xdt = xs * dt_exp
        xdt_b = xdt.astype(BF16)
        st = st_s[d]
        y = _dot(cm_b, st.astype(BF16)) * jnp.exp(a_exp)
        for g in range(SSM_GROUPS):
            cb_g = _nt(cm_b[:, 64 * g:64 * g + 64], bm_b[:, 64 * g:64 * g + 64])
            for hh in range(hpg):
                h = g * hpg + hh
                j = N_HEADS * d + h
                seg = acum_c[:, j:j + 1] - acum_r[j:j + 1, :]
                decay = jnp.where(mask, jnp.exp(jnp.where(mask, seg, 0.0)), 0.0)
                x_h = jnp.where(head_of_lane == h, xdt_b, jnp.zeros_like(xdt_b))
                y = y + _dot((cb_g * decay).astype(BF16), x_h)
        if d == 0:
            y_s[pl.ds(c0, q), :] = y
        else:
            y_s[pl.ds(c0, q), :] = y_s[pl.ds(c0, q), :] + y
        xw = (xdt * jnp.exp(a_end - a_exp)).astype(BF16)
        upd = _dot(bm_s[pl.ds(c0, q), :].T.astype(BF16), xw)
        st_s[d] = st * jnp.exp(a_end) + jnp.where(own_group, upd, 0.0)

    def fwd(c, carry):
        chunk(c, 0)
        return carry

    def bwd(c, carry):
        chunk(nchunk - 1 - c, 1)
        return carry

    lax.fori_loop(0, nchunk, fwd, 0)
    lax.fori_loop(0, nchunk, bwd, 0)
    y = y_s[...] + dvec_ref[...] * xs_s[...]
    o_ref[...] = _rms(y * _silu(u_ref[:, 256:512]), gn_ref[...]).astype(BF16)
    if not latent:
        for d in range(2):
            for h in range(N_HEADS):
                g = h // hpg
                st_o_ref[0, d, h] = transpose64(st_s[d, 64 * g:64 * g + 64, 64 * h:64 * h + 64])


def _ssd(u_ssm, dt_t, conv_w, conv_b, dt_bias, a_log, d_vec, g_norm, h0=None, layer=0):
    latent = h0 is not None
    seq = L_LAT if latent else L_CTX
    nb = B_LAT if latent else B_CTX
    off = R_CTX // L_LAT if latent else 0
    const = lambda b: (0, 0)
    dtb = dt_bias.reshape(1, 8)
    alg = a_log.reshape(1, 8)
    in_specs = [pl.BlockSpec((seq, W_SSM), lambda b: (off + b, 0)),
                pl.BlockSpec((8, seq), lambda b: (0, off + b)),
                pl.BlockSpec((3, 512), const), pl.BlockSpec((1, 512), const),
                pl.BlockSpec((1, 8), const), pl.BlockSpec((8, 1), const),
                pl.BlockSpec((1, 8), const), pl.BlockSpec((8, 1), const),
                pl.BlockSpec((1, 256), const), pl.BlockSpec((1, 256), const)]
    args = [u_ssm, dt_t, conv_w, conv_b.reshape(1, 512), dtb, dtb.reshape(8, 1), alg, alg.reshape(8, 1),
            d_vec, g_norm]
    out_shape = [jax.ShapeDtypeStruct((nb * seq, 256), BF16)]
    out_specs = [pl.BlockSpec((seq, 256), lambda b: (b, 0))]
    if latent:
        in_specs.append(pl.BlockSpec((1, 1, 2, N_HEADS, SSM_P, SSM_N), lambda b: (b, layer, 0, 0, 0, 0)))
        args.append(h0)
    else:
        out_shape.append(jax.ShapeDtypeStruct((nb, 2, N_HEADS, SSM_P, SSM_N), F32))
        out_specs.append(pl.BlockSpec((1, 2, N_HEADS, SSM_P, SSM_N), lambda b: (b, 0, 0, 0, 0)))
    scratch = [pltpu.VMEM((seq, 256), F32), pltpu.VMEM((seq, 128), F32), pltpu.VMEM((seq, 128), F32),
               pltpu.VMEM((seq, 8), F32), pltpu.VMEM((8, seq), F32), pltpu.VMEM((seq, 256), F32),
               pltpu.VMEM((2, SSM_GROUPS * SSM_N, 256), F32)]
    return pl.pallas_call(
        functools.partial(_ssd_kernel, latent=latent, seq=seq),
        out_shape=out_shape,
        grid=(nb,),
        in_specs=in_specs,
        out_specs=out_specs,
        scratch_shapes=scratch,
        compiler_params=_cparams(("parallel",)),
        name="ssd_lat" if latent else "ssd_ctx",
    )(*args)


def _merge_kernel(xc_ref, xl_ref, mod_ref, c0, l0, c1, l1, c2, l2, c3, l3, gate_ref, wb_ref, wo_ref, g2_ref,
                  wr_ref, xo_ref, h2_ref, aff_ref, afft_ref):
    i = pl.program_id(0)
    acc = None
    for b, (bc, bl) in enumerate(((c0, l0), (c1, l1), (c2, l2), (c3, l3))):
        proj = _dot(_pick_group(i, bc, bl).astype(BF16), wb_ref[b])
        term = gate_ref[:, D * b:D * b + D].astype(F32) * proj
        acc = term if acc is None else acc + term
    x = _pick_group(i, xc_ref, xl_ref) + mod_ref[0, 2:3, :] * _dot(acc.astype(BF16), wo_ref[...])
    xo_ref[...] = x
    h2 = _rms(x, g2_ref[...]) * (1.0 + mod_ref[0, 4:5, :]) + mod_ref[0, 3:4, :]
    hb = h2.astype(BF16)
    h2_ref[...] = hb
    hl = (h2 - hb.astype(F32)).astype(BF16)
    wr = wr_ref[...]
    wh = wr.astype(BF16)
    wl = (wr - wh.astype(F32)).astype(BF16)
    logits = _dot(hb, wh) + _dot(hl, wh) + _dot(hb, wl)
    aff = _softmax_rows(logits)
    aff_ref[...] = aff
    eye = jnp.where(_iota((N_EXPERTS, N_EXPERTS), 0) == _iota((N_EXPERTS, N_EXPERTS), 1), 1.0, 0.0).astype(BF16)
    afft_ref[...] = sum(_nt(eye, p) for p in _split3(aff))


def _merge(x_ctx, x_lat, mod_l, branches, gates, wb, wo, g2, wr):
    const = lambda i: (0, 0)
    row = lambda i: (i, 0)
    ctx_row = lambda i: (_ctx_tile(i), 0)
    lat_row = lambda i: (_lat_tile(i), 0)
    return pl.pallas_call(
        _merge_kernel,
        out_shape=[jax.ShapeDtypeStruct((R_ALL, D), F32), jax.ShapeDtypeStruct((R_ALL, D), BF16),
                   jax.ShapeDtypeStruct((R_ALL, N_EXPERTS), F32), jax.ShapeDtypeStruct((N_EXPERTS, R_ALL), F32)],
        grid=(NT_ALL,),
        in_specs=[pl.BlockSpec((TM, D), ctx_row), pl.BlockSpec((TM, D), lat_row),
                  pl.BlockSpec((1, 6, D), lambda i: (_mod_row(i), 0, 0))]
                 + [pl.BlockSpec((TM, 256), ctx_row), pl.BlockSpec((TM, 256), lat_row)] * 4
                 + [pl.BlockSpec((TM, W_GATE), row),
                    pl.BlockSpec((4, 256, D), lambda i: (0, 0, 0), pipeline_mode=pl.Buffered(1)),
                    pl.BlockSpec((D, D), const, pipeline_mode=pl.Buffered(1)),
                    pl.BlockSpec((1, D), const),
                    pl.BlockSpec((D, N_EXPERTS), const)],
        out_specs=[pl.BlockSpec((TM, D), row), pl.BlockSpec((TM, D), row),
                   pl.BlockSpec((TM, N_EXPERTS), row), pl.BlockSpec((N_EXPERTS, TM), lambda i: (0, i))],
        compiler_params=_cparams(("parallel",)),
        name="merge_router",
    )(x_ctx, x_lat, mod_l, *[a for pair in branches for a in pair], gates, wb, wo, g2, wr)


def _select_kernel(afft_ref, lpos_t_ref, lpos_ref, tstart_ref, cnt_ref, gt_s, eq_s, need_s, carry_s, *, cap):
    t = pl.program_id(0)
    ne = N_EXPERTS

    @pl.when(t == 0)
    def _():
        aff = afft_ref[...]

        def count_ge(v):
            return jnp.sum(jnp.where(aff >= v, 1.0, 0.0), axis=1, keepdims=True)

        def bisect(_, lh):
            lo, hi = lh
            mid = jnp.where(lo > 0.0, jnp.sqrt(lo) * jnp.sqrt(hi), hi * 2.0 ** -32)
            mid = jnp.clip(mid, lo, hi)
            ok = count_ge(mid) >= cap
            return jnp.where(ok, mid, lo), jnp.where(ok, hi, mid)

        _, hi = lax.fori_loop(0, SELECT_BISECT_STEPS, bisect,
                              (jnp.zeros((ne, 1), F32), jnp.full((ne, 1), 2.0, F32)))

        def short(st):
            return jnp.min(st[1]) < cap

        def peel(st):
            bound, cnt = st
            nxt = jnp.max(jnp.where(aff < bound, aff, -1.0), axis=1, keepdims=True)
            upd = cnt < cap
            return jnp.where(upd, nxt, bound), jnp.where(upd, count_ge(nxt), cnt)

        thr, _ = lax.while_loop(short, peel, (hi, count_ge(hi)))
        gt = jnp.where(aff > thr, 1.0, 0.0)
        gt_s[...] = gt
        eq_s[...] = jnp.where(aff == thr, 1.0, 0.0)
        need_col = cap - jnp.sum(gt, axis=1, keepdims=True)
        eye = _iota((ne, ne), 0) == _iota((ne, ne), 1)
        need_s[...] = jnp.sum(jnp.where(eye, need_col, 0.0), axis=0, keepdims=True)
        carry_s[...] = jnp.zeros(carry_s.shape, F32)

    sl = pl.ds(pl.multiple_of(t * TM, TM), TM)
    eye_t = jnp.where(_iota((TM, TM), 0) == _iota((TM, TM), 1), 1.0, 0.0).astype(BF16)
    eye_e = jnp.where(_iota((ne, ne), 0) == _iota((ne, ne), 1), 1.0, 0.0).astype(BF16)
    before = jnp.where(_iota((TM, TM), 0) > _iota((TM, TM), 1), 1.0, 0.0).astype(BF16)
    gtm = _nt(eye_t, gt_s[:, sl].astype(BF16))
    eqm = _nt(eye_t, eq_s[:, sl].astype(BF16))
    eq_seen = carry_s[0:1, :]
    pos0 = carry_s[1:2, :]
    eq_rank = _dot(before, eqm.astype(BF16)) + eq_seen
    sel = jnp.maximum(gtm, eqm * jnp.where(eq_rank < need_s[...], 1.0, 0.0))
    lp = _dot(before, sel.astype(BF16))
    cnt = jnp.sum(sel, axis=0, keepdims=True)
    lpos_ref[...] = jnp.where(sel > 0.0, lp, -1.0)
    lp_t = _nt(eye_e, lp.astype(BF16))
    sel_t = _nt(eye_e, sel.astype(BF16))
    lpos_t_ref[...] = jnp.where(sel_t > 0.0, lp_t, -1.0)
    tstart_ref[0] = pos0.astype(jnp.int32)
    cnt_ref[0] = cnt.astype(jnp.int32)
    carry_s[0:1, :] = eq_seen + jnp.sum(eqm, axis=0, keepdims=True)
    carry_s[1:2, :] = pos0 + cnt


def _select(afft, cap):
    n = afft.shape[1]
    nt = n // TM
    ne = N_EXPERTS
    return pl.pallas_call(
        functools.partial(_select_kernel, cap=cap),
        out_shape=[jax.ShapeDtypeStruct((ne, n), F32), jax.ShapeDtypeStruct((n, ne), F32),
                   jax.ShapeDtypeStruct((nt, 1, ne), jnp.int32), jax.ShapeDtypeStruct((nt, 1, ne), jnp.int32)],
        grid=(nt,),
        in_specs=[pl.BlockSpec((ne, n), lambda t: (0, 0))],
        out_specs=[pl.BlockSpec((ne, TM), lambda t: (0, t)), pl.BlockSpec((TM, ne), lambda t: (t, 0)),
                   pl.BlockSpec((1, 1, ne), lambda t: (t, 0, 0)), pl.BlockSpec((1, 1, ne), lambda t: (t, 0, 0))],
        scratch_shapes=[pltpu.VMEM((ne, n), F32), pltpu.VMEM((ne, n), F32),
                        pltpu.VMEM((1, ne), F32), pltpu.VMEM((8, ne), F32)],
        compiler_params=_cparams(("arbitrary",)),
        name="moe_select",
    )(afft)


def _win_index(idx):
    e = jnp.zeros_like(idx)
    for k in range(1, N_EXPERTS):
        e = e + jnp.where(idx >= k * MOE_WIN, 1, 0)
    return e, idx - e * MOE_WIN


def _gather_consts():
    ne, win, al = N_EXPERTS, MOE_WIN, MOE_ALIGN
    rows = np.arange(ne * win)
    ex = (rows[:, None] // win == np.arange(ne)[None, :]).astype(np.float32)
    rows16 = np.arange(ne * al)
    ex16 = (rows16[:, None] // al == np.arange(ne)[None, :]).astype(np.float32)
    return (jnp.asarray(np.concatenate([ex, ex], axis=1), BF16),
            jnp.asarray(np.broadcast_to((rows % win)[:, None], (ne * win, TM)), F32),
            jnp.asarray(np.concatenate([ex16, ex16], axis=1), BF16),
            jnp.asarray(np.broadcast_to((rows16 % al)[:, None], (ne * al, TM)), F32))


def _gather_kernel(tstart_sm, cnt_sm, rounds_sm, h2_ref, lpos_t_ref, ex_ref, row_ref, ex16_ref, row16_ref,
                   xe_ref, stage, pend, sem, *, nt, cap):
    i = pl.program_id(0)
    slot = lax.rem(i, 2)
    ne, w, win, al = N_EXPERTS, MOE_W, MOE_WIN, MOE_ALIGN

    @pl.when(i == 0)
    def _():
        pend[...] = jnp.zeros(pend.shape, F32)

    def first_slot(e):
        return tstart_sm[i * ne + e]

    def below(e):
        return first_slot(e) & (al - 1)

    lp_t = lpos_t_ref[...]
    lp_t = jnp.where(lp_t < 0.0, -1e6, lp_t)
    sub = _iota((ne, TM), 0)
    shift = jnp.zeros((ne, TM), F32)
    shift16 = jnp.zeros((ne, TM), F32)
    for e in range(ne):
        shift = jnp.where(sub == e, below(e).astype(F32), shift)
        filled = below(e) + cnt_sm[i * ne + e]
        shift16 = jnp.where(sub == e, (below(e) - ((filled >> 4) << 4)).astype(F32), shift16)
    tgt = _dot(ex_ref[...], jnp.concatenate([lp_t, shift], axis=0).astype(BF16))

    def build(k, sl):
        oh = jnp.where(tgt == row_ref[...] + (k * w).astype(F32), 1.0, 0.0).astype(BF16)
        return _dot(oh, h2_ref[...])

    def copy(e, start, sl):
        return pltpu.make_async_copy(
            stage.at[sl, pl.ds(e * win, win)],
            xe_ref.at[e, pl.ds(pl.multiple_of(start, al), win)],
            sem.at[sl])

    def issue(k, sl):
        for e in range(ne):
            copy(e, jnp.minimum(first_slot(e) - below(e) + k * w, cap), sl).start(priority=e % 2)

    def wait_all(sl):
        for e in range(ne):
            copy(e, 0, sl).wait()

    zero = jnp.int32(0)
    g = build(zero, slot)
    stage[slot] = g.astype(BF16)
    for e in range(ne):
        stage[slot, e * win:e * win + al, :] = (g[e * win:e * win + al] + pend[e]).astype(BF16)

    tgt16 = _dot(ex16_ref[...], jnp.concatenate([lp_t, shift16], axis=0).astype(BF16))
    oh16 = jnp.where(tgt16 == row16_ref[...], 1.0, 0.0).astype(BF16)
    new_pend = _dot(oh16, h2_ref[...])
    for e in range(ne):
        keep = jnp.where(below(e) + cnt_sm[i * ne + e] < al, 1.0, 0.0)
        pend[e] = new_pend[e * al:(e + 1) * al] + keep * pend[e]

    @pl.when(i > 0)
    def _():
        wait_all(1 - slot)

    issue(zero, slot)

    def extra(k, carry):
        wait_all(slot)
        stage[slot] = build(k, slot).astype(BF16)
        issue(k, slot)
        return carry

    lax.fori_loop(1, rounds_sm[i], extra, 0)

    @pl.when(i == nt - 1)
    def _():
        wait_all(slot)
        stage[1 - slot, 0:win, :] = jnp.zeros((win, D), BF16)
        for e in range(ne):
            pltpu.make_async_copy(
                stage.at[1 - slot, pl.ds(0, win)], xe_ref.at[e, pl.ds(cap, win)], sem.at[1 - slot]).start()
        wait_all(1 - slot)


def _gather(h2b, lpos_t, tstart, cnt, rounds, cap, tile_off):
    n = lpos_t.shape[1]
    nt = n // TM
    ne, win = N_EXPERTS, MOE_WIN
    consts = _gather_consts()
    grid_spec = pltpu.PrefetchScalarGridSpec(
        num_scalar_prefetch=3,
        grid=(nt,),
        in_specs=[pl.BlockSpec((TM, D), lambda i, a, b, c: (tile_off + i, 0)),
                  pl.BlockSpec((ne, TM), lambda i, a, b, c: (0, i))]
                 + [pl.BlockSpec(t.shape, lambda i, a, b, c: (0, 0)) for t in consts],
        out_specs=pl.BlockSpec(memory_space=pl.ANY),
        scratch_shapes=[pltpu.VMEM((2, ne * win, D), BF16), pltpu.VMEM((ne, MOE_ALIGN, D), F32),
                        pltpu.SemaphoreType.DMA((2,))],
    )
    return pl.pallas_call(
        functools.partial(_gather_kernel, nt=nt, cap=cap),
        out_shape=jax.ShapeDtypeStruct((ne, cap + win, D), BF16),
        grid_spec=grid_spec,
        compiler_params=_cparams(("arbitrary",)),
        name="moe_gather",
    )(tstart, cnt, rounds, h2b, lpos_t, *consts)


FFN_TILES_CTX = (CAP_FACTOR * R_CTX // N_EXPERTS) // FFN_TM
FFN_TILES_LAT = (CAP_FACTOR * R_LAT // N_EXPERTS) // FFN_TM


def _ffn_kernel(xc_ref, xl_ref, wg_ref, wu_ref, wd_ref, yc_ref, yl_ref, wbuf, wg_s, wu_s, wd_s, sem, *, layer):
    e = pl.program_id(0)
    j = pl.program_id(1)
    slot = lax.rem(e, 2)

    def wcopy(k, w_ref, expert, sl):
        return pltpu.make_async_copy(w_ref.at[layer, expert], wbuf.at[sl, k], sem.at[sl, k])

    def start_weights(expert, sl):
        for k, w_ref in enumerate((wg_ref, wu_ref, wd_ref)):
            wcopy(k, w_ref, expert, sl).start()

    @pl.when(j == 0)
    def _():
        @pl.when(e == 0)
        def _():
            start_weights(e, slot)

        for k, w_ref in enumerate((wg_ref, wu_ref, wd_ref)):
            wcopy(k, w_ref, e, slot).wait()
        wg_s[...] = wbuf[slot, 0].astype(BF16)
        wu_s[...] = wbuf[slot, 1].astype(BF16)
        wd_s[...] = wbuf[slot, 2].astype(BF16)

        @pl.when(e + 1 < N_EXPERTS)
        def _():
            start_weights(e + 1, 1 - slot)

    x = jnp.where(j < FFN_TILES_CTX, xc_ref[0], xl_ref[0])
    hid = (_silu(_dot(x, wg_s[...])) * _dot(x, wu_s[...])).astype(BF16)
    y = _dot(hid, wd_s[...]).astype(BF16)

    @pl.when(j < FFN_TILES_CTX)
    def _():
        yc_ref[0] = y

    @pl.when(j >= FFN_TILES_CTX)
    def _():
        yl_ref[0] = y


def _ffn(xe_ctx, xe_lat, wg, wu, wd, layer):
    ne = N_EXPERTS
    blk = (1, FFN_TM, D)
    wspec = pl.BlockSpec(memory_space=pl.ANY)
    ctx_map = lambda e, j: (e, jnp.minimum(j, FFN_TILES_CTX - 1), 0)
    lat_map = lambda e, j: (e, jnp.maximum(j - FFN_TILES_CTX, 0), 0)
    return pl.pallas_call(
        functools.partial(_ffn_kernel, layer=layer),
        out_shape=[jax.ShapeDtypeStruct((ne, FFN_TILES_CTX * FFN_TM, D), BF16),
                   jax.ShapeDtypeStruct((ne, FFN_TILES_LAT * FFN_TM, D), BF16)],
        grid=(ne, FFN_TILES_CTX + FFN_TILES_LAT),
        in_specs=[pl.BlockSpec(blk, ctx_map), pl.BlockSpec(blk, lat_map), wspec, wspec, wspec],
        out_specs=[pl.BlockSpec(blk, ctx_map), pl.BlockSpec(blk, lat_map)],
        scratch_shapes=[pltpu.VMEM((2, 3, D, D), F32)] + [pltpu.VMEM((D, D), BF16)] * 3
                       + [pltpu.SemaphoreType.DMA((2, 3))],
        compiler_params=_cparams(("arbitrary", "arbitrary")),
        name="moe_ffn",
    )(xe_ctx, xe_lat, wg, wu, wd)


def _combine_kernel(tstart_sm, rounds_sm, lpos_ref, aff_ref, x_ref, mod_ref, fg_ref, ye_ref, o_ref,
                    ybuf, acc_s, sem, *, nt, cap, final):
    i = pl.program_id(0)
    slot = lax.rem(i, COMBINE_BUFS)
    ne, w, win, al = N_EXPERTS, MOE_W, MOE_WIN, MOE_ALIGN
    m = ne * win

    def win_start(tile, e, k):
        first = tstart_sm[tile * ne + e] + k * w
        return jnp.minimum((first >> 4) << 4, cap - win)

    def copy(e, start, sl):
        return pltpu.make_async_copy(
            ye_ref.at[e, pl.ds(pl.multiple_of(start, al), win)],
            ybuf.at[sl, pl.ds(e * win, win)],
            sem.at[sl])

    def fetch(tile, k, sl):
        for e in range(ne):
            copy(e, win_start(tile, e, k), sl).start(priority=e % 2)

    def wait_all(sl):
        for e in range(ne):
            copy(e, 0, sl).wait()

    zero = jnp.int32(0)

    ahead = COMBINE_BUFS - 1

    @pl.when(i == 0)
    def _():
        for t in range(min(ahead, nt)):
            fetch(jnp.int32(t), zero, jnp.int32(t))

    wait_all(slot)

    @pl.when(i + ahead < nt)
    def _():
        fetch(i + ahead, zero, lax.rem(i + ahead, COMBINE_BUFS))

    e_lane, r_lane = _win_index(_iota((1, m), 1))
    expand = jnp.where(_win_index(_iota((ne, m), 1))[0] == _iota((ne, m), 0), 1.0, 0.0).astype(BF16)
    aff = aff_ref[...]
    lpx = _dot(lpos_ref[...].astype(BF16), expand)
    affx = _dot(aff.astype(BF16), expand)
    rl = r_lane.astype(F32)

    def compute(k, sl):
        shift = jnp.zeros((1, m), F32)
        for e in range(ne):
            sh = (tstart_sm[i * ne + e] - win_start(i, e, k)).astype(F32)
            shift = jnp.where(e_lane == e, sh, shift)
        lo = (k * w).astype(F32)
        hit = jnp.where(lpx + shift == rl, 1.0, 0.0) * jnp.where(lpx >= lo, 1.0, 0.0) * jnp.where(lpx < lo + w, 1.0, 0.0)
        return _dot((hit * affx).astype(BF16), ybuf[sl])

    acc_s[...] = compute(zero, slot)

    def extra(k, carry):
        fetch(i, k, slot)
        wait_all(slot)
        acc_s[...] = acc_s[...] + compute(k, slot)
        return carry

    lax.fori_loop(1, rounds_sm[i], extra, 0)
    x = x_ref[...] + mod_ref[0, 5:6, :] * acc_s[...]
    if final:
        x = _rms(x, fg_ref[...])
    o_ref[...] = x


def _combine(x, mod_l, lpos, aff, ye, tstart, rounds, fg, cap, tile_off, final):
    n = lpos.shape[0]
    nt = n // TM
    ne = N_EXPERTS
    grid_spec = pltpu.PrefetchScalarGridSpec(
        num_scalar_prefetch=2,
        grid=(nt,),
        in_specs=[pl.BlockSpec((TM, ne), lambda i, a, b: (i, 0)),
                  pl.BlockSpec((TM, ne), lambda i, a, b: (tile_off + i, 0)),
                  pl.BlockSpec((TM, D), lambda i, a, b: (tile_off + i, 0)),
                  pl.BlockSpec((1, 6, D), lambda i, a, b: (_mod_row(tile_off + i), 0, 0)),
                  pl.BlockSpec((1, D), lambda i, a, b: (0, 0)),
                  pl.BlockSpec(memory_space=pl.ANY)],
        out_specs=pl.BlockSpec((TM, D), lambda i, a, b: (i, 0)),
        scratch_shapes=[pltpu.VMEM((COMBINE_BUFS, ne * MOE_WIN, D), BF16), pltpu.VMEM((TM, D), F32),
                        pltpu.SemaphoreType.DMA((COMBINE_BUFS,))],
    )
    return pl.pallas_call(
        functools.partial(_combine_kernel, nt=nt, cap=cap, final=final),
        out_shape=jax.ShapeDtypeStruct((n, D), F32),
        grid_spec=grid_spec,
        compiler_params=_cparams(("arbitrary",)),
        name="moe_combine",
    )(tstart, rounds, lpos, aff, x, mod_l, fg, ye)


def _moe_layer(x, mod_l, h2b, aff, afft, wg, wu, wd, fg, layer):
    final = layer == DEPTH - 1
    groups = []
    for n, tile_off, afft_g in ((R_CTX, 0, afft[:, :R_CTX]), (R_LAT, NT_CTX, afft[:, R_CTX:])):
        cap = (CAP_FACTOR * n) // N_EXPERTS
        lpos_t, lpos, tstart, cnt = _select(afft_g, cap)
        tstart = tstart.reshape(-1)
        rounds = jnp.maximum((jnp.max(cnt.reshape(-1, N_EXPERTS), axis=1) + MOE_W - 1) // MOE_W, 1).astype(jnp.int32)
        xe = _gather(h2b, lpos_t, tstart, cnt.reshape(-1), rounds, cap, tile_off)
        groups.append((lpos, tstart, rounds, cap, tile_off, xe))
    ye = _ffn(groups[0][-1], groups[1][-1], wg, wu, wd, layer)
    return [_combine(x, mod_l, lpos, aff, ye_g, tstart, rounds, fg, cap, tile_off, final)
            for (lpos, tstart, rounds, cap, tile_off, _), ye_g in zip(groups, ye)]


def _layout_w_in(w):
    o = np.cumsum([0, 256, 256, 256, 384, 256, 32, 256, 256, 256, 256, 256, 128, 128, 8, 4096])
    wb = w.astype(BF16)
    zeros = lambda n: jnp.zeros((D, n), BF16)
    w_all = jnp.concatenate([wb[:, o[0]:o[5]], zeros(64), wb[:, o[5]:o[6]], zeros(32), wb[:, o[6]:o[14]],
                             zeros(120), wb[:, o[14]:o[15]]], axis=1)
    return w_all, wb[:, o[13]:o[14]].T


def _layout_mla(w_uq, w_ukv):
    uq = w_uq.reshape(MLA_Q_RANK, N_HEADS, MLA_NOPE + MLA_ROPE)
    uq = jnp.pad(uq, ((0, 0), (0, 0), (0, LANES - MLA_NOPE - MLA_ROPE))).reshape(MLA_Q_RANK, N_HEADS * LANES)
    ukv = w_ukv.reshape(MLA_KV_RANK, N_HEADS, MLA_NOPE + MLA_V)
    uk = jnp.pad(ukv[:, :, :MLA_NOPE], ((0, 0), (0, 0), (0, LANES - MLA_NOPE))).reshape(MLA_KV_RANK, N_HEADS * LANES)
    uv = jnp.pad(ukv[:, :, MLA_NOPE:], ((0, 0), (0, 0), (0, LANES - MLA_V))).reshape(MLA_KV_RANK, N_HEADS * LANES)
    return uq.astype(BF16), uk.astype(BF16), uv.astype(BF16)


def kernel(x_prompt, x_sample, cache_na_k, cache_na_v, cache_mla_ckv, cache_mla_krope, cache_df_k, cache_df_v,
           state_ssm, c, c_ctx, mod_w, mod_b, norm1_g, norm2_g, w_in, na_rel_bias, mla_q_norm_g, mla_kv_norm_g,
           mla_w_uq, mla_w_ukv, df_lambda, df_subln_g, ssm_conv_w, ssm_conv_b, ssm_dt_bias, ssm_a_log, ssm_d,
           ssm_norm_g, w_branch, w_out, router_w, exp_w_gate, exp_w_up, exp_w_down, final_norm_g):
    x_ctx = x_prompt.reshape(R_CTX, D)
    x_lat = x_sample.reshape(R_LAT, D)
    cvec = jnp.concatenate([c_ctx[None, :], c, jnp.zeros((16 - 1 - B_LAT, D), F32)], axis=0)
    mod = _modulation(cvec, mod_w, mod_b).reshape(DEPTH, 16, 6, D)

    cos32, sa32, sb32 = _rope_tables32()
    pad128 = lambda a, fill: np.pad(a, ((0, 0), (64, 32)), constant_values=fill)
    rope128 = (pad128(cos32, 1.0), pad128(sa32, 0.0), pad128(sb32, 0.0))
    rope256 = tuple(np.tile(a, (1, 8)) for a in (cos32, sa32, sb32))
    fg = final_norm_g.reshape(1, D)
    cna_k = cache_na_k.reshape(B_LAT, DEPTH, PAST, 256)
    cna_v = cache_na_v.reshape(B_LAT, DEPTH, PAST, 256)
    cdf_k = cache_df_k.reshape(B_LAT, DEPTH, PAST, 256)
    cdf_v = cache_df_v.reshape(B_LAT, DEPTH, PAST, 256)

    outs = {k: [] for k in ("na_k", "na_v", "ckv", "krope", "df_k", "df_v", "ssm")}
    for l in range(DEPTH):
        mod_l = mod[l]
        (u_na, u_mla, u_df, u_ssm, gates, dt_t, na_k, na_v, df_k, df_v, krope) = _in_proj(
            x_ctx, x_lat, mod_l, norm1_g[l].reshape(1, D), *_layout_w_in(w_in[l]))
        bias = _na_bias_table(na_rel_bias[l])
        br_na = (_attn_ctx(u_na), _na_lat(u_na, cna_k, cna_v, bias, l))
        wuq, wuk, wuv = _layout_mla(mla_w_uq[l], mla_w_ukv[l])
        gq = mla_q_norm_g[l].reshape(1, MLA_Q_RANK)
        gkv = mla_kv_norm_g[l].reshape(1, MLA_KV_RANK)
        mla_c, ckv_new = _mla(u_mla, gq, gkv, wuq, wuk, wuv)
        kr_ctx = jnp.pad(cache_mla_krope[:, l], ((0, 0), (0, 0), (64, 32)))
        (mla_l,) = _mla(u_mla, gq, gkv, wuq, wuk, wuv, rope128, cache_mla_ckv, kr_ctx, l)
        lam_init = 0.8 - 0.6 * math.exp(-0.3 * l)
        gs = df_subln_g[l].reshape(1, DF_V)
        br_df = (_df(u_df, df_lambda[l], gs, lam_init),
                 _df(u_df, df_lambda[l], gs, lam_init, rope256, cdf_k, cdf_v, l))
        d_vec = jnp.repeat(ssm_d[l], SSM_P).reshape(1, 256)
        gn = ssm_norm_g[l].reshape(1, 256)
        ssm_c, st_new = _ssd(u_ssm, dt_t, ssm_conv_w[l], ssm_conv_b[l], ssm_dt_bias[l], ssm_a_log[l], d_vec, gn)
        (ssm_l,) = _ssd(u_ssm, dt_t, ssm_conv_w[l], ssm_conv_b[l], ssm_dt_bias[l], ssm_a_log[l], d_vec, gn,
                        state_ssm, l)
        x_mid, h2b, aff, afft = _merge(x_ctx, x_lat, mod_l, (br_na, (mla_c, mla_l), br_df, (ssm_c, ssm_l)), gates,
                                       w_branch[l].astype(BF16), w_out[l].astype(BF16),
                                       norm2_g[l].reshape(1, D), router_w[l])
        x_ctx, x_lat = _moe_layer(x_mid, mod_l, h2b, aff, afft, exp_w_gate, exp_w_up, exp_w_down, fg, l)
        outs["na_k"].append(na_k.reshape(B_CTX, L_CTX, N_HEADS, 64))
        outs["na_v"].append(na_v.reshape(B_CTX, L_CTX, N_HEADS, 64))
        outs["ckv"].append(ckv_new.reshape(B_CTX, L_CTX, MLA_KV_RANK))
        outs["krope"].append(krope.reshape(B_CTX, L_CTX, MLA_ROPE))
        outs["df_k"].append(df_k.reshape(B_CTX, L_CTX, N_HEADS, 2, DF_HD))
        outs["df_v"].append(df_v.reshape(B_CTX, L_CTX, N_HEADS, DF_V))
        outs["ssm"].append(st_new)
    stack = lambda k: jnp.stack(outs[k], axis=1)
    return (x_ctx.reshape(B_CTX, L_CTX, D), x_lat.reshape(B_LAT, L_LAT, D), stack("na_k"), stack("na_v"),
            stack("ckv"), stack("krope"), stack("df_k"), stack("df_v"), stack("ssm"))
```
